```python
import math
import jax
import jax.numpy as jnp
from jax import lax
import numpy as np

D_MODEL = 1024
BATCH = 8
SEQ = 2048
DEPTH = 4
DEC_BATCH = 128
DEC_SEQ = 8
PAST_LEN = 2048
PAGE_SIZE = 128

N_EVEN = (DEPTH + 1) // 2
N_ODD = DEPTH // 2
PLE_DIM = 256
D_FF = 4 * D_MODEL
HG_WIDTH = D_MODEL // 2
HG_HEADS = 4
HG_DK = HG_WIDTH // HG_HEADS
HG_DV = HG_DK
HG_CHUNK = 64
DA_WIDTH = D_MODEL - HG_WIDTH
DA_HEADS = 4
DA_DV = DA_WIDTH // DA_HEADS
DA_DH = DA_DV // 2
DA_QK = DA_HEADS * 2 * DA_DH
Q_BLOCK = 128
IN_SPLITS = [HG_WIDTH, 2 * HG_WIDTH, 3 * HG_WIDTH, 4 * HG_WIDTH,
             4 * HG_WIDTH + DA_QK, 4 * HG_WIDTH + 2 * DA_QK]
IN_COLS = 4 * HG_WIDTH + 2 * DA_QK + DA_WIDTH
RW_HEAD = 64
RW_HEADS = D_MODEL // RW_HEAD
RW_DECAY_LORA = 64
RW_A_LORA = 64
RW_V_LORA = 32
RW_GATE_LORA = 160
ALPHA = (2.0 * DEPTH) ** 0.25
BETA = (8.0 * DEPTH) ** -0.25
LN_EPS = 1e-5
RMS_EPS = 1e-6
GN_EPS = 64e-5
F_MIN = 1e-30
NEG_BIG = -1e30

kernel_name = 'hybrid_hgrn2_diffattn_rwkv7_step'


def layer_norm(x, g, b):
    xf = x.astype(jnp.float32)
    mu = jnp.mean(xf, axis=-1, keepdims=True)
    var = jnp.mean(jnp.square(xf - mu), axis=-1, keepdims=True)
    y = (xf - mu) * lax.rsqrt(var + LN_EPS) * g.astype(jnp.float32) + b.astype(jnp.float32)
    return y.astype(x.dtype)


def rms_norm(x, g):
    xf = x.astype(jnp.float32)
    return xf * lax.rsqrt(jnp.mean(xf * xf, axis=-1, keepdims=True) + RMS_EPS) * g.astype(jnp.float32)


def head_group_norm(y, g, b):
    mu = jnp.mean(y, axis=-1, keepdims=True)
    var = jnp.mean(jnp.square(y - mu), axis=-1, keepdims=True)
    gh = g.reshape(RW_HEADS, RW_HEAD).astype(jnp.float32)
    bh = b.reshape(RW_HEADS, RW_HEAD).astype(jnp.float32)
    return (y - mu) * lax.rsqrt(var + GN_EPS) * gh + bh


def gather_pages(pool, page_table):
    g = pool[page_table]
    return g.reshape((g.shape[0], g.shape[1] * g.shape[2]) + g.shape[3:])


def hgrn2_chunkwise(q, k, v, log_f, s0):
    B, L, H, DK = q.shape
    C = math.gcd(L, HG_CHUNK)
    nc = L // C

    def to_chunks(t):
        return t.reshape(B, nc, C, H, t.shape[-1]).transpose(1, 0, 2, 3, 4)

    causal = jnp.tril(jnp.ones((C, C), dtype=bool))[None, :, :, None, None]

    def step(S, inp):
        qc, kc, vc, gc = inp
        b = jnp.cumsum(gc, axis=1)
        inter = jnp.einsum('bthd,bhdv->bthv', qc * jnp.exp(b), S)
        rel = b[:, :, None] - b[:, None, :]
        decay = jnp.where(causal, jnp.exp(jnp.where(causal, rel, 0.0)), 0.0)
        scores = jnp.einsum('bthd,btshd,bshd->bhts', qc, decay, kc)
        intra = jnp.einsum('bhts,bshv->bthv', scores, vc)
        b_last = b[:, -1]
        S = jnp.exp(b_last)[..., None] * S + jnp.einsum(
            'bshd,bshv->bhdv', kc * jnp.exp(b_last[:, None] - b), vc)
        return S, inter + intra

    s_fin, o = lax.scan(step, s0, (to_chunks(q), to_chunks(k), to_chunks(v), to_chunks(log_f)))
    o = o.transpose(1, 0, 2, 3, 4).reshape(B, L, H, v.shape[-1])
    return o, s_fin


def hgrn2_mix(hq, hf, hi, hg, lb, norm_g, s0):
    B, L, _ = hq.shape

    def heads(t):
        return t.reshape(B, L, HG_HEADS, HG_DK).astype(jnp.float32)

    q = jax.nn.silu(heads(hq))
    lbh = lb.reshape(HG_HEADS, HG_DK)
    f = lbh + (1.0 - lbh) * jax.nn.sigmoid(heads(hf))
    log_f = jnp.log(jnp.maximum(f, F_MIN))
    k = 1.0 - f
    o, s = hgrn2_chunkwise(q, k, heads(hi), log_f, s0.astype(jnp.float32))
    o = rms_norm(o, norm_g) * jax.nn.sigmoid(heads(hg))
    return o.reshape(B, L, HG_WIDTH).astype(hq.dtype), s.astype(s0.dtype)


def diff_attn_block(q, k, v, q_start, lam):
    T = q.shape[1]
    L = k.shape[1]
    s = jnp.einsum('bthcd,blhcd->bhctl', q, k).astype(jnp.float32) * (DA_DH ** -0.5)
    allowed = jnp.arange(L)[None, :] <= (q_start + jnp.arange(T))[:, None]
    s = jnp.where(allowed, s, NEG_BIG)
    p = jax.nn.softmax(s, axis=-1)
    w = p[:, :, 0] - lam * p[:, :, 1]
    return jnp.einsum('bhtl,blhv->bthv', w.astype(v.dtype), v)


def diff_attn_prompt(q, k, v, lam):
    B, S = q.shape[0], q.shape[1]
    qb = math.gcd(S, Q_BLOCK)
    nb = S // qb
    q_blocks = q.reshape(B, nb, qb, DA_HEADS, 2, DA_DH).swapaxes(0, 1)
    starts = jnp.arange(nb) * qb
    out = lax.map(lambda a: diff_attn_block(a[0], k, v, a[1], lam), (q_blocks, starts))
    return out.swapaxes(0, 1).reshape(B, S, DA_HEADS, DA_DV)


def even_mixer(x, hg_s0, past_k, past_v, lb, lam, lam_init, W, j):
    B, L, _ = x.shape
    proj = x @ W['w_in_even'][j]
    hq, hf, hi, hg, dq, dk, dv = jnp.split(proj, IN_SPLITS, axis=-1)
    o_h, s_hg = hgrn2_mix(hq, hf, hi, hg, lb, W['hg_norm_g'][j], hg_s0)
    q = dq.reshape(B, L, DA_HEADS, 2, DA_DH)
    k = dk.reshape(B, L, DA_HEADS, 2, DA_DH)
    v = dv.reshape(B, L, DA_HEADS, DA_DV)
    if past_k is None:
        o_d = diff_attn_prompt(q, k, v, lam)
    else:
        k_all = jnp.concatenate([past_k.astype(k.dtype), k], axis=1)
        v_all = jnp.concatenate([past_v.astype(v.dtype), v], axis=1)
        o_d = diff_attn_block(q, k_all, v_all, past_k.shape[1], lam)
    o_d = (rms_norm(o_d, W['da_norm_g'][j]) * (1.0 - lam_init)).reshape(B, L, DA_WIDTH).astype(x.dtype)
    out = jnp.concatenate([o_h, o_d], axis=-1) @ W['w_out_even'][j]
    return out, s_hg, k, v


def rwkv7_scan(r, w, k, v, a, b, s0):
    def step(S, inp):
        rt, wt, kt, vt, at, bt = inp
        sa = jnp.einsum('bhij,bhj->bhi', S, at)
        S = S * wt[:, :, None, :] + sa[..., None] * bt[:, :, None, :] + vt[..., None] * kt[:, :, None, :]
        return S, jnp.einsum('bhij,bhj->bhi', S, rt)

    seq = tuple(t.swapaxes(0, 1) for t in (r, w, k, v, a, b))
    S, y = lax.scan(step, s0, seq)
    return y.swapaxes(0, 1), S


def rwkv7_mixer(x, shift0, wkv0, v_first, W, j):
    B, L, D = x.shape
    x_prev = jnp.concatenate([shift0[:, None, :].astype(x.dtype), x[:, :-1]], axis=1)
    xx = x_prev - x
    mix = W['rw_mix'][j]
    xr, xw, xk, xv, xa, xg = (x + xx * mix[m] for m in range(6))
    r = xr @ W['rw_w_r'][j]
    k = xk @ W['rw_w_k'][j]
    v = xv @ W['rw_w_v'][j]
    w = -jax.nn.softplus(-(W['rw_w0'][j] + jnp.tanh(xw @ W['rw_w1'][j]) @ W['rw_w2'][j])) - 0.5
    if j == 0:
        v_first = v
    else:
        v = v + (v_first - v) * jax.nn.sigmoid(
            W['rw_v0'][j - 1] + (xv @ W['rw_v1'][j - 1]) @ W['rw_v2'][j - 1])
    a = jax.nn.sigmoid(W['rw_a0'][j] + (xa @ W['rw_a1'][j]) @ W['rw_a2'][j])
    g = jax.nn.sigmoid(xg @ W['rw_g1'][j]) @ W['rw_g2'][j]

    def heads(t):
        return t.reshape(B, L, RW_HEADS, RW_HEAD).astype(jnp.float32)

    kk = heads(k * W['rw_k_k'][j])
    kk = kk / jnp.maximum(jnp.sqrt(jnp.sum(kk * kk, axis=-1, keepdims=True)), 1e-12)
    k = k * (1.0 + (a - 1.0) * W['rw_k_a'][j])
    rh, kh, vh, ah = heads(r), heads(k), heads(v), heads(a)
    decay = jnp.exp(-jnp.exp(heads(w)))
    y, s = rwkv7_scan(rh, decay, kh, vh, -kk, kk * ah, wkv0.astype(jnp.float32))
    y = head_group_norm(y, W['rw_lnx_g'][j], W['rw_lnx_b'][j])
    y = y + jnp.sum(rh * kh * W['rw_r_k'][j].astype(jnp.float32), axis=-1, keepdims=True) * vh
    out = (y.reshape(B, L, D).astype(x.dtype) * g) @ W['rw_w_o'][j]
    return out, x[:, -1], s.astype(wkv0.dtype), v_first


def trunk(x, p, hg_s0, wkv_s0, shift0, cache_k, cache_v, page_table, W):
    lb_sm = jax.nn.softmax(W['hg_lb_logits'].astype(jnp.float32), axis=0)
    lower_bounds = jnp.cumsum(lb_sm, axis=0) - lb_sm[0]
    new_k, new_v, new_hg, new_wkv, new_shift = [], [], [], [], []
    v_first = None
    for i in range(DEPTH):
        j = i // 2
        if i % 2 == 0:
            if page_table is None:
                past_k = None
                past_v = None
            else:
                past_k = gather_pages(cache_k[j], page_table)
                past_v = gather_pages(cache_v[j], page_table)
            lam_init = 0.8 - 0.6 * math.exp(-0.3 * i)
            lam = (jnp.exp(jnp.sum(W['da_lam_q1'][j] * W['da_lam_k1'][j]).astype(jnp.float32))
                   - jnp.exp(jnp.sum(W['da_lam_q2'][j] * W['da_lam_k2'][j]).astype(jnp.float32))
                   + lam_init)
            h, s_hg, k_rows, v_rows = even_mixer(x, hg_s0[j], past_k, past_v, lower_bounds[j],
                                                 lam, lam_init, W, j)
            new_k.append(k_rows)
            new_v.append(v_rows)
            new_hg.append(s_hg)
        else:
            h, sh, s_wkv, v_first = rwkv7_mixer(x, shift0[j], wkv_s0[j], v_first, W, j)
            new_shift.append(sh)
            new_wkv.append(s_wkv)
        x = layer_norm(ALPHA * x + h, W['ln1_g'][i], W['ln1_b'][i])
        hid = jax.nn.relu(x @ W['mlp_up'][i])
        x = layer_norm(ALPHA * x + (hid * hid) @ W['mlp_down'][i], W['ln2_g'][i], W['ln2_b'][i])
        x = x + jax.nn.sigmoid(x @ W['ple_gate'][i]) * (p[i] @ W['ple_proj'][i])
    return (x, jnp.stack(new_k), jnp.stack(new_v), jnp.stack(new_hg),
            jnp.stack(new_wkv), jnp.stack(new_shift))


def setup_inputs(seed: int = 0) -> dict:
    key = jax.random.key(seed)
    ks = iter(jax.random.split(key, 64))

    def nrm(shape, scale=1.0):
        return jax.random.normal(next(ks), shape, jnp.float32) * scale

    def unif(shape, lo, hi):
        return jax.random.uniform(next(ks), shape, jnp.float32, lo, hi)

    def gain(shape):
        return 1.0 + nrm(shape, 0.02)

    n_pages = PAST_LEN // PAGE_SIZE
    n_used = DEC_BATCH * n_pages
    n_phys = n_used + max(1, n_used // 4)
    page_table = jax.random.permutation(next(ks), n_phys)[:n_used].reshape(DEC_BATCH, n_pages).astype(jnp.int32)
    d = D_MODEL
    rd = d ** -0.5
    return {
        'x_prompt': nrm((BATCH, SEQ, d)),
        'x_sample': nrm((DEC_BATCH, DEC_SEQ, d)),
        'cache_k': nrm((N_EVEN, n_phys, PAGE_SIZE, DA_HEADS, 2, DA_DH)),
        'cache_v': nrm((N_EVEN, n_phys, PAGE_SIZE, DA_HEADS, DA_DV)),
        'state_hgrn': nrm((N_EVEN, DEC_BATCH, HG_HEADS, HG_DK, HG_DV)),
        'state_wkv': nrm((N_ODD, DEC_BATCH, RW_HEADS, RW_HEAD, RW_HEAD), 0.5),
        'state_shift': nrm((N_ODD, DEC_BATCH, d)),
        'page_table': page_table,
        'p_prompt': nrm((DEPTH, BATCH, SEQ, PLE_DIM)),
        'p_sample': nrm((DEPTH, DEC_BATCH, DEC_SEQ, PLE_DIM)),
        'w_in_even': nrm((N_EVEN, d, IN_COLS), rd),
        'w_out_even': nrm((N_EVEN, d, d), rd * BETA),
        'hg_lb_logits': nrm((N_EVEN, HG_WIDTH), 0.5),
        'hg_norm_g': gain((N_EVEN, HG_DV)),
        'da_lam_q1': nrm((N_EVEN, DA_DH), 0.1),
        'da_lam_k1': nrm((N_EVEN, DA_DH), 0.1),
        'da_lam_q2': nrm((N_EVEN, DA_DH), 0.1),
        'da_lam_k2': nrm((N_EVEN, DA_DH), 0.1),
        'da_norm_g': gain((N_EVEN, DA_DV)),
        'rw_mix': unif((N_ODD, 6, d), 0.0, 1.0),
        'rw_w_r': nrm((N_ODD, d, d), rd),
        'rw_w_k': nrm((N_ODD, d, d), rd),
        'rw_w_v': nrm((N_ODD, d, d), rd),
        'rw_w_o': nrm((N_ODD, d, d), rd * BETA),
        'rw_w0': unif((N_ODD, d), -6.5, -1.5),
        'rw_w1': nrm((N_ODD, d, RW_DECAY_LORA), rd),
        'rw_w2': nrm((N_ODD, RW_DECAY_LORA, d), 0.5 * RW_DECAY_LORA ** -0.5),
        'rw_a0': nrm((N_ODD, d), 0.1),
        'rw_a1': nrm((N_ODD, d, RW_A_LORA), rd),
        'rw_a2': nrm((N_ODD, RW_A_LORA, d), 0.5 * RW_A_LORA ** -0.5),
        'rw_v0': 1.0 + nrm((N_ODD - 1, d), 0.1),
        'rw_v1': nrm((N_ODD - 1, d, RW_V_LORA), rd),
        'rw_v2': nrm((N_ODD - 1, RW_V_LORA, d), 0.5 * RW_V_LORA ** -0.5),
        'rw_g1': nrm((N_ODD, d, RW_GATE_LORA), rd),
        'rw_g2': nrm((N_ODD, RW_GATE_LORA, d), RW_GATE_LORA ** -0.5),
        'rw_k_k': 0.85 + nrm((N_ODD, d), 0.02),
        'rw_k_a': gain((N_ODD, d)),
        'rw_r_k': nrm((N_ODD, RW_HEADS, RW_HEAD), 0.1),
        'rw_lnx_g': gain((N_ODD, d)),
        'rw_lnx_b': nrm((N_ODD, d), 0.02),
        'ln1_g': gain((DEPTH, d)),
        'ln1_b': nrm((DEPTH, d), 0.02),
        'ln2_g': gain((DEPTH, d)),
        'ln2_b': nrm((DEPTH, d), 0.02),
        'mlp_up': nrm((DEPTH, d, D_FF), rd),
        'mlp_down': nrm((DEPTH, D_FF, d), D_FF ** -0.5 * BETA),
        'ple_proj': nrm((DEPTH, PLE_DIM, d), PLE_DIM ** -0.5),
        'ple_gate': nrm((DEPTH, d, d), rd),
    }


def reference(x_prompt, x_sample, cache_k, cache_v, state_hgrn, state_wkv, state_shift, page_table,
              p_prompt, p_sample, w_in_even, w_out_even, hg_lb_logits, hg_norm_g,
              da_lam_q1, da_lam_k1, da_lam_q2, da_lam_k2, da_norm_g,
              rw_mix, rw_w_r, rw_w_k, rw_w_v, rw_w_o, rw_w0, rw_w1, rw_w2,
              rw_a0, rw_a1, rw_a2, rw_v0, rw_v1, rw_v2, rw_g1, rw_g2,
              rw_k_k, rw_k_a, rw_r_k, rw_lnx_g, rw_lnx_b,
              ln1_g, ln1_b, ln2_g, ln2_b, mlp_up, mlp_down, ple_proj, ple_gate):
    W = dict(w_in_even=w_in_even, w_out_even=w_out_even, hg_lb_logits=hg_lb_logits,
             hg_norm_g=hg_norm_g, da_lam_q1=da_lam_q1, da_lam_k1=da_lam_k1,
             da_lam_q2=da_lam_q2, da_lam_k2=da_lam_k2, da_norm_g=da_norm_g,
             rw_mix=rw_mix, rw_w_r=rw_w_r, rw_w_k=rw_w_k, rw_w_v=rw_w_v, rw_w_o=rw_w_o,
             rw_w0=rw_w0, rw_w1=rw_w1, rw_w2=rw_w2, rw_a0=rw_a0, rw_a1=rw_a1, rw_a2=rw_a2,
             rw_v0=rw_v0, rw_v1=rw_v1, rw_v2=rw_v2, rw_g1=rw_g1, rw_g2=rw_g2,
             rw_k_k=rw_k_k, rw_k_a=rw_k_a, rw_r_k=rw_r_k, rw_lnx_g=rw_lnx_g, rw_lnx_b=rw_lnx_b,
             ln1_g=ln1_g, ln1_b=ln1_b, ln2_g=ln2_g, ln2_b=ln2_b,
             mlp_up=mlp_up, mlp_down=mlp_down, ple_proj=ple_proj, ple_gate=ple_gate)
    B = x_prompt.shape[0]
    dt = x_prompt.dtype
    hg0 = jnp.zeros((N_EVEN, B, HG_HEADS, HG_DK, HG_DV), dt)
    wkv0 = jnp.zeros((N_ODD, B, RW_HEADS, RW_HEAD, RW_HEAD), dt)
    sh0 = jnp.zeros((N_ODD, B, D_MODEL), dt)
    y_prompt, k_prompt, v_prompt, hgrn_prompt, wkv_prompt, shift_prompt = trunk(
        x_prompt, p_prompt, hg0, wkv0, sh0, None, None, None, W)
    y_sample, k_sample, v_sample, hgrn_sample, wkv_sample, shift_sample = trunk(
        x_sample, p_sample, state_hgrn, state_wkv, state_shift, cache_k, cache_v, page_table, W)
    return (y_prompt, y_sample, k_prompt, v_prompt, k_sample, v_sample,
            hgrn_prompt, hgrn_sample, wkv_prompt, wkv_sample, shift_prompt, shift_sample)
```

```python
import functools
import math

import jax
import jax.numpy as jnp
from jax import lax
from jax.experimental import pallas as pl
from jax.experimental.pallas import tpu as pltpu

F32 = jnp.float32
BF16 = jnp.bfloat16

D_MODEL = 1024
DEPTH = 4
PAGE_SIZE = 128
N_EVEN = (DEPTH + 1) // 2
N_ODD = DEPTH // 2
PLE_DIM = 256
D_FF = 4 * D_MODEL
HG_WIDTH = D_MODEL // 2
HG_HEADS = 4
HG_DK = HG_WIDTH // HG_HEADS
DA_WIDTH = D_MODEL - HG_WIDTH
DA_HEADS = 4
DA_DV = DA_WIDTH // DA_HEADS
DA_DH = DA_DV // 2
DA_QK = DA_HEADS * 2 * DA_DH
IN_COLS = 4 * HG_WIDTH + 2 * DA_QK + DA_WIDTH
RW_HEAD = 64
RW_HEADS = D_MODEL // RW_HEAD
ALPHA = (2.0 * DEPTH) ** 0.25
LN_EPS = 1e-5
RMS_EPS = 1e-6
GN_EPS = 64e-5
F_MIN = 1e-30
NEG_BIG = -1e30

LANES = 128
SUBLANES = 8
VMEM_LIMIT = 52 * 1024 * 1024
HG_CHUNK = 16
MM_ROWS = 16


def _params(*sem):
    return pltpu.CompilerParams(dimension_semantics=sem, vmem_limit_bytes=VMEM_LIMIT)


def _const_spec(shape):
    nd = len(shape)
    return pl.BlockSpec(shape, lambda *_: (0,) * nd, pipeline_mode=pl.Buffered(1))


def _row_spec(rows, cols):
    return pl.BlockSpec((rows, cols), lambda i: (i, 0))


def _tile(n, pref):
    t = min(n, pref)
    while n % t:
        t //= 2
    return t


def _dot(a, b):
    return jnp.dot(a, b, preferred_element_type=F32)


def _dot_nt(a, b):
    return lax.dot_general(a, b, (((1,), (1,)), ((), ())), preferred_element_type=F32)


def _dot_tn(a, b):
    return lax.dot_general(a, b, (((0,), (0,)), ((), ())), preferred_element_type=F32)


def _split3(x):
    h = x.astype(BF16)
    r = x - h.astype(F32)
    m = r.astype(BF16)
    l = (r - m.astype(F32)).astype(BF16)
    return h, m, l


def _dot_x3(x, rhs_bf16):
    h, m, l = _split3(x)
    return _dot(h, rhs_bf16) + _dot(m, rhs_bf16) + _dot(l, rhs_bf16)


def _dot_x3_lhs(lhs_bf16, x):
    h, m, l = _split3(x)
    return _dot(lhs_bf16, h) + _dot(lhs_bf16, m) + _dot(lhs_bf16, l)


def _layer_norm(z, g, b):
    mu = jnp.mean(z, axis=-1, keepdims=True)
    zc = z - mu
    var = jnp.mean(zc * zc, axis=-1, keepdims=True)
    return zc * lax.rsqrt(var + LN_EPS) * g + b


def _sigmoid(x):
    return 1.0 / (1.0 + jnp.exp(-x))


def _head_ones(width):
    r = lax.broadcasted_iota(jnp.int32, (LANES, LANES), 0) // width
    c = lax.broadcasted_iota(jnp.int32, (LANES, LANES), 1) // width
    return jnp.where(r == c, 1.0, 0.0).astype(BF16)


def _segsum(x, ones):
    cols = x.shape[1] // LANES
    parts = [_dot_x3(x[:, c * LANES:(c + 1) * LANES], ones) for c in range(cols)]
    return parts[0] if cols == 1 else jnp.concatenate(parts, axis=1)


def _even_in_kernel(x_ref, w_ref, h4_ref, q_ref, k_ref, v_ref):
    xb = x_ref[...].astype(BF16)
    cw = HG_WIDTH
    for c in range(4):
        h4_ref[:, c * cw:(c + 1) * cw] = _dot(xb, w_ref[:, c * cw:(c + 1) * cw])
    q0 = 4 * HG_WIDTH
    q_ref[...] = _dot(xb, w_ref[:, q0:q0 + DA_QK])
    k_ref[...] = _dot(xb, w_ref[:, q0 + DA_QK:q0 + 2 * DA_QK])
    v_ref[...] = _dot(xb, w_ref[:, q0 + 2 * DA_QK:q0 + 2 * DA_QK + DA_WIDTH])


def _even_in(x, w_in_bf):
    n = x.shape[0]
    tm = _tile(n, 512)
    outs = (jax.ShapeDtypeStruct((n, 4 * HG_WIDTH), F32), jax.ShapeDtypeStruct((n, DA_QK), F32),
            jax.ShapeDtypeStruct((n, DA_QK), F32), jax.ShapeDtypeStruct((n, DA_WIDTH), F32))
    return pl.pallas_call(
        _even_in_kernel, out_shape=outs, grid=(n // tm,),
        in_specs=[_row_spec(tm, D_MODEL), _const_spec((D_MODEL, IN_COLS))],
        out_specs=(_row_spec(tm, 4 * HG_WIDTH), _row_spec(tm, DA_QK), _row_spec(tm, DA_QK),
                   _row_spec(tm, DA_WIDTH)),
        compiler_params=_params("parallel"), name="even_in")(x, w_in_bf)


def _pad_rows(x, rows):
    if x.shape[0] >= rows:
        return x
    return jnp.concatenate([x, jnp.zeros((rows - x.shape[0], x.shape[1]), x.dtype)], axis=0)


def _hgrn_kernel(h4_ref, s0_ref, lbl_ref, ng_ref, o_ref, sout_ref,
                 st_ref, q_s, k_s, b_s, *, layer, chunk, carry):
    tl = h4_ref.shape[0]
    nchunk = tl // chunk
    w = HG_WIDTH

    rows = [lbl_ref[i:i + 1, :] for i in range(N_EVEN)]
    mx = functools.reduce(jnp.maximum, rows)
    es = [jnp.exp(r - mx) for r in rows]
    den = functools.reduce(lambda a, b: a + b, es)
    sm = [e / den for e in es]
    cs = functools.reduce(lambda a, b: a + b, sm[:layer + 1])
    lb = cs - sm[0]

    hq = h4_ref[:, 0:w]
    hf = h4_ref[:, w:2 * w]
    q_s[...] = hq * _sigmoid(hq)
    f = lb + (1.0 - lb) * _sigmoid(hf)
    g = jnp.log(jnp.maximum(f, F_MIN))
    k_s[...] = 1.0 - f
    ri = lax.broadcasted_iota(jnp.int32, (tl, tl), 0)
    ci = lax.broadcasted_iota(jnp.int32, (tl, tl), 1)
    tril = jnp.where((ri // chunk == ci // chunk) & (ci <= ri), 1.0, 0.0).astype(BF16)
    b_s[...] = _dot_x3_lhs(tril, g)

    if carry:
        @pl.when(pl.program_id(1) == 0)
        def _():
            for h in range(HG_HEADS):
                st_ref[h] = s0_ref[0, h].T

    rowid = lax.broadcasted_iota(jnp.int32, (chunk, 1), 0)
    ng = ng_ref[...]

    def do_chunk(c, _):
        r0 = pl.multiple_of(c * chunk, chunk)
        rs = pl.ds(r0, chunk)
        for h in range(HG_HEADS):
            cs_ = slice(h * HG_DK, (h + 1) * HG_DK)
            q = q_s[rs, cs_]
            k = k_s[rs, cs_]
            b = b_s[rs, cs_]
            v = h4_ref[rs, 2 * w + h * HG_DK:2 * w + (h + 1) * HG_DK]
            hg = h4_ref[rs, 3 * w + h * HG_DK:3 * w + (h + 1) * HG_DK]
            if carry:
                st = st_ref[h]
            else:
                st = s0_ref[c, h].T
            qd = _pad_rows((q * jnp.exp(b)).astype(BF16), MM_ROWS)
            o = _dot_nt(qd, st.astype(BF16))[:chunk]
            for s in range(chunk):
                keep = rowid >= s
                e = jnp.where(keep, jnp.exp(jnp.where(keep, b - b[s:s + 1, :], 0.0)), 0.0)
                col = jnp.sum(q * e * k[s:s + 1, :], axis=-1, keepdims=True)
                o = o + col * v[s:s + 1, :]
            b_last = b[chunk - 1:chunk, :]
            kd = _pad_rows((k * jnp.exp(b_last - b)).astype(BF16), MM_ROWS)
            vd = _pad_rows(v.astype(BF16), MM_ROWS)
            st_new = st * jnp.exp(b_last) + _dot_tn(vd, kd)
            if carry:
                st_ref[h] = st_new
            else:
                sout_ref[c, h] = st_new.T
            on = o * lax.rsqrt(jnp.mean(o * o, axis=-1, keepdims=True) + RMS_EPS) * ng
            o_ref[rs, cs_] = on * _sigmoid(hg)
        return 0

    lax.fori_loop(0, nchunk, do_chunk, 0)

    if carry:
        @pl.when(pl.program_id(1) == pl.num_programs(1) - 1)
        def _():
            for h in range(HG_HEADS):
                sout_ref[0, h] = st_ref[h].T


def _hgrn(h4, s0, lb_logits, norm_g, layer, nb, seq):
    n = h4.shape[0]
    chunk = math.gcd(seq, HG_CHUNK)
    carry = seq > chunk
    if carry:
        tl = _tile(seq, 256)
        grid = (nb, seq // tl)
        row_map = lambda b, l: (b * (seq // tl) + l, 0)
        st_map = lambda b, l: (b, 0, 0, 0)
        st_blk = (1, HG_HEADS, HG_DK, HG_DK)
        sem = ("parallel", "arbitrary")
    else:
        bt = _tile(nb, 8)
        tl = bt * seq
        grid = (nb // bt, 1)
        row_map = lambda b, l: (b, 0)
        st_map = lambda b, l: (b, 0, 0, 0)
        st_blk = (bt, HG_HEADS, HG_DK, HG_DK)
        sem = ("parallel", "arbitrary")
    kern = functools.partial(_hgrn_kernel, layer=layer, chunk=chunk, carry=carry)
    return pl.pallas_call(
        kern,
        out_shape=(jax.ShapeDtypeStruct((n, HG_WIDTH), F32),
                   jax.ShapeDtypeStruct((nb, HG_HEADS, HG_DK, HG_DK), F32)),
        grid=grid,
        in_specs=[pl.BlockSpec((tl, 4 * HG_WIDTH), row_map),
                  pl.BlockSpec(st_blk, st_map),
                  pl.BlockSpec((N_EVEN, HG_WIDTH), lambda b, l: (0, 0)),
                  pl.BlockSpec((1, HG_DK), lambda b, l: (0, 0))],
        out_specs=(pl.BlockSpec((tl, HG_WIDTH), row_map), pl.BlockSpec(st_blk, st_map)),
        scratch_shapes=[pltpu.VMEM((HG_HEADS, HG_DK, HG_DK), F32),
                        pltpu.VMEM((tl, HG_WIDTH), F32), pltpu.VMEM((tl, HG_WIDTH), F32),
                        pltpu.VMEM((tl, HG_WIDTH), F32)],
        compiler_params=_params(*sem), name="hgrn2")(h4, s0, lb_logits, norm_g)


def _lambda(lamp_ref):
    lam_init = lamp_ref[4:5, 0:1]
    l1 = jnp.sum(lamp_ref[0:1, :] * lamp_ref[1:2, :], axis=-1, keepdims=True)
    l2 = jnp.sum(lamp_ref[2:3, :] * lamp_ref[3:4, :], axis=-1, keepdims=True)
    return jnp.exp(l1) - jnp.exp(l2) + lam_init, lam_init


def _diff_finish(o0, o1, lam, lam_init, ng):
    o = o0 - lam * o1
    return o * lax.rsqrt(jnp.mean(o * o, axis=-1, keepdims=True) + RMS_EPS) * ng * (1.0 - lam_init)


def _attn_prompt_kernel(q_ref, k_ref, v_ref, lamp_ref, ng_ref, o_ref, m_s, l_s, acc_s):
    tq = q_ref.shape[0]
    i = pl.program_id(2)
    lane = lax.broadcasted_iota(jnp.int32, (tq, DA_DV), 1)
    qs = q_ref[...] * (DA_DH ** -0.5)
    q2 = jnp.concatenate([jnp.where(lane < DA_DH, qs, 0.0), jnp.where(lane >= DA_DH, qs, 0.0)],
                         axis=0).astype(BF16)
    m_s[...] = jnp.full(m_s.shape, NEG_BIG, F32)
    l_s[...] = jnp.zeros(l_s.shape, F32)
    acc_s[...] = jnp.zeros(acc_s.shape, F32)
    row = lax.broadcasted_iota(jnp.int32, (2 * tq, tq), 0) % tq
    col = lax.broadcasted_iota(jnp.int32, (2 * tq, tq), 1)

    def step(j, masked):
        r0 = pl.multiple_of(j * tq, tq)
        kb = k_ref[pl.ds(r0, tq), :].astype(BF16)
        vb = v_ref[pl.ds(r0, tq), :].astype(BF16)
        s = _dot_nt(q2, kb)
        if masked:
            s = jnp.where(col <= row, s, NEG_BIG)
        m_old = m_s[...]
        m_new = jnp.maximum(m_old, jnp.max(s, axis=-1, keepdims=True))
        alpha = jnp.exp(m_old - m_new)
        p = jnp.exp(s - m_new)
        l_s[...] = alpha * l_s[...] + jnp.sum(p, axis=-1, keepdims=True)
        m_s[...] = m_new
        acc_s[...] = alpha * acc_s[...] + _dot(p.astype(BF16), vb)

    def body(j, c):
        step(j, False)
        return c

    lax.fori_loop(0, i, body, 0)
    step(i, True)
    lam, lam_init = _lambda(lamp_ref)
    o0 = acc_s[0:tq, :] / l_s[0:tq, :]
    o1 = acc_s[tq:2 * tq, :] / l_s[tq:2 * tq, :]
    o_ref[...] = _diff_finish(o0, o1, lam, lam_init, ng_ref[...])


def _attn_prompt(q, k, v, lamp, norm_g, nb, seq):
    n = q.shape[0]
    tq = _tile(seq, 256)
    nq = seq // tq
    return pl.pallas_call(
        _attn_prompt_kernel, out_shape=jax.ShapeDtypeStruct((n, DA_WIDTH), F32),
        grid=(nb, DA_HEADS, nq),
        in_specs=[pl.BlockSpec((tq, DA_DV), lambda b, h, i: (b * nq + i, h)),
                  pl.BlockSpec((seq, DA_DV), lambda b, h, i: (b, h)),
                  pl.BlockSpec((seq, DA_DV), lambda b, h, i: (b, h)),
                  pl.BlockSpec((8, LANES), lambda b, h, i: (0, 0)),
                  pl.BlockSpec((1, DA_DV), lambda b, h, i: (0, 0))],
        out_specs=pl.BlockSpec((tq, DA_DV), lambda b, h, i: (b * nq + i, h)),
        scratch_shapes=[pltpu.VMEM((2 * tq, 1), F32), pltpu.VMEM((2 * tq, 1), F32),
                        pltpu.VMEM((2 * tq, DA_DV), F32)],
        compiler_params=_params("parallel", "parallel", "arbitrary"),
        name="attn_prompt")(q, k, v, lamp, norm_g)


def _attn_sample_kernel(pt_ref, q_ref, kp_ref, vp_ref, kn_ref, vn_ref, lamp_ref, ng_ref, o_ref,
                        qbd_s, m_s, l_s, acc_s):
    t = q_ref.shape[0]
    nrow = 2 * DA_HEADS * t
    p_idx = pl.program_id(1)

    @pl.when(p_idx == 0)
    def _():
        qs = q_ref[...] * (DA_DH ** -0.5)
        qt = jnp.concatenate([qs] * (2 * DA_HEADS), axis=0)
        r = lax.broadcasted_iota(jnp.int32, (nrow, DA_QK), 0) // t
        c = lax.broadcasted_iota(jnp.int32, (nrow, DA_QK), 1) // DA_DH
        qbd_s[...] = jnp.where(r == c, qt, 0.0).astype(BF16)
        m_s[...] = jnp.full(m_s.shape, NEG_BIG, F32)
        l_s[...] = jnp.zeros(l_s.shape, F32)
        acc_s[...] = jnp.zeros(acc_s.shape, F32)

    def update(s, v_of_head, mask):
        if mask is not None:
            s = jnp.where(mask, s, NEG_BIG)
        m_old = m_s[...]
        m_new = jnp.maximum(m_old, jnp.max(s, axis=-1, keepdims=True))
        alpha = jnp.exp(m_old - m_new)
        p = jnp.exp(s - m_new)
        l_s[...] = alpha * l_s[...] + jnp.sum(p, axis=-1, keepdims=True)
        m_s[...] = m_new
        pb = p.astype(BF16)
        pv = [_dot(pb[2 * t * h:2 * t * (h + 1), :], v_of_head(h)) for h in range(DA_HEADS)]
        acc_s[...] = alpha * acc_s[...] + jnp.concatenate(pv, axis=0)

    update(_dot(qbd_s[...], kp_ref[...].astype(BF16)),
           lambda h: vp_ref[:, h, :].astype(BF16), None)

    @pl.when(p_idx == pl.num_programs(1) - 1)
    def _():
        kb = _pad_rows(kn_ref[...], PAGE_SIZE).astype(BF16)
        vb = _pad_rows(vn_ref[...], PAGE_SIZE).astype(BF16)
        r = lax.broadcasted_iota(jnp.int32, (nrow, PAGE_SIZE), 0) % t
        c = lax.broadcasted_iota(jnp.int32, (nrow, PAGE_SIZE), 1)
        update(_dot_nt(qbd_s[...], kb), lambda h: vb[:, h * DA_DV:(h + 1) * DA_DV], c <= r)
        lam, lam_init = _lambda(lamp_ref)
        ng = ng_ref[...]
        for h in range(DA_HEADS):
            r0 = 2 * h * t
            o0 = acc_s[r0:r0 + t, :] / l_s[r0:r0 + t, :]
            o1 = acc_s[r0 + t:r0 + 2 * t, :] / l_s[r0 + t:r0 + 2 * t, :]
            o_ref[:, h * DA_DV:(h + 1) * DA_DV] = _diff_finish(o0, o1, lam, lam_init, ng)


def _attn_sample(q, k_new, v_new, cache_kt, cache_v, page_table, lamp, norm_g, layer, nb, seq):
    n = q.shape[0]
    n_pages = page_table.shape[1]
    nrow = 2 * DA_HEADS * seq
    row_map = lambda b, p, pt: (b, 0)
    grid_spec = pltpu.PrefetchScalarGridSpec(
        num_scalar_prefetch=1, grid=(nb, n_pages),
        in_specs=[pl.BlockSpec((seq, DA_QK), row_map),
                  pl.BlockSpec((None, None, DA_QK, PAGE_SIZE),
                               lambda b, p, pt: (layer, pt[b * n_pages + p], 0, 0)),
                  pl.BlockSpec((None, None, PAGE_SIZE, DA_HEADS, DA_DV),
                               lambda b, p, pt: (layer, pt[b * n_pages + p], 0, 0, 0)),
                  pl.BlockSpec((seq, DA_QK), row_map),
                  pl.BlockSpec((seq, DA_WIDTH), row_map),
                  pl.BlockSpec((8, LANES), lambda b, p, pt: (0, 0)),
                  pl.BlockSpec((1, DA_DV), lambda b, p, pt: (0, 0))],
        out_specs=pl.BlockSpec((seq, DA_WIDTH), row_map),
        scratch_shapes=[pltpu.VMEM((nrow, DA_QK), BF16), pltpu.VMEM((nrow, 1), F32),
                        pltpu.VMEM((nrow, 1), F32), pltpu.VMEM((nrow, DA_DV), F32)])
    return pl.pallas_call(
        _attn_sample_kernel, out_shape=jax.ShapeDtypeStruct((n, DA_WIDTH), F32),
        grid_spec=grid_spec, compiler_params=_params("parallel", "arbitrary"),
        name="attn_sample")(page_table.reshape(-1), q, cache_kt, cache_v, k_new, v_new, lamp, norm_g)


def _even_out_kernel(oh_ref, od_ref, x_ref, w_ref, g_ref, b_ref, o_ref):
    h = (_dot(oh_ref[...].astype(BF16), w_ref[0:HG_WIDTH, :])
         + _dot(od_ref[...].astype(BF16), w_ref[HG_WIDTH:D_MODEL, :]))
    o_ref[...] = _layer_norm(ALPHA * x_ref[...] + h, g_ref[...], b_ref[...])


def _even_out(oh, od, x, w_out_bf, g, b):
    n = x.shape[0]
    tm = _tile(n, 512)
    vec = pl.BlockSpec((1, D_MODEL), lambda i: (0, 0))
    return pl.pallas_call(
        _even_out_kernel, out_shape=jax.ShapeDtypeStruct((n, D_MODEL), F32), grid=(n // tm,),
        in_specs=[_row_spec(tm, HG_WIDTH), _row_spec(tm, DA_WIDTH), _row_spec(tm, D_MODEL),
                  _const_spec((D_MODEL, D_MODEL)), vec, vec],
        out_specs=_row_spec(tm, D_MODEL),
        compiler_params=_params("parallel"), name="even_out")(oh, od, x, w_out_bf, g, b)


FF_CHUNK = 1024


def _mlp_kernel(x_ref, p_ref, up_ref, down_ref, g_ref, b_ref, gate_ref, proj_ref, o_ref):
    x = x_ref[...]
    xb = x.astype(BF16)
    acc = jnp.zeros(x.shape, F32)
    for c in range(D_FF // FF_CHUNK):
        cs = slice(c * FF_CHUNK, (c + 1) * FF_CHUNK)
        hid = jnp.maximum(_dot(xb, up_ref[:, cs]), 0.0)
        acc = acc + _dot((hid * hid).astype(BF16), down_ref[cs, :])
    x2 = _layer_norm(ALPHA * x + acc, g_ref[...], b_ref[...])
    gate = _sigmoid(_dot(x2.astype(BF16), gate_ref[...]))
    o_ref[...] = x2 + gate * _dot(p_ref[...].astype(BF16), proj_ref[...])


def _mlp(x, p, up_bf, down_bf, g, b, gate_bf, proj_bf):
    n = x.shape[0]
    tm = _tile(n, 512)
    vec = pl.BlockSpec((1, D_MODEL), lambda i: (0, 0))
    return pl.pallas_call(
        _mlp_kernel, out_shape=jax.ShapeDtypeStruct((n, D_MODEL), F32), grid=(n // tm,),
        in_specs=[_row_spec(tm, D_MODEL), _row_spec(tm, PLE_DIM),
                  _const_spec((D_MODEL, D_FF)), _const_spec((D_FF, D_MODEL)), vec, vec,
                  _const_spec((D_MODEL, D_MODEL)), _const_spec((PLE_DIM, D_MODEL))],
        out_specs=_row_spec(tm, D_MODEL),
        compiler_params=_params("parallel"), name="mlp")(x, p, up_bf, down_bf, g, b, gate_bf, proj_bf)


def _rwkv_pre_kernel(*refs, has_vfirst):
    if has_vfirst:
        (x_ref, xp_ref, vf_ref, mix_ref, wr_ref, wk_ref, wv_ref, w0_ref, w1_ref, w2_ref,
         a0_ref, a1_ref, a2_ref, v0_ref, v1_ref, v2_ref, g1_ref, g2_ref, kk_ref, ka_ref,
         r_o, k_o, v_o, g_o, na_o, wr_o, dec_o, b_o, br_o, kr_o) = refs
    else:
        (x_ref, xp_ref, mix_ref, wr_ref, wk_ref, wv_ref, w0_ref, w1_ref, w2_ref,
         a0_ref, a1_ref, a2_ref, g1_ref, g2_ref, kk_ref, ka_ref,
         r_o, k_o, v_o, g_o, na_o, wr_o, dec_o, b_o, br_o, kr_o) = refs
    x = x_ref[...]
    xx = xp_ref[...] - x
    xr, xw, xk, xv, xa, xg = ((x + xx * mix_ref[m:m + 1, :]).astype(BF16) for m in range(6))
    r = _dot(xr, wr_ref[...])
    k = _dot(xk, wk_ref[...])
    v = _dot(xv, wv_ref[...])
    lw = w0_ref[...] + _dot(jnp.tanh(_dot(xw, w1_ref[...])).astype(BF16), w2_ref[...])
    w = -(jnp.maximum(-lw, 0.0) + jnp.log(1.0 + jnp.exp(-jnp.abs(lw)))) - 0.5
    if has_vfirst:
        mixv = _sigmoid(v0_ref[...] + _dot(_dot(xv, v1_ref[...]).astype(BF16), v2_ref[...]))
        v = v + (vf_ref[...] - v) * mixv
    a = _sigmoid(a0_ref[...] + _dot(_dot(xa, a1_ref[...]).astype(BF16), a2_ref[...]))
    g = _dot(_sigmoid(_dot(xg, g1_ref[...])).astype(BF16), g2_ref[...])
    ones = _head_ones(RW_HEAD)
    kk = k * kk_ref[...]
    kk = kk / jnp.maximum(jnp.sqrt(_segsum(kk * kk, ones)), 1e-12)
    k = k * (1.0 + (a - 1.0) * ka_ref[...])
    dec = jnp.exp(-jnp.exp(w))
    b = kk * a
    r_o[...] = r
    k_o[...] = k
    v_o[...] = v
    g_o[...] = g
    na_o[...] = -kk
    wr_o[...] = dec * r
    dec_o[...] = dec
    b_o[...] = b
    br_o[...] = _segsum(b * r, ones)
    kr_o[...] = _segsum(k * r, ones)


def _rwkv_pre(x, x_prev, v_first, wts):
    n = x.shape[0]
    tm = _tile(n, 256)
    has_vfirst = v_first is not None
    vec = pl.BlockSpec((1, D_MODEL), lambda i: (0, 0))
    acts = [x, x_prev] + ([v_first] if has_vfirst else [])
    names = ['mix', 'w_r', 'w_k', 'w_v', 'w0', 'w1', 'w2', 'a0', 'a1', 'a2']
    if has_vfirst:
        names += ['v0', 'v1', 'v2']
    names += ['g1', 'g2', 'k_k', 'k_a']
    ws = [wts[nm] for nm in names]
    w_specs = [vec if w.shape == (1, D_MODEL) else _const_spec(w.shape) for w in ws]
    out = jax.ShapeDtypeStruct((n, D_MODEL), F32)
    return pl.pallas_call(
        functools.partial(_rwkv_pre_kernel, has_vfirst=has_vfirst),
        out_shape=(out,) * 10, grid=(n // tm,),
        in_specs=[_row_spec(tm, D_MODEL)] * len(acts) + w_specs,
        out_specs=(_row_spec(tm, D_MODEL),) * 10,
        compiler_params=_params("parallel"), name="rwkv_pre")(*acts, *ws)


RW_PAIRS = 4


def _rwkv_scan_kernel(na_ref, wr_ref, dec_ref, b_ref, k_ref, v_ref, br_ref, kr_ref, s0_ref,
                      y_ref, sout_ref, s_s, *, carry, seq):
    tl = na_ref.shape[0]
    nseq = tl // seq if not carry else 1
    n = RW_HEAD
    lane = lax.broadcasted_iota(jnp.int32, (n, LANES), 1)
    sub = lax.broadcasted_iota(jnp.int32, (n, LANES), 0)
    m0 = lane < n
    diag = sub == (lane % n)
    d0 = diag & m0
    d1 = diag & jnp.logical_not(m0)

    def load_state(bi):
        for p in range(RW_PAIRS):
            s_s[p] = jnp.concatenate([s0_ref[bi, 2 * p], s0_ref[bi, 2 * p + 1]], axis=1)

    def store_state(bi):
        for p in range(RW_PAIRS):
            s = s_s[p]
            sout_ref[bi, 2 * p] = s[:, 0:n]
            sout_ref[bi, 2 * p + 1] = s[:, n:2 * n]

    def halves(x):
        s_lo = jnp.sum(jnp.where(m0, x, 0.0), axis=-1, keepdims=True)
        s_hi = jnp.sum(jnp.where(m0, 0.0, x), axis=-1, keepdims=True)
        return jnp.where(m0, s_lo, s_hi)

    def step(t8, _):
        ts = pl.ds(pl.multiple_of(t8 * SUBLANES, SUBLANES), SUBLANES)
        for p in range(RW_PAIRS):
            cs = slice(p * LANES, (p + 1) * LANES)
            a8, wr8, dec8, b8 = na_ref[ts, cs], wr_ref[ts, cs], dec_ref[ts, cs], b_ref[ts, cs]
            k8, v8, br8, kr8 = k_ref[ts, cs], v_ref[ts, cs], br_ref[ts, cs], kr_ref[ts, cs]
            s = s_s[p]
            ys = []
            for i in range(SUBLANES):
                row = lambda x8: x8[i:i + 1, :]
                v = row(v8)
                sa = halves(s * row(a8))
                qs = halves(s * row(wr8))
                vlo = jnp.sum(jnp.where(d0, v, 0.0), axis=-1, keepdims=True)
                vhi = jnp.sum(jnp.where(d1, v, 0.0), axis=-1, keepdims=True)
                vc = jnp.where(m0, vlo, vhi)
                s = s * row(dec8) + sa * row(b8) + vc * row(k8)
                ycol = qs + sa * row(br8)
                ys.append(jnp.sum(jnp.where(diag, ycol, 0.0), axis=0, keepdims=True)
                          + v * row(kr8))
            s_s[p] = s
            y_ref[ts, cs] = jnp.concatenate(ys, axis=0)
        return 0

    if carry:
        @pl.when(pl.program_id(2) == 0)
        def _():
            load_state(0)
        lax.fori_loop(0, tl // SUBLANES, step, 0)

        @pl.when(pl.program_id(2) == pl.num_programs(2) - 1)
        def _():
            store_state(0)
    else:
        g8 = seq // SUBLANES

        def one_seq(bi, _):
            load_state(bi)
            lax.fori_loop(bi * g8, (bi + 1) * g8, step, 0)
            store_state(bi)
            return 0
        lax.fori_loop(0, nseq, one_seq, 0)


def _rwkv_scan(na, wr, dec, b, k, v, br, kr, s0, nb, seq):
    n = na.shape[0]
    pw = RW_PAIRS * LANES
    ng = D_MODEL // pw
    hp = 2 * RW_PAIRS
    carry = seq > 64
    if carry:
        tl = _tile(seq, 256)
        nl = seq // tl
        grid = (nb, ng, nl)
        row_map = lambda bi, g, l: (bi * nl + l, g)
        st_blk = (1, hp, RW_HEAD, RW_HEAD)
    else:
        bt = _tile(nb, 8)
        tl = bt * seq
        grid = (nb // bt, ng, 1)
        row_map = lambda bi, g, l: (bi, g)
        st_blk = (bt, hp, RW_HEAD, RW_HEAD)
    st_map = lambda bi, g, l: (bi, g, 0, 0)
    act = pl.BlockSpec((tl, pw), row_map)
    return pl.pallas_call(
        functools.partial(_rwkv_scan_kernel, carry=carry, seq=seq),
        out_shape=(jax.ShapeDtypeStruct((n, D_MODEL), F32),
                   jax.ShapeDtypeStruct((nb, RW_HEADS, RW_HEAD, RW_HEAD), F32)),
        grid=grid,
        in_specs=[act] * 8 + [pl.BlockSpec(st_blk, st_map)],
        out_specs=(act, pl.BlockSpec(st_blk, st_map)),
        scratch_shapes=[pltpu.VMEM((RW_PAIRS, RW_HEAD, LANES), F32)],
        compiler_params=_params("parallel", "parallel", "arbitrary"),
        name="rwkv_scan")(na, wr, dec, b, k, v, br, kr, s0)


def _rwkv_post_kernel(y_ref, r_ref, k_ref, v_ref, g_ref, x_ref, lg_ref, lb_ref, rk_ref, wo_ref,
                      n1g_ref, n1b_ref, o_ref):
    ones = _head_ones(RW_HEAD)
    y = y_ref[...]
    inv = 1.0 / RW_HEAD
    mu = _segsum(y, ones) * inv
    yc = y - mu
    var = _segsum(yc * yc, ones) * inv
    yn = yc * lax.rsqrt(var + GN_EPS) * lg_ref[...] + lb_ref[...]
    yn = yn + _segsum(r_ref[...] * k_ref[...] * rk_ref[...], ones) * v_ref[...]
    h = _dot((yn * g_ref[...]).astype(BF16), wo_ref[...])
    o_ref[...] = _layer_norm(ALPHA * x_ref[...] + h, n1g_ref[...], n1b_ref[...])


def _rwkv_post(y, r, k, v, g, x, lnx_g, lnx_b, r_k, wo_bf, n1g, n1b):
    n = x.shape[0]
    tm = _tile(n, 512)
    vec = pl.BlockSpec((1, D_MODEL), lambda i: (0, 0))
    return pl.pallas_call(
        _rwkv_post_kernel, out_shape=jax.ShapeDtypeStruct((n, D_MODEL), F32), grid=(n // tm,),
        in_specs=[_row_spec(tm, D_MODEL)] * 6 + [vec, vec, vec, _const_spec((D_MODEL, D_MODEL)),
                                                  vec, vec],
        out_specs=_row_spec(tm, D_MODEL),
        compiler_params=_params("parallel"), name="rwkv_post")(
            y, r, k, v, g, x, lnx_g, lnx_b, r_k, wo_bf, n1g, n1b)


def _trunk(x3, p4, hg_s0, wkv_s0, shift0, cache_k, cache_v, page_table, W):
    nb, seq, _ = x3.shape
    n = nb * seq
    x = x3.reshape(n, D_MODEL)
    new_k, new_v, new_hg, new_wkv, new_shift = [], [], [], [], []
    v_first = None
    for i in range(DEPTH):
        j = i // 2
        if i % 2 == 0:
            h4, q, k, v = _even_in(x, W['w_in_even'][j])
            oh, s_hg = _hgrn(h4, hg_s0[j], W['hg_lb_logits'], W['hg_norm_g'][j], j, nb, seq)
            if page_table is None:
                od = _attn_prompt(q, k, v, W['lamp'][j], W['da_norm_g'][j], nb, seq)
            else:
                od = _attn_sample(q, k, v, cache_k, cache_v, page_table, W['lamp'][j],
                                  W['da_norm_g'][j], j, nb, seq)
            x = _even_out(oh, od, x, W['w_out_even'][j], W['ln1_g'][i], W['ln1_b'][i])
            new_k.append(k.reshape(nb, seq, DA_HEADS, 2, DA_DH))
            new_v.append(v.reshape(nb, seq, DA_HEADS, DA_DV))
            new_hg.append(s_hg)
        else:
            xs = x.reshape(nb, seq, D_MODEL)
            x_prev = jnp.concatenate([shift0[j][:, None, :], xs[:, :-1]], axis=1).reshape(n, D_MODEL)
            r, k, v, g, na, wr, dec, b, br, kr = _rwkv_pre(x, x_prev, v_first, W['rw'][j])
            if j == 0:
                v_first = v
            y, s_wkv = _rwkv_scan(na, wr, dec, b, k, v, br, kr, wkv_s0[j], nb, seq)
            new_shift.append(xs[:, -1])
            new_wkv.append(s_wkv)
            x = _rwkv_post(y, r, k, v, g, x, W['rw_lnx_g'][j], W['rw_lnx_b'][j], W['rw_r_k'][j],
                           W['rw_w_o'][j], W['ln1_g'][i], W['ln1_b'][i])
        x = _mlp(x, p4[i].reshape(n, PLE_DIM), W['mlp_up'][i], W['mlp_down'][i],
                 W['ln2_g'][i], W['ln2_b'][i], W['ple_gate'][i], W['ple_proj'][i])
    return (x.reshape(nb, seq, D_MODEL), jnp.stack(new_k), jnp.stack(new_v), jnp.stack(new_hg),
            jnp.stack(new_wkv), jnp.stack(new_shift))


def kernel(x_prompt, x_sample, cache_k, cache_v, state_hgrn, state_wkv, state_shift, page_table,
           p_prompt, p_sample, w_in_even, w_out_even, hg_lb_logits, hg_norm_g,
           da_lam_q1, da_lam_k1, da_lam_q2, da_lam_k2, da_norm_g,
           rw_mix, rw_w_r, rw_w_k, rw_w_v, rw_w_o, rw_w0, rw_w1, rw_w2,
           rw_a0, rw_a1, rw_a2, rw_v0, rw_v1, rw_v2, rw_g1, rw_g2,
           rw_k_k, rw_k_a, rw_r_k, rw_lnx_g, rw_lnx_b,
           ln1_g, ln1_b, ln2_g, ln2_b, mlp_up, mlp_down, ple_proj, ple_gate):
    bf = lambda w: w.astype(BF16)
    vec = lambda w: w.reshape(1, -1)

    def lam_rows(j):
        pad = lambda a: jnp.pad(a.astype(F32), (0, LANES - a.shape[0]))
        lam_init = 0.8 - 0.6 * math.exp(-0.3 * (2 * j))
        rows = [pad(da_lam_q1[j]), pad(da_lam_k1[j]), pad(da_lam_q2[j]), pad(da_lam_k2[j]),
                jnp.full((LANES,), lam_init, F32)]
        return jnp.stack(rows + [jnp.zeros((LANES,), F32)] * 3)

    rw = []
    for j in range(N_ODD):
        d = dict(mix=rw_mix[j], w_r=bf(rw_w_r[j]), w_k=bf(rw_w_k[j]), w_v=bf(rw_w_v[j]),
                 w0=vec(rw_w0[j]), w1=bf(rw_w1[j]), w2=bf(rw_w2[j]),
                 a0=vec(rw_a0[j]), a1=bf(rw_a1[j]), a2=bf(rw_a2[j]),
                 g1=bf(rw_g1[j]), g2=bf(rw_g2[j]), k_k=vec(rw_k_k[j]), k_a=vec(rw_k_a[j]))
        if j > 0:
            d.update(v0=vec(rw_v0[j - 1]), v1=bf(rw_v1[j - 1]), v2=bf(rw_v2[j - 1]))
        rw.append(d)
    W = dict(
        w_in_even=[bf(w_in_even[j]) for j in range(N_EVEN)],
        w_out_even=[bf(w_out_even[j]) for j in range(N_EVEN)],
        hg_lb_logits=hg_lb_logits,
        hg_norm_g=[vec(hg_norm_g[j]) for j in range(N_EVEN)],
        da_norm_g=[vec(da_norm_g[j]) for j in range(N_EVEN)],
        lamp=[lam_rows(j) for j in range(N_EVEN)],
        rw=rw,
        rw_lnx_g=[vec(rw_lnx_g[j]) for j in range(N_ODD)],
        rw_lnx_b=[vec(rw_lnx_b[j]) for j in range(N_ODD)],
        rw_r_k=[vec(rw_r_k[j]) for j in range(N_ODD)],
        rw_w_o=[bf(rw_w_o[j]) for j in range(N_ODD)],
        ln1_g=[vec(ln1_g[i]) for i in range(DEPTH)], ln1_b=[vec(ln1_b[i]) for i in range(DEPTH)],
        ln2_g=[vec(ln2_g[i]) for i in range(DEPTH)], ln2_b=[vec(ln2_b[i]) for i in range(DEPTH)],
        mlp_up=[bf(mlp_up[i]) for i in range(DEPTH)], mlp_down=[bf(mlp_down[i]) for i in range(DEPTH)],
        ple_gate=[bf(ple_gate[i]) for i in range(DEPTH)], ple_proj=[bf(ple_proj[i]) for i in range(DEPTH)],
    )
    nbp = x_prompt.shape[0]
    dt = x_prompt.dtype
    hg0 = jnp.zeros((N_EVEN, nbp, HG_HEADS, HG_DK, HG_DK), dt)
    wkv0 = jnp.zeros((N_ODD, nbp, RW_HEADS, RW_HEAD, RW_HEAD), dt)
    sh0 = jnp.zeros((N_ODD, nbp, D_MODEL), dt)
    y_p, k_p, v_p, hg_p, wkv_p, sh_p = _trunk(x_prompt, p_prompt, hg0, wkv0, sh0, None, None, None, W)
    ckt = jnp.transpose(cache_k, (0, 1, 3, 4, 5, 2)).reshape(
        cache_k.shape[0], cache_k.shape[1], DA_QK, PAGE_SIZE)
    y_s, k_s, v_s, hg_s, wkv_s, sh_s = _trunk(x_sample, p_sample, state_hgrn, state_wkv, state_shift,
                                              ckt, cache_v, page_table, W)
    return (y_p, y_s, k_p, v_p, k_s, v_s, hg_p, hg_s, wkv_p, wkv_s, sh_p, sh_s)
```

```python
import functools
import math

import jax
import jax.numpy as jnp
from jax import lax
from jax.experimental import pallas as pl
from jax.experimental.pallas import tpu as pltpu

F32 = jnp.float32
BF16 = jnp.bfloat16

D_MODEL = 1024
DEPTH = 4
PAGE_SIZE = 128
N_EVEN = (DEPTH + 1) // 2
N_ODD = DEPTH // 2
PLE_DIM = 256
D_FF = 4 * D_MODEL
HG_WIDTH = D_MODEL // 2
HG_HEADS = 4
HG_DK = HG_WIDTH // HG_HEADS
DA_WIDTH = D_MODEL - HG_WIDTH
DA_HEADS = 4
DA_DV = DA_WIDTH // DA_HEADS
DA_DH = DA_DV // 2
DA_QK = DA_HEADS * 2 * DA_DH
IN_COLS = 4 * HG_WIDTH + 2 * DA_QK + DA_WIDTH
RW_HEAD = 64
RW_HEADS = D_MODEL // RW_HEAD
ALPHA = (2.0 * DEPTH) ** 0.25
LN_EPS = 1e-5
RMS_EPS = 1e-6
GN_EPS = 64e-5
F_MIN = 1e-30
NEG_BIG = -1e30

LANES = 128
SUBLANES = 8
VMEM_LIMIT = 52 * 1024 * 1024
HG_CHUNK = 16
MM_ROWS = 16


def _params(*sem):
    return pltpu.CompilerParams(dimension_semantics=sem, vmem_limit_bytes=VMEM_LIMIT)


def _const_spec(shape):
    nd = len(shape)
    return pl.BlockSpec(shape, lambda *_: (0,) * nd, pipeline_mode=pl.Buffered(1))


def _row_spec(rows, cols):
    return pl.BlockSpec((rows, cols), lambda i: (i, 0))


def _tile(n, pref):
    t = min(n, pref)
    while n % t:
        t //= 2
    return t


def _dot(a, b):
    return jnp.dot(a, b, preferred_element_type=F32)


def _dot_nt(a, b):
    return lax.dot_general(a, b, (((1,), (1,)), ((), ())), preferred_element_type=F32)


def _dot_tn(a, b):
    return lax.dot_general(a, b, (((0,), (0,)), ((), ())), preferred_element_type=F32)


def _split3(x):
    h = x.astype(BF16)
    r = x - h.astype(F32)
    m = r.astype(BF16)
    l = (r - m.astype(F32)).astype(BF16)
    return h, m, l


def _dot_x3(x, rhs_bf16):
    h, m, l = _split3(x)
    return _dot(h, rhs_bf16) + _dot(m, rhs_bf16) + _dot(l, rhs_bf16)


def _dot_x3_lhs(lhs_bf16, x):
    h, m, l = _split3(x)
    return _dot(lhs_bf16, h) + _dot(lhs_bf16, m) + _dot(lhs_bf16, l)


def _layer_norm(z, g, b):
    mu = jnp.mean(z, axis=-1, keepdims=True)
    zc = z - mu
    var = jnp.mean(zc * zc, axis=-1, keepdims=True)
    return zc * lax.rsqrt(var + LN_EPS) * g + b


def _sigmoid(x):
    return 1.0 / (1.0 + jnp.exp(-x))


def _head_ones(width):
    r = lax.broadcasted_iota(jnp.int32, (LANES, LANES), 0) // width
    c = lax.broadcasted_iota(jnp.int32, (LANES, LANES), 1) // width
    return jnp.where(r == c, 1.0, 0.0).astype(BF16)


def _segsum(x, ones):
    cols = x.shape[1] // LANES
    parts = [_dot_x3(x[:, c * LANES:(c + 1) * LANES], ones) for c in range(cols)]
    return parts[0] if cols == 1 else jnp.concatenate(parts, axis=1)


def _even_in_kernel(x_ref, w_ref, h4_ref, q_ref, k_ref, v_ref):
    xb = x_ref[...].astype(BF16)
    cw = HG_WIDTH
    for c in range(4):
        h4_ref[:, c * cw:(c + 1) * cw] = _dot(xb, w_ref[:, c * cw:(c + 1) * cw])
    q0 = 4 * HG_WIDTH
    q_ref[...] = _dot(xb, w_ref[:, q0:q0 + DA_QK])
    k_ref[...] = _dot(xb, w_ref[:, q0 + DA_QK:q0 + 2 * DA_QK])
    v_ref[...] = _dot(xb, w_ref[:, q0 + 2 * DA_QK:q0 + 2 * DA_QK + DA_WIDTH])


def _even_in(x, w_in_bf):
    n = x.shape[0]
    tm = _tile(n, 512)
    outs = (jax.ShapeDtypeStruct((n, 4 * HG_WIDTH), F32), jax.ShapeDtypeStruct((n, DA_QK), F32),
            jax.ShapeDtypeStruct((n, DA_QK), F32), jax.ShapeDtypeStruct((n, DA_WIDTH), F32))
    return pl.pallas_call(
        _even_in_kernel, out_shape=outs, grid=(n // tm,),
        in_specs=[_row_spec(tm, D_MODEL), _const_spec((D_MODEL, IN_COLS))],
        out_specs=(_row_spec(tm, 4 * HG_WIDTH), _row_spec(tm, DA_QK), _row_spec(tm, DA_QK),
                   _row_spec(tm, DA_WIDTH)),
        compiler_params=_params("parallel"), name="even_in")(x, w_in_bf)


def _pad_rows(x, rows):
    if x.shape[0] >= rows:
        return x
    return jnp.concatenate([x, jnp.zeros((rows - x.shape[0], x.shape[1]), x.dtype)], axis=0)


def _hgrn_kernel(h4_ref, s0_ref, lbl_ref, ng_ref, o_ref, sout_ref,
                 st_ref, q_s, k_s, b_s, *, layer, chunk, carry):
    tl = h4_ref.shape[0]
    nchunk = tl // chunk
    w = HG_WIDTH

    rows = [lbl_ref[i:i + 1, :] for i in range(N_EVEN)]
    mx = functools.reduce(jnp.maximum, rows)
    es = [jnp.exp(r - mx) for r in rows]
    den = functools.reduce(lambda a, b: a + b, es)
    sm = [e / den for e in es]
    cs = functools.reduce(lambda a, b: a + b, sm[:layer + 1])
    lb = cs - sm[0]

    hq = h4_ref[:, 0:w]
    hf = h4_ref[:, w:2 * w]
    q_s[...] = hq * _sigmoid(hq)
    f = lb + (1.0 - lb) * _sigmoid(hf)
    g = jnp.log(jnp.maximum(f, F_MIN))
    k_s[...] = 1.0 - f
    ri = lax.broadcasted_iota(jnp.int32, (tl, tl), 0)
    ci = lax.broadcasted_iota(jnp.int32, (tl, tl), 1)
    tril = jnp.where((ri // chunk == ci // chunk) & (ci <= ri), 1.0, 0.0).astype(BF16)
    b_s[...] = _dot_x3_lhs(tril, g)

    if carry:
        @pl.when(pl.program_id(1) == 0)
        def _():
            for h in range(HG_HEADS):
                st_ref[h] = s0_ref[0, h].T

    rowid = lax.broadcasted_iota(jnp.int32, (chunk, 1), 0)
    ng = ng_ref[...]

    def do_chunk(c, _):
        r0 = pl.multiple_of(c * chunk, chunk)
        rs = pl.ds(r0, chunk)
        for h in range(HG_HEADS):
            cs_ = slice(h * HG_DK, (h + 1) * HG_DK)
            q = q_s[rs, cs_]
            k = k_s[rs, cs_]
            b = b_s[rs, cs_]
            v = h4_ref[rs, 2 * w + h * HG_DK:2 * w + (h + 1) * HG_DK]
            hg = h4_ref[rs, 3 * w + h * HG_DK:3 * w + (h + 1) * HG_DK]
            if carry:
                st = st_ref[h]
            else:
                st = s0_ref[c, h].T
            qd = _pad_rows((q * jnp.exp(b)).astype(BF16), MM_ROWS)
            o = _dot_nt(qd, st.astype(BF16))[:chunk]
            for s in range(chunk):
                keep = rowid >= s
                e = jnp.where(keep, jnp.exp(jnp.where(keep, b - b[s:s + 1, :], 0.0)), 0.0)
                col = jnp.sum(q * e * k[s:s + 1, :], axis=-1, keepdims=True)
                o = o + col * v[s:s + 1, :]
            b_last = b[chunk - 1:chunk, :]
            kd = _pad_rows((k * jnp.exp(b_last - b)).astype(BF16), MM_ROWS)
            vd = _pad_rows(v.astype(BF16), MM_ROWS)
            st_new = st * jnp.exp(b_last) + _dot_tn(vd, kd)
            if carry:
                st_ref[h] = st_new
            else:
                sout_ref[c, h] = st_new.T
            on = o * lax.rsqrt(jnp.mean(o * o, axis=-1, keepdims=True) + RMS_EPS) * ng
            o_ref[rs, cs_] = on * _sigmoid(hg)
        return 0

    lax.fori_loop(0, nchunk, do_chunk, 0)

    if carry:
        @pl.when(pl.program_id(1) == pl.num_programs(1) - 1)
        def _():
            for h in range(HG_HEADS):
                sout_ref[0, h] = st_ref[h].T


def _hgrn(h4, s0, lb_logits, norm_g, layer, nb, seq):
    n = h4.shape[0]
    chunk = math.gcd(seq, HG_CHUNK)
    carry = seq > chunk
    if carry:
        tl = _tile(seq, 256)
        grid = (nb, seq // tl)
        row_map = lambda b, l: (b * (seq // tl) + l, 0)
        st_map = lambda b, l: (b, 0, 0, 0)
        st_blk = (1, HG_HEADS, HG_DK, HG_DK)
        sem = ("parallel", "arbitrary")
    else:
        bt = _tile(nb, 8)
        tl = bt * seq
        grid = (nb // bt, 1)
        row_map = lambda b, l: (b, 0)
        st_map = lambda b, l: (b, 0, 0, 0)
        st_blk = (bt, HG_HEADS, HG_DK, HG_DK)
        sem = ("parallel", "arbitrary")
    kern = functools.partial(_hgrn_kernel, layer=layer, chunk=chunk, carry=carry)
    return pl.pallas_call(
        kern,
        out_shape=(jax.ShapeDtypeStruct((n, HG_WIDTH), F32),
                   jax.ShapeDtypeStruct((nb, HG_HEADS, HG_DK, HG_DK), F32)),
        grid=grid,
        in_specs=[pl.BlockSpec((tl, 4 * HG_WIDTH), row_map),
                  pl.BlockSpec(st_blk, st_map),
                  pl.BlockSpec((N_EVEN, HG_WIDTH), lambda b, l: (0, 0)),
                  pl.BlockSpec((1, HG_DK), lambda b, l: (0, 0))],
        out_specs=(pl.BlockSpec((tl, HG_WIDTH), row_map), pl.BlockSpec(st_blk, st_map)),
        scratch_shapes=[pltpu.VMEM((HG_HEADS, HG_DK, HG_DK), F32),
                        pltpu.VMEM((tl, HG_WIDTH), F32), pltpu.VMEM((tl, HG_WIDTH), F32),
                        pltpu.VMEM((tl, HG_WIDTH), F32)],
        compiler_params=_params(*sem), name="hgrn2")(h4, s0, lb_logits, norm_g)


def _lambda(lamp_ref):
    lam_init = lamp_ref[4:5, 0:1]
    l1 = jnp.sum(lamp_ref[0:1, :] * lamp_ref[1:2, :], axis=-1, keepdims=True)
    l2 = jnp.sum(lamp_ref[2:3, :] * lamp_ref[3:4, :], axis=-1, keepdims=True)
    return jnp.exp(l1) - jnp.exp(l2) + lam_init, lam_init


def _diff_finish(o0, o1, lam, lam_init, ng):
    o = o0 - lam * o1
    return o * lax.rsqrt(jnp.mean(o * o, axis=-1, keepdims=True) + RMS_EPS) * ng * (1.0 - lam_init)


def _lane_fold(x, op):
    parts = [x[:, c * LANES:(c + 1) * LANES] for c in range(x.shape[1] // LANES)]
    return functools.reduce(op, parts)


def _attn_prompt_kernel(q_ref, k_ref, v_ref, lamp_ref, ng_ref, o_ref, kb_s, vb_s, s_s, mp_s, lp_s,
                        acc_s):
    tq = q_ref.shape[0]
    i = pl.program_id(2)

    @pl.when(i == 0)
    def _():
        kb_s[...] = k_ref[...].astype(BF16)
        vb_s[...] = v_ref[...].astype(BF16)

    lane = lax.broadcasted_iota(jnp.int32, (tq, DA_DV), 1)
    qs = q_ref[...] * (DA_DH ** -0.5)
    q2 = jnp.concatenate([jnp.where(lane < DA_DH, qs, 0.0), jnp.where(lane >= DA_DH, qs, 0.0)],
                         axis=0).astype(BF16)
    mp_s[...] = jnp.full(mp_s.shape, NEG_BIG, F32)

    def scores(j, masked):
        r0 = pl.multiple_of(j * tq, tq)
        s = _dot_nt(q2, kb_s[pl.ds(r0, tq), :])
        if masked:
            row = lax.broadcasted_iota(jnp.int32, (2 * tq, tq), 0) % tq
            col = lax.broadcasted_iota(jnp.int32, (2 * tq, tq), 1)
            s = jnp.where(col <= row, s, NEG_BIG)
        s_s[j] = s
        mp_s[...] = jnp.maximum(mp_s[...], _lane_fold(s, jnp.maximum))

    def pass1(j, c):
        scores(j, False)
        return c

    lax.fori_loop(0, i, pass1, 0)
    scores(i, True)
    m = jnp.max(mp_s[...], axis=-1, keepdims=True)
    lp_s[...] = jnp.zeros(lp_s.shape, F32)
    acc_s[...] = jnp.zeros(acc_s.shape, F32)

    def pass2(j, c):
        r0 = pl.multiple_of(j * tq, tq)
        p = jnp.exp(s_s[j] - m)
        lp_s[...] += _lane_fold(p, lambda a, b: a + b)
        acc_s[...] += _dot(p.astype(BF16), vb_s[pl.ds(r0, tq), :])
        return c

    lax.fori_loop(0, i + 1, pass2, 0)
    l = jnp.sum(lp_s[...], axis=-1, keepdims=True)
    lam, lam_init = _lambda(lamp_ref)
    o0 = acc_s[0:tq, :] / l[0:tq, :]
    o1 = acc_s[tq:2 * tq, :] / l[tq:2 * tq, :]
    o_ref[...] = _diff_finish(o0, o1, lam, lam_init, ng_ref[...])


def _attn_prompt(q, k, v, lamp, norm_g, nb, seq):
    n = q.shape[0]
    tq = _tile(seq, 256)
    nq = seq // tq
    return pl.pallas_call(
        _attn_prompt_kernel, out_shape=jax.ShapeDtypeStruct((n, DA_WIDTH), F32),
        grid=(nb, DA_HEADS, nq),
        in_specs=[pl.BlockSpec((tq, DA_DV), lambda b, h, i: (b * nq + i, h)),
                  pl.BlockSpec((seq, DA_DV), lambda b, h, i: (b, h)),
                  pl.BlockSpec((seq, DA_DV), lambda b, h, i: (b, h)),
                  pl.BlockSpec((8, LANES), lambda b, h, i: (0, 0)),
                  pl.BlockSpec((1, DA_DV), lambda b, h, i: (0, 0))],
        out_specs=pl.BlockSpec((tq, DA_DV), lambda b, h, i: (b * nq + i, h)),
        scratch_shapes=[pltpu.VMEM((seq, DA_DV), BF16), pltpu.VMEM((seq, DA_DV), BF16),
                        pltpu.VMEM((nq, 2 * tq, tq), F32), pltpu.VMEM((2 * tq, LANES), F32),
                        pltpu.VMEM((2 * tq, LANES), F32), pltpu.VMEM((2 * tq, DA_DV), F32)],
        compiler_params=_params("parallel", "parallel", "arbitrary"),
        name="attn_prompt")(q, k, v, lamp, norm_g)


def _attn_sample_kernel(pt_ref, q_ref, *refs, n_pages):
    kp = refs[:n_pages]
    vp = refs[n_pages:2 * n_pages]
    kn_ref, vn_ref, lamp_ref, ng_ref, o_ref = refs[2 * n_pages:]
    t = q_ref.shape[0]
    nrow = 2 * DA_HEADS * t
    hrow = 2 * t

    qs = q_ref[...] * (DA_DH ** -0.5)
    qt = jnp.concatenate([qs] * (2 * DA_HEADS), axis=0)
    r = lax.broadcasted_iota(jnp.int32, (nrow, DA_QK), 0) // t
    c = lax.broadcasted_iota(jnp.int32, (nrow, DA_QK), 1) // DA_DH
    qbd = jnp.where(r == c, qt, 0.0).astype(BF16)

    s_pages = [_dot(qbd, kp[i][...].astype(BF16)) for i in range(n_pages)]
    kb = _pad_rows(kn_ref[...], PAGE_SIZE).astype(BF16)
    vb = _pad_rows(vn_ref[...], PAGE_SIZE).astype(BF16)
    r = lax.broadcasted_iota(jnp.int32, (nrow, PAGE_SIZE), 0) % t
    c = lax.broadcasted_iota(jnp.int32, (nrow, PAGE_SIZE), 1)
    s_new = jnp.where(c <= r, _dot_nt(qbd, kb), NEG_BIG)
    s_all = s_pages + [s_new]
    m = jnp.max(functools.reduce(jnp.maximum, s_all), axis=-1, keepdims=True)
    p_all = [jnp.exp(s - m) for s in s_all]
    l = jnp.sum(functools.reduce(lambda a, b: a + b, p_all), axis=-1, keepdims=True)
    pb = [p.astype(BF16) for p in p_all]

    lam, lam_init = _lambda(lamp_ref)
    ng = ng_ref[...]
    for h in range(DA_HEADS):
        rs = slice(h * hrow, (h + 1) * hrow)
        acc = _dot(pb[n_pages][rs, :], vb[:, h * DA_DV:(h + 1) * DA_DV])
        for i in range(n_pages):
            acc = acc + _dot(pb[i][rs, :], vp[i][:, h, :].astype(BF16))
        o = acc / l[rs, :]
        o_ref[:, h * DA_DV:(h + 1) * DA_DV] = _diff_finish(o[0:t, :], o[t:2 * t, :], lam, lam_init, ng)


def _attn_sample(q, k_new, v_new, cache_kt, cache_v, page_table, lamp, norm_g, layer, nb, seq):
    n = q.shape[0]
    n_pages = page_table.shape[1]
    row_map = lambda b, pt: (b, 0)
    k_specs = [pl.BlockSpec((None, None, DA_QK, PAGE_SIZE),
                            lambda b, pt, i=i: (layer, pt[b * n_pages + i], 0, 0))
               for i in range(n_pages)]
    v_specs = [pl.BlockSpec((None, None, PAGE_SIZE, DA_HEADS, DA_DV),
                            lambda b, pt, i=i: (layer, pt[b * n_pages + i], 0, 0, 0))
               for i in range(n_pages)]
    grid_spec = pltpu.PrefetchScalarGridSpec(
        num_scalar_prefetch=1, grid=(nb,),
        in_specs=[pl.BlockSpec((seq, DA_QK), row_map)] + k_specs + v_specs
        + [pl.BlockSpec((seq, DA_QK), row_map), pl.BlockSpec((seq, DA_WIDTH), row_map),
           pl.BlockSpec((8, LANES), lambda b, pt: (0, 0)),
           pl.BlockSpec((1, DA_DV), lambda b, pt: (0, 0))],
        out_specs=pl.BlockSpec((seq, DA_WIDTH), row_map))
    return pl.pallas_call(
        functools.partial(_attn_sample_kernel, n_pages=n_pages),
        out_shape=jax.ShapeDtypeStruct((n, DA_WIDTH), F32),
        grid_spec=grid_spec, compiler_params=_params("parallel"),
        name="attn_sample")(page_table.reshape(-1), q, *([cache_kt] * n_pages),
                            *([cache_v] * n_pages), k_new, v_new, lamp, norm_g)


def _even_out_kernel(oh_ref, od_ref, x_ref, w_ref, g_ref, b_ref, o_ref):
    h = (_dot(oh_ref[...].astype(BF16), w_ref[0:HG_WIDTH, :])
         + _dot(od_ref[...].astype(BF16), w_ref[HG_WIDTH:D_MODEL, :]))
    o_ref[...] = _layer_norm(ALPHA * x_ref[...] + h, g_ref[...], b_ref[...])


def _even_out(oh, od, x, w_out_bf, g, b):
    n = x.shape[0]
    tm = _tile(n, 512)
    vec = pl.BlockSpec((1, D_MODEL), lambda i: (0, 0))
    return pl.pallas_call(
        _even_out_kernel, out_shape=jax.ShapeDtypeStruct((n, D_MODEL), F32), grid=(n // tm,),
        in_specs=[_row_spec(tm, HG_WIDTH), _row_spec(tm, DA_WIDTH), _row_spec(tm, D_MODEL),
                  _const_spec((D_MODEL, D_MODEL)), vec, vec],
        out_specs=_row_spec(tm, D_MODEL),
        compiler_params=_params("parallel"), name="even_out")(oh, od, x, w_out_bf, g, b)


FF_CHUNK = 1024


def _mlp_kernel(x_ref, p_ref, up_ref, down_ref, g_ref, b_ref, gate_ref, proj_ref, o_ref):
    x = x_ref[...]
    xb = x.astype(BF16)
    acc = jnp.zeros(x.shape, F32)
    for c in range(D_FF // FF_CHUNK):
        cs = slice(c * FF_CHUNK, (c + 1) * FF_CHUNK)
        hid = jnp.maximum(_dot(xb, up_ref[:, cs]), 0.0)
        acc = acc + _dot((hid * hid).astype(BF16), down_ref[cs, :])
    x2 = _layer_norm(ALPHA * x + acc, g_ref[...], b_ref[...])
    gate = _sigmoid(_dot(x2.astype(BF16), gate_ref[...]))
    o_ref[...] = x2 + gate * _dot(p_ref[...].astype(BF16), proj_ref[...])


def _mlp(x, p, up_bf, down_bf, g, b, gate_bf, proj_bf):
    n = x.shape[0]
    tm = _tile(n, 512)
    vec = pl.BlockSpec((1, D_MODEL), lambda i: (0, 0))
    return pl.pallas_call(
        _mlp_kernel, out_shape=jax.ShapeDtypeStruct((n, D_MODEL), F32), grid=(n // tm,),
        in_specs=[_row_spec(tm, D_MODEL), _row_spec(tm, PLE_DIM),
                  _const_spec((D_MODEL, D_FF)), _const_spec((D_FF, D_MODEL)), vec, vec,
                  _const_spec((D_MODEL, D_MODEL)), _const_spec((PLE_DIM, D_MODEL))],
        out_specs=_row_spec(tm, D_MODEL),
        compiler_params=_params("parallel"), name="mlp")(x, p, up_bf, down_bf, g, b, gate_bf, proj_bf)


def _rwkv_pre_kernel(*refs, has_vfirst):
    if has_vfirst:
        (x_ref, xp_ref, vf_ref, mix_ref, wr_ref, wk_ref, wv_ref, w0_ref, w1_ref, w2_ref,
         a0_ref, a1_ref, a2_ref, v0_ref, v1_ref, v2_ref, g1_ref, g2_ref,
         r_o, k_o, v_o, g_o, dec_o, a_o) = refs
    else:
        (x_ref, xp_ref, mix_ref, wr_ref, wk_ref, wv_ref, w0_ref, w1_ref, w2_ref,
         a0_ref, a1_ref, a2_ref, g1_ref, g2_ref,
         r_o, k_o, v_o, g_o, dec_o, a_o) = refs
    x = x_ref[...]
    xx = xp_ref[...] - x
    xr, xw, xk, xv, xa, xg = ((x + xx * mix_ref[m:m + 1, :]).astype(BF16) for m in range(6))
    r = _dot(xr, wr_ref[...])
    k = _dot(xk, wk_ref[...])
    v = _dot(xv, wv_ref[...])
    lw = w0_ref[...] + _dot(jnp.tanh(_dot(xw, w1_ref[...])).astype(BF16), w2_ref[...])
    w = -(jnp.maximum(-lw, 0.0) + jnp.log(1.0 + jnp.exp(-jnp.abs(lw)))) - 0.5
    if has_vfirst:
        mixv = _sigmoid(v0_ref[...] + _dot(_dot(xv, v1_ref[...]).astype(BF16), v2_ref[...]))
        v = v + (vf_ref[...] - v) * mixv
    a = _sigmoid(a0_ref[...] + _dot(_dot(xa, a1_ref[...]).astype(BF16), a2_ref[...]))
    g = _dot(_sigmoid(_dot(xg, g1_ref[...])).astype(BF16), g2_ref[...])
    r_o[...] = r
    k_o[...] = k
    v_o[...] = v
    g_o[...] = g
    dec_o[...] = jnp.exp(-jnp.exp(w))
    a_o[...] = a


def _rwkv_pre(x, x_prev, v_first, wts):
    n = x.shape[0]
    tm = _tile(n, 256)
    has_vfirst = v_first is not None
    vec = pl.BlockSpec((1, D_MODEL), lambda i: (0, 0))
    acts = [x, x_prev] + ([v_first] if has_vfirst else [])
    names = ['mix', 'w_r', 'w_k', 'w_v', 'w0', 'w1', 'w2', 'a0', 'a1', 'a2']
    if has_vfirst:
        names += ['v0', 'v1', 'v2']
    names += ['g1', 'g2']
    ws = [wts[nm] for nm in names]
    w_specs = [vec if w.shape == (1, D_MODEL) else _const_spec(w.shape) for w in ws]
    out = jax.ShapeDtypeStruct((n, D_MODEL), F32)
    return pl.pallas_call(
        functools.partial(_rwkv_pre_kernel, has_vfirst=has_vfirst),
        out_shape=(out,) * 6, grid=(n // tm,),
        in_specs=[_row_spec(tm, D_MODEL)] * len(acts) + w_specs,
        out_specs=(_row_spec(tm, D_MODEL),) * 6,
        compiler_params=_params("parallel"), name="rwkv_pre")(*acts, *ws)


RW_TBLK = 32


def _rwkv_scan_kernel(r_ref, k_ref, v_ref, dec_ref, a_ref, kk_ref, ka_ref, rk_ref, lg_ref, lb_ref,
                      s0_ref, y_ref, sout_ref, s_s, y_s):
    tb = pl.program_id(1)
    nt = r_ref.shape[0]
    n = RW_HEAD

    @pl.when(tb == 0)
    def _():
        s_s[...] = s0_ref[...]

    def step(t, _):
        r, k, v, dec, ag = r_ref[t], k_ref[t], v_ref[t], dec_ref[t], a_ref[t]
        kk = k * kk_ref[...]
        kk = kk / jnp.maximum(jnp.sqrt(jnp.sum(kk * kk, axis=0, keepdims=True)), 1e-12)
        a = -kk
        b = kk * ag
        k2 = k * (1.0 + (ag - 1.0) * ka_ref[...])
        wr = dec * r
        br = jnp.sum(b * r, axis=0, keepdims=True)
        kr = jnp.sum(k2 * r, axis=0, keepdims=True)

        def rows(i8, _):
            i0 = pl.multiple_of(i8 * SUBLANES, SUBLANES)
            v8 = v_ref[t, pl.ds(i0, SUBLANES), :]
            ys = []
            for ii in range(SUBLANES):
                si = s_s[i0 + ii]
                sa = jnp.sum(si * a, axis=0, keepdims=True)
                qs = jnp.sum(si * wr, axis=0, keepdims=True)
                vi = v8[ii:ii + 1, :]
                s_s[i0 + ii] = si * dec + sa * b + vi * k2
                ys.append(qs + sa * br + vi * kr)
            y_s[pl.ds(i0, SUBLANES), :] = jnp.concatenate(ys, axis=0)
            return 0

        lax.fori_loop(0, n // SUBLANES, rows, 0)
        y = y_s[...]
        mu = jnp.mean(y, axis=0, keepdims=True)
        yc = y - mu
        var = jnp.mean(yc * yc, axis=0, keepdims=True)
        yn = yc * lax.rsqrt(var + GN_EPS) * lg_ref[...] + lb_ref[...]
        y_ref[t] = yn + jnp.sum(r * k2 * rk_ref[...], axis=0, keepdims=True) * v
        return 0

    lax.fori_loop(0, nt, step, 0)

    @pl.when(tb == pl.num_programs(1) - 1)
    def _():
        sout_ref[...] = s_s[...]


def _rwkv_scan(r, k, v, dec, a, s0, wts, nb, seq):
    n = r.shape[0]
    hd = RW_HEAD
    batch_lanes = nb % LANES == 0
    if batch_lanes:
        nbg = nb // LANES
        groups, chains = RW_HEADS * nbg, LANES
        to_l = lambda x: (x.reshape(nbg, LANES, seq, RW_HEADS, hd).transpose(0, 3, 2, 4, 1)
                          .reshape(groups, seq, hd, chains))
        from_l = lambda y: (y.reshape(nbg, RW_HEADS, seq, hd, LANES).transpose(0, 4, 2, 1, 3)
                            .reshape(n, D_MODEL))
        par = lambda p: jnp.broadcast_to(p.reshape(1, RW_HEADS, hd, 1),
                                         (nbg, RW_HEADS, hd, chains)).reshape(groups, hd, chains)
        st_in = lambda s: (s.reshape(nbg, LANES, RW_HEADS, hd, hd).transpose(0, 2, 3, 4, 1)
                           .reshape(groups, hd, hd, chains))
        st_out = lambda s: (s.reshape(nbg, RW_HEADS, hd, hd, LANES).transpose(0, 4, 1, 2, 3)
                            .reshape(nb, RW_HEADS, hd, hd))
    else:
        groups, chains = 1, nb * RW_HEADS
        to_l = lambda x: (x.reshape(nb, seq, RW_HEADS, hd).transpose(1, 3, 0, 2)
                          .reshape(1, seq, hd, chains))
        from_l = lambda y: (y.reshape(seq, hd, nb, RW_HEADS).transpose(2, 0, 3, 1)
                            .reshape(n, D_MODEL))
        par = lambda p: jnp.broadcast_to(p.reshape(RW_HEADS, hd).T.reshape(1, hd, 1, RW_HEADS),
                                         (1, hd, nb, RW_HEADS)).reshape(1, hd, chains)
        st_in = lambda s: s.transpose(2, 3, 0, 1).reshape(1, hd, hd, chains)
        st_out = lambda s: s.reshape(hd, hd, nb, RW_HEADS).transpose(2, 3, 0, 1)
    tt = _tile(seq, RW_TBLK)
    act = pl.BlockSpec((None, tt, hd, chains), lambda g, t: (g, t, 0, 0))
    prm = pl.BlockSpec((None, hd, chains), lambda g, t: (g, 0, 0))
    st = pl.BlockSpec((None, hd, hd, chains), lambda g, t: (g, 0, 0, 0))
    y, s_new = pl.pallas_call(
        _rwkv_scan_kernel,
        out_shape=(jax.ShapeDtypeStruct((groups, seq, hd, chains), F32),
                   jax.ShapeDtypeStruct((groups, hd, hd, chains), F32)),
        grid=(groups, seq // tt),
        in_specs=[act] * 5 + [prm] * 5 + [st],
        out_specs=(act, st),
        scratch_shapes=[pltpu.VMEM((hd, hd, chains), F32), pltpu.VMEM((hd, chains), F32)],
        compiler_params=_params("parallel", "arbitrary"),
        name="rwkv_scan")(to_l(r), to_l(k), to_l(v), to_l(dec), to_l(a),
                          par(wts['k_k']), par(wts['k_a']), par(wts['r_k']),
                          par(wts['lnx_g']), par(wts['lnx_b']), st_in(s0))
    return from_l(y), st_out(s_new)


def _rwkv_post_kernel(y_ref, g_ref, x_ref, wo_ref, n1g_ref, n1b_ref, o_ref):
    h = _dot((y_ref[...] * g_ref[...]).astype(BF16), wo_ref[...])
    o_ref[...] = _layer_norm(ALPHA * x_ref[...] + h, n1g_ref[...], n1b_ref[...])


def _rwkv_post(y, g, x, wo_bf, n1g, n1b):
    n = x.shape[0]
    tm = _tile(n, 512)
    vec = pl.BlockSpec((1, D_MODEL), lambda i: (0, 0))
    return pl.pallas_call(
        _rwkv_post_kernel, out_shape=jax.ShapeDtypeStruct((n, D_MODEL), F32), grid=(n // tm,),
        in_specs=[_row_spec(tm, D_MODEL)] * 3 + [_const_spec((D_MODEL, D_MODEL)), vec, vec],
        out_specs=_row_spec(tm, D_MODEL),
        compiler_params=_params("parallel"), name="rwkv_post")(y, g, x, wo_bf, n1g, n1b)


def _trunk(x3, p4, hg_s0, wkv_s0, shift0, cache_k, cache_v, page_table, W):
    nb, seq, _ = x3.shape
    n = nb * seq
    x = x3.reshape(n, D_MODEL)
    new_k, new_v, new_hg, new_wkv, new_shift = [], [], [], [], []
    v_first = None
    for i in range(DEPTH):
        j = i // 2
        if i % 2 == 0:
            h4, q, k, v = _even_in(x, W['w_in_even'][j])
            oh, s_hg = _hgrn(h4, hg_s0[j], W['hg_lb_logits'], W['hg_norm_g'][j], j, nb, seq)
            if page_table is None:
                od = _attn_prompt(q, k, v, W['lamp'][j], W['da_norm_g'][j], nb, seq)
            else:
                od = _attn_sample(q, k, v, cache_k, cache_v, page_table, W['lamp'][j],
                                  W['da_norm_g'][j], j, nb, seq)
            x = _even_out(oh, od, x, W['w_out_even'][j], W['ln1_g'][i], W['ln1_b'][i])
            new_k.append(k.reshape(nb, seq, DA_HEADS, 2, DA_DH))
            new_v.append(v.reshape(nb, seq, DA_HEADS, DA_DV))
            new_hg.append(s_hg)
        else:
            xs = x.reshape(nb, seq, D_MODEL)
            x_prev = jnp.concatenate([shift0[j][:, None, :], xs[:, :-1]], axis=1).reshape(n, D_MODEL)
            r, k, v, g, dec, a = _rwkv_pre(x, x_prev, v_first, W['rw'][j])
            if j == 0:
                v_first = v
            y, s_wkv = _rwkv_scan(r, k, v, dec, a, wkv_s0[j], W['rw_scan'][j], nb, seq)
            new_shift.append(xs[:, -1])
            new_wkv.append(s_wkv)
            x = _rwkv_post(y, g, x, W['rw_w_o'][j], W['ln1_g'][i], W['ln1_b'][i])
        x = _mlp(x, p4[i].reshape(n, PLE_DIM), W['mlp_up'][i], W['mlp_down'][i],
                 W['ln2_g'][i], W['ln2_b'][i], W['ple_gate'][i], W['ple_proj'][i])
    return (x.reshape(nb, seq, D_MODEL), jnp.stack(new_k), jnp.stack(new_v), jnp.stack(new_hg),
            jnp.stack(new_wkv), jnp.stack(new_shift))


def kernel(x_prompt, x_sample, cache_k, cache_v, state_hgrn, state_wkv, state_shift, page_table,
           p_prompt, p_sample, w_in_even, w_out_even, hg_lb_logits, hg_norm_g,
           da_lam_q1, da_lam_k1, da_lam_q2, da_lam_k2, da_norm_g,
           rw_mix, rw_w_r, rw_w_k, rw_w_v, rw_w_o, rw_w0, rw_w1, rw_w2,
           rw_a0, rw_a1, rw_a2, rw_v0, rw_v1, rw_v2, rw_g1, rw_g2,
           rw_k_k, rw_k_a, rw_r_k, rw_lnx_g, rw_lnx_b,
           ln1_g, ln1_b, ln2_g, ln2_b, mlp_up, mlp_down, ple_proj, ple_gate):
    bf = lambda w: w.astype(BF16)
    vec = lambda w: w.reshape(1, -1)

    def lam_rows(j):
        pad = lambda a: jnp.pad(a.astype(F32), (0, LANES - a.shape[0]))
        lam_init = 0.8 - 0.6 * math.exp(-0.3 * (2 * j))
        rows = [pad(da_lam_q1[j]), pad(da_lam_k1[j]), pad(da_lam_q2[j]), pad(da_lam_k2[j]),
                jnp.full((LANES,), lam_init, F32)]
        return jnp.stack(rows + [jnp.zeros((LANES,), F32)] * 3)

    rw = []
    for j in range(N_ODD):
        d = dict(mix=rw_mix[j], w_r=bf(rw_w_r[j]), w_k=bf(rw_w_k[j]), w_v=bf(rw_w_v[j]),
                 w0=vec(rw_w0[j]), w1=bf(rw_w1[j]), w2=bf(rw_w2[j]),
                 a0=vec(rw_a0[j]), a1=bf(rw_a1[j]), a2=bf(rw_a2[j]),
                 g1=bf(rw_g1[j]), g2=bf(rw_g2[j]))
        if j > 0:
            d.update(v0=vec(rw_v0[j - 1]), v1=bf(rw_v1[j - 1]), v2=bf(rw_v2[j - 1]))
        rw.append(d)
    W = dict(
        w_in_even=[bf(w_in_even[j]) for j in range(N_EVEN)],
        w_out_even=[bf(w_out_even[j]) for j in range(N_EVEN)],
        hg_lb_logits=hg_lb_logits,
        hg_norm_g=[vec(hg_norm_g[j]) for j in range(N_EVEN)],
        da_norm_g=[vec(da_norm_g[j]) for j in range(N_EVEN)],
        lamp=[lam_rows(j) for j in range(N_EVEN)],
        rw=rw,
        rw_scan=[dict(k_k=rw_k_k[j], k_a=rw_k_a[j], r_k=rw_r_k[j], lnx_g=rw_lnx_g[j],
                      lnx_b=rw_lnx_b[j]) for j in range(N_ODD)],
        rw_w_o=[bf(rw_w_o[j]) for j in range(N_ODD)],
        ln1_g=[vec(ln1_g[i]) for i in range(DEPTH)], ln1_b=[vec(ln1_b[i]) for i in range(DEPTH)],
        ln2_g=[vec(ln2_g[i]) for i in range(DEPTH)], ln2_b=[vec(ln2_b[i]) for i in range(DEPTH)],
        mlp_up=[bf(mlp_up[i]) for i in range(DEPTH)], mlp_down=[bf(mlp_down[i]) for i in range(DEPTH)],
        ple_gate=[bf(ple_gate[i]) for i in range(DEPTH)], ple_proj=[bf(ple_proj[i]) for i in range(DEPTH)],
    )
    nbp = x_prompt.shape[0]
    dt = x_prompt.dtype
    hg0 = jnp.zeros((N_EVEN, nbp, HG_HEADS, HG_DK, HG_DK), dt)
    wkv0 = jnp.zeros((N_ODD, nbp, RW_HEADS, RW_HEAD, RW_HEAD), dt)
    sh0 = jnp.zeros((N_ODD, nbp, D_MODEL), dt)
    y_p, k_p, v_p, hg_p, wkv_p, sh_p = _trunk(x_prompt, p_prompt, hg0, wkv0, sh0, None, None, None, W)
    ckt = jnp.transpose(cache_k, (0, 1, 3, 4, 5, 2)).reshape(
        cache_k.shape[0], cache_k.shape[1], DA_QK, PAGE_SIZE)
    y_s, k_s, v_s, hg_s, wkv_s, sh_s = _trunk(x_sample, p_sample, state_hgrn, state_wkv, state_shift,
                                              ckt, cache_v, page_table, W)
    return (y_p, y_s, k_p, v_p, k_s, v_s, hg_p, hg_s, wkv_p, wkv_s, sh_p, sh_s)
```

```python
import functools
import math

import jax
import jax.numpy as jnp
from jax import lax
from jax.experimental import pallas as pl
from jax.experimental.pallas import tpu as pltpu

F32 = jnp.float32
BF16 = jnp.bfloat16

D_MODEL = 1024
DEPTH = 4
PAGE_SIZE = 128
N_EVEN = (DEPTH + 1) // 2
N_ODD = DEPTH // 2
PLE_DIM = 256
D_FF = 4 * D_MODEL
HG_WIDTH = D_MODEL // 2
HG_HEADS = 4
HG_DK = HG_WIDTH // HG_HEADS
DA_WIDTH = D_MODEL - HG_WIDTH
DA_HEADS = 4
DA_DV = DA_WIDTH // DA_HEADS
DA_DH = DA_DV // 2
DA_QK = DA_HEADS * 2 * DA_DH
IN_COLS = 4 * HG_WIDTH + 2 * DA_QK + DA_WIDTH
RW_HEAD = 64
RW_HEADS = D_MODEL // RW_HEAD
ALPHA = (2.0 * DEPTH) ** 0.25
LN_EPS = 1e-5
RMS_EPS = 1e-6
GN_EPS = 64e-5
F_MIN = 1e-30
NEG_BIG = -1e30

LANES = 128
SUBLANES = 8
VMEM_LIMIT = 52 * 1024 * 1024
HG_CHUNK = 16
MM_ROWS = 16


def _params(*sem):
    return pltpu.CompilerParams(dimension_semantics=sem, vmem_limit_bytes=VMEM_LIMIT)


def _const_spec(shape):
    nd = len(shape)
    return pl.BlockSpec(shape, lambda *_: (0,) * nd, pipeline_mode=pl.Buffered(1))


def _row_spec(rows, cols):
    return pl.BlockSpec((rows, cols), lambda i: (i, 0))


def _tile(n, pref):
    t = min(n, pref)
    while n % t:
        t //= 2
    return t


def _dot(a, b):
    return jnp.dot(a, b, preferred_element_type=F32)


def _dot_nt(a, b):
    return lax.dot_general(a, b, (((1,), (1,)), ((), ())), preferred_element_type=F32)


def _dot_tn(a, b):
    return lax.dot_general(a, b, (((0,), (0,)), ((), ())), preferred_element_type=F32)


def _split3(x):
    h = x.astype(BF16)
    r = x - h.astype(F32)
    m = r.astype(BF16)
    l = (r - m.astype(F32)).astype(BF16)
    return h, m, l


def _dot_x3(x, rhs_bf16):
    h, m, l = _split3(x)
    return _dot(h, rhs_bf16) + _dot(m, rhs_bf16) + _dot(l, rhs_bf16)


def _dot_x3_lhs(lhs_bf16, x):
    h, m, l = _split3(x)
    return _dot(lhs_bf16, h) + _dot(lhs_bf16, m) + _dot(lhs_bf16, l)


def _layer_norm(z, g, b):
    mu = jnp.mean(z, axis=-1, keepdims=True)
    zc = z - mu
    var = jnp.mean(zc * zc, axis=-1, keepdims=True)
    return zc * lax.rsqrt(var + LN_EPS) * g + b


def _sigmoid(x):
    return 1.0 / (1.0 + jnp.exp(-x))


def _head_ones(width):
    r = lax.broadcasted_iota(jnp.int32, (LANES, LANES), 0) // width
    c = lax.broadcasted_iota(jnp.int32, (LANES, LANES), 1) // width
    return jnp.where(r == c, 1.0, 0.0).astype(BF16)


def _segsum(x, ones):
    cols = x.shape[1] // LANES
    parts = [_dot_x3(x[:, c * LANES:(c + 1) * LANES], ones) for c in range(cols)]
    return parts[0] if cols == 1 else jnp.concatenate(parts, axis=1)


def _even_in_kernel(x_ref, w_ref, *refs):
    h4_ref, q_ref, k_ref, v_ref = refs[-4:]
    xb = x_ref[...].astype(BF16)
    cw = HG_WIDTH
    for c in range(4):
        h4_ref[:, c * cw:(c + 1) * cw] = _dot(xb, w_ref[:, c * cw:(c + 1) * cw])
    q0 = 4 * HG_WIDTH
    q_ref[...] = _dot(xb, w_ref[:, q0:q0 + DA_QK])
    k_ref[...] = _dot(xb, w_ref[:, q0 + DA_QK:q0 + 2 * DA_QK])
    v_ref[...] = _dot(xb, w_ref[:, q0 + 2 * DA_QK:q0 + 2 * DA_QK + DA_WIDTH])


def _even_in(x, w_in_bf, layer, kv_stacks):
    n = x.shape[0]
    tm = _tile(n, 512)
    slab = lambda width: pl.BlockSpec((None, tm, width), lambda i: (layer, i, 0))
    outs = (jax.ShapeDtypeStruct((n, 4 * HG_WIDTH), F32), jax.ShapeDtypeStruct((n, DA_QK), F32),
            jax.ShapeDtypeStruct((N_EVEN, n, DA_QK), F32), jax.ShapeDtypeStruct((N_EVEN, n, DA_WIDTH), F32))
    in_specs = [_row_spec(tm, D_MODEL), _const_spec((D_MODEL, IN_COLS))]
    args = [x, w_in_bf]
    aliases = {}
    if kv_stacks is not None:
        in_specs += [pl.BlockSpec(memory_space=pl.ANY)] * 2
        args += list(kv_stacks)
        aliases = {2: 2, 3: 3}
    return pl.pallas_call(
        _even_in_kernel, out_shape=outs, grid=(n // tm,),
        in_specs=in_specs,
        out_specs=(_row_spec(tm, 4 * HG_WIDTH), _row_spec(tm, DA_QK), slab(DA_QK), slab(DA_WIDTH)),
        input_output_aliases=aliases,
        compiler_params=_params("parallel"), name="even_in")(*args)


def _pad_rows(x, rows):
    if x.shape[0] >= rows:
        return x
    return jnp.concatenate([x, jnp.zeros((rows - x.shape[0], x.shape[1]), x.dtype)], axis=0)


def _hgrn_kernel(h4_ref, s0_ref, lbl_ref, ng_ref, o_ref, sout_ref,
                 st_ref, q_s, k_s, b_s, *, layer, chunk, carry):
    tl = h4_ref.shape[0]
    nchunk = tl // chunk
    w = HG_WIDTH

    rows = [lbl_ref[i:i + 1, :] for i in range(N_EVEN)]
    mx = functools.reduce(jnp.maximum, rows)
    es = [jnp.exp(r - mx) for r in rows]
    den = functools.reduce(lambda a, b: a + b, es)
    sm = [e / den for e in es]
    cs = functools.reduce(lambda a, b: a + b, sm[:layer + 1])
    lb = cs - sm[0]

    hq = h4_ref[:, 0:w]
    hf = h4_ref[:, w:2 * w]
    q_s[...] = hq * _sigmoid(hq)
    f = lb + (1.0 - lb) * _sigmoid(hf)
    g = jnp.log(jnp.maximum(f, F_MIN))
    k_s[...] = 1.0 - f
    ri = lax.broadcasted_iota(jnp.int32, (tl, tl), 0)
    ci = lax.broadcasted_iota(jnp.int32, (tl, tl), 1)
    tril = jnp.where((ri // chunk == ci // chunk) & (ci <= ri), 1.0, 0.0).astype(BF16)
    b_s[...] = _dot_x3_lhs(tril, g)

    if carry:
        @pl.when(pl.program_id(1) == 0)
        def _():
            for h in range(HG_HEADS):
                st_ref[h] = s0_ref[0, h].T

    rowid = lax.broadcasted_iota(jnp.int32, (chunk, 1), 0)
    ng = ng_ref[...]

    def do_chunk(c, _):
        r0 = pl.multiple_of(c * chunk, chunk)
        rs = pl.ds(r0, chunk)
        for h in range(HG_HEADS):
            cs_ = slice(h * HG_DK, (h + 1) * HG_DK)
            q = q_s[rs, cs_]
            k = k_s[rs, cs_]
            b = b_s[rs, cs_]
            v = h4_ref[rs, 2 * w + h * HG_DK:2 * w + (h + 1) * HG_DK]
            hg = h4_ref[rs, 3 * w + h * HG_DK:3 * w + (h + 1) * HG_DK]
            if carry:
                st = st_ref[h]
            else:
                st = s0_ref[c, h].T
            qd = _pad_rows((q * jnp.exp(b)).astype(BF16), MM_ROWS)
            o = _dot_nt(qd, st.astype(BF16))[:chunk]
            for s in range(chunk):
                keep = rowid >= s
                e = jnp.where(keep, jnp.exp(jnp.where(keep, b - b[s:s + 1, :], 0.0)), 0.0)
                col = jnp.sum(q * e * k[s:s + 1, :], axis=-1, keepdims=True)
                o = o + col * v[s:s + 1, :]
            b_last = b[chunk - 1:chunk, :]
            kd = _pad_rows((k * jnp.exp(b_last - b)).astype(BF16), MM_ROWS)
            vd = _pad_rows(v.astype(BF16), MM_ROWS)
            st_new = st * jnp.exp(b_last) + _dot_tn(vd, kd)
            if carry:
                st_ref[h] = st_new
            else:
                sout_ref[c, h] = st_new.T
            on = o * lax.rsqrt(jnp.mean(o * o, axis=-1, keepdims=True) + RMS_EPS) * ng
            o_ref[rs, cs_] = on * _sigmoid(hg)
        return 0

    lax.fori_loop(0, nchunk, do_chunk, 0)

    if carry:
        @pl.when(pl.program_id(1) == pl.num_programs(1) - 1)
        def _():
            for h in range(HG_HEADS):
                sout_ref[0, h] = st_ref[h].T


def _hgrn(h4, s0_all, s_layer, lb_logits, norm_g, layer, nb, seq):
    n = h4.shape[0]
    chunk = math.gcd(seq, HG_CHUNK)
    carry = seq > chunk
    if carry:
        tl = _tile(seq, 256)
        grid = (nb, seq // tl)
        row_map = lambda b, l: (b * (seq // tl) + l, 0)
        st_map = lambda b, l: (b, 0, 0, 0)
        st_blk = (1, HG_HEADS, HG_DK, HG_DK)
        sem = ("parallel", "arbitrary")
    else:
        bt = _tile(nb, 8)
        tl = bt * seq
        grid = (nb // bt, 1)
        row_map = lambda b, l: (b, 0)
        st_map = lambda b, l: (b, 0, 0, 0)
        st_blk = (bt, HG_HEADS, HG_DK, HG_DK)
        sem = ("parallel", "arbitrary")
    kern = functools.partial(_hgrn_kernel, layer=layer, chunk=chunk, carry=carry)
    return pl.pallas_call(
        kern,
        out_shape=(jax.ShapeDtypeStruct((n, HG_WIDTH), F32),
                   jax.ShapeDtypeStruct((nb, HG_HEADS, HG_DK, HG_DK), F32)),
        grid=grid,
        in_specs=[pl.BlockSpec((tl, 4 * HG_WIDTH), row_map),
                  pl.BlockSpec((None,) + st_blk, lambda b, l: (s_layer, b, 0, 0, 0)),
                  pl.BlockSpec((N_EVEN, HG_WIDTH), lambda b, l: (0, 0)),
                  pl.BlockSpec((1, HG_DK), lambda b, l: (0, 0))],
        out_specs=(pl.BlockSpec((tl, HG_WIDTH), row_map), pl.BlockSpec(st_blk, st_map)),
        scratch_shapes=[pltpu.VMEM((HG_HEADS, HG_DK, HG_DK), F32),
                        pltpu.VMEM((tl, HG_WIDTH), F32), pltpu.VMEM((tl, HG_WIDTH), F32),
                        pltpu.VMEM((tl, HG_WIDTH), F32)],
        compiler_params=_params(*sem), name="hgrn2")(h4, s0_all, lb_logits, norm_g)


def _lambda(lamp_ref):
    lam_init = lamp_ref[4:5, 0:1]
    l1 = jnp.sum(lamp_ref[0:1, :] * lamp_ref[1:2, :], axis=-1, keepdims=True)
    l2 = jnp.sum(lamp_ref[2:3, :] * lamp_ref[3:4, :], axis=-1, keepdims=True)
    return jnp.exp(l1) - jnp.exp(l2) + lam_init, lam_init


def _diff_finish(o0, o1, lam, lam_init, ng):
    o = o0 - lam * o1
    return o * lax.rsqrt(jnp.mean(o * o, axis=-1, keepdims=True) + RMS_EPS) * ng * (1.0 - lam_init)


def _lane_fold(x, op):
    parts = [x[:, c * LANES:(c + 1) * LANES] for c in range(x.shape[1] // LANES)]
    return functools.reduce(op, parts)


def _attn_prompt_kernel(q_ref, k_ref, v_ref, lamp_ref, ng_ref, o_ref, kb_s, vb_s, s_s, mp_s, lp_s,
                        acc_s):
    tq = q_ref.shape[0]
    i = pl.program_id(2)

    @pl.when(i == 0)
    def _():
        kb_s[...] = k_ref[...].astype(BF16)
        vb_s[...] = v_ref[...].astype(BF16)

    lane = lax.broadcasted_iota(jnp.int32, (tq, DA_DV), 1)
    qs = q_ref[...] * (DA_DH ** -0.5)
    q2 = jnp.concatenate([jnp.where(lane < DA_DH, qs, 0.0), jnp.where(lane >= DA_DH, qs, 0.0)],
                         axis=0).astype(BF16)
    mp_s[...] = jnp.full(mp_s.shape, NEG_BIG, F32)

    def scores(j, masked):
        r0 = pl.multiple_of(j * tq, tq)
        s = _dot_nt(q2, kb_s[pl.ds(r0, tq), :])
        if masked:
            row = lax.broadcasted_iota(jnp.int32, (2 * tq, tq), 0) % tq
            col = lax.broadcasted_iota(jnp.int32, (2 * tq, tq), 1)
            s = jnp.where(col <= row, s, NEG_BIG)
        s_s[j] = s
        mp_s[...] = jnp.maximum(mp_s[...], _lane_fold(s, jnp.maximum))

    def pass1(j, c):
        scores(j, False)
        return c

    lax.fori_loop(0, i, pass1, 0)
    scores(i, True)
    m = jnp.max(mp_s[...], axis=-1, keepdims=True)
    lp_s[...] = jnp.zeros(lp_s.shape, F32)
    acc_s[...] = jnp.zeros(acc_s.shape, F32)

    def pass2(j, c):
        r0 = pl.multiple_of(j * tq, tq)
        p = jnp.exp(s_s[j] - m)
        lp_s[...] += _lane_fold(p, lambda a, b: a + b)
        acc_s[...] += _dot(p.astype(BF16), vb_s[pl.ds(r0, tq), :])
        return c

    lax.fori_loop(0, i + 1, pass2, 0)
    l = jnp.sum(lp_s[...], axis=-1, keepdims=True)
    lam, lam_init = _lambda(lamp_ref)
    o0 = acc_s[0:tq, :] / l[0:tq, :]
    o1 = acc_s[tq:2 * tq, :] / l[tq:2 * tq, :]
    o_ref[...] = _diff_finish(o0, o1, lam, lam_init, ng_ref[...])


def _attn_prompt(q, k_all, v_all, lamp, norm_g, layer, nb, seq):
    n = q.shape[0]
    tq = _tile(seq, 256)
    nq = seq // tq
    return pl.pallas_call(
        _attn_prompt_kernel, out_shape=jax.ShapeDtypeStruct((n, DA_WIDTH), F32),
        grid=(nb, DA_HEADS, nq),
        in_specs=[pl.BlockSpec((tq, DA_DV), lambda b, h, i: (b * nq + i, h)),
                  pl.BlockSpec((None, seq, DA_DV), lambda b, h, i: (layer, b, h)),
                  pl.BlockSpec((None, seq, DA_DV), lambda b, h, i: (layer, b, h)),
                  pl.BlockSpec((8, LANES), lambda b, h, i: (0, 0)),
                  pl.BlockSpec((1, DA_DV), lambda b, h, i: (0, 0))],
        out_specs=pl.BlockSpec((tq, DA_DV), lambda b, h, i: (b * nq + i, h)),
        scratch_shapes=[pltpu.VMEM((seq, DA_DV), BF16), pltpu.VMEM((seq, DA_DV), BF16),
                        pltpu.VMEM((nq, 2 * tq, tq), F32), pltpu.VMEM((2 * tq, LANES), F32),
                        pltpu.VMEM((2 * tq, LANES), F32), pltpu.VMEM((2 * tq, DA_DV), F32)],
        compiler_params=_params("parallel", "parallel", "arbitrary"),
        name="attn_prompt")(q, k_all, v_all, lamp, norm_g)


def _attn_sample_kernel(pt_ref, q_ref, *refs, n_pages):
    kp = refs[:n_pages]
    vp = refs[n_pages:2 * n_pages]
    kn_ref, vn_ref, lamp_ref, ng_ref, o_ref = refs[2 * n_pages:]
    t = q_ref.shape[0]
    nrow = 2 * DA_HEADS * t
    hrow = 2 * t

    qs = q_ref[...] * (DA_DH ** -0.5)
    qt = jnp.concatenate([qs] * (2 * DA_HEADS), axis=0)
    r = lax.broadcasted_iota(jnp.int32, (nrow, DA_QK), 0) // t
    c = lax.broadcasted_iota(jnp.int32, (nrow, DA_QK), 1) // DA_DH
    qbd = jnp.where(r == c, qt, 0.0).astype(BF16)

    s_pages = [_dot(qbd, kp[i][...].astype(BF16)) for i in range(n_pages)]
    kb = _pad_rows(kn_ref[...], PAGE_SIZE).astype(BF16)
    vb = _pad_rows(vn_ref[...], PAGE_SIZE).astype(BF16)
    r = lax.broadcasted_iota(jnp.int32, (nrow, PAGE_SIZE), 0) % t
    c = lax.broadcasted_iota(jnp.int32, (nrow, PAGE_SIZE), 1)
    s_new = jnp.where(c <= r, _dot_nt(qbd, kb), NEG_BIG)
    s_all = s_pages + [s_new]
    m = jnp.max(functools.reduce(jnp.maximum, s_all), axis=-1, keepdims=True)
    p_all = [jnp.exp(s - m) for s in s_all]
    l = jnp.sum(functools.reduce(lambda a, b: a + b, p_all), axis=-1, keepdims=True)
    pb = [p.astype(BF16) for p in p_all]

    pc = PAGE_SIZE * DA_HEADS
    spread = jnp.where(lax.broadcasted_iota(jnp.int32, (PAGE_SIZE, pc), 1) // DA_HEADS
                       == lax.broadcasted_iota(jnp.int32, (PAGE_SIZE, pc), 0), 1.0, 0.0).astype(BF16)
    own = (lax.broadcasted_iota(jnp.int32, (nrow, pc), 1) % DA_HEADS
           == lax.broadcasted_iota(jnp.int32, (nrow, pc), 0) // hrow)
    acc = jnp.concatenate([_dot(pb[n_pages][h * hrow:(h + 1) * hrow, :], vb[:, h * DA_DV:(h + 1) * DA_DV])
                           for h in range(DA_HEADS)], axis=0)
    for i in range(n_pages):
        pe = jnp.where(own, _dot(pb[i], spread), 0.0).astype(BF16)
        acc = acc + _dot(pe, vp[i][...].astype(BF16))
    o = acc / l
    lam, lam_init = _lambda(lamp_ref)
    ng = ng_ref[...]
    for h in range(DA_HEADS):
        r0 = h * hrow
        o_ref[:, h * DA_DV:(h + 1) * DA_DV] = _diff_finish(o[r0:r0 + t, :], o[r0 + t:r0 + 2 * t, :],
                                                           lam, lam_init, ng)


def _attn_sample(q, k_new, v_new, cache_kt, cache_v, page_table, lamp, norm_g, layer, nb, seq):
    n = q.shape[0]
    n_pages = page_table.shape[1]
    row_map = lambda b, pt: (b, 0)
    new_map = lambda b, pt: (layer, b, 0)
    k_specs = [pl.BlockSpec((None, None, DA_QK, PAGE_SIZE),
                            lambda b, pt, i=i: (layer, pt[b * n_pages + i], 0, 0))
               for i in range(n_pages)]
    v_specs = [pl.BlockSpec((None, None, PAGE_SIZE * DA_HEADS, DA_DV),
                            lambda b, pt, i=i: (layer, pt[b * n_pages + i], 0, 0))
               for i in range(n_pages)]
    grid_spec = pltpu.PrefetchScalarGridSpec(
        num_scalar_prefetch=1, grid=(nb,),
        in_specs=[pl.BlockSpec((seq, DA_QK), row_map)] + k_specs + v_specs
        + [pl.BlockSpec((None, seq, DA_QK), new_map), pl.BlockSpec((None, seq, DA_WIDTH), new_map),
           pl.BlockSpec((8, LANES), lambda b, pt: (0, 0)),
           pl.BlockSpec((1, DA_DV), lambda b, pt: (0, 0))],
        out_specs=pl.BlockSpec((seq, DA_WIDTH), row_map))
    return pl.pallas_call(
        functools.partial(_attn_sample_kernel, n_pages=n_pages),
        out_shape=jax.ShapeDtypeStruct((n, DA_WIDTH), F32),
        grid_spec=grid_spec, compiler_params=_params("parallel"),
        name="attn_sample")(page_table.reshape(-1), q, *([cache_kt] * n_pages),
                            *([cache_v] * n_pages), k_new, v_new, lamp, norm_g)


def _even_out_kernel(oh_ref, od_ref, x_ref, w_ref, g_ref, b_ref, o_ref):
    h = (_dot(oh_ref[...].astype(BF16), w_ref[0:HG_WIDTH, :])
         + _dot(od_ref[...].astype(BF16), w_ref[HG_WIDTH:D_MODEL, :]))
    o_ref[...] = _layer_norm(ALPHA * x_ref[...] + h, g_ref[...], b_ref[...])


def _even_out(oh, od, x, w_out_bf, g, b):
    n = x.shape[0]
    tm = _tile(n, 512)
    vec = pl.BlockSpec((1, D_MODEL), lambda i: (0, 0))
    return pl.pallas_call(
        _even_out_kernel, out_shape=jax.ShapeDtypeStruct((n, D_MODEL), F32), grid=(n // tm,),
        in_specs=[_row_spec(tm, HG_WIDTH), _row_spec(tm, DA_WIDTH), _row_spec(tm, D_MODEL),
                  _const_spec((D_MODEL, D_MODEL)), vec, vec],
        out_specs=_row_spec(tm, D_MODEL),
        compiler_params=_params("parallel"), name="even_out")(oh, od, x, w_out_bf, g, b)


FF_CHUNK = 1024


def _mlp_kernel(x_ref, p_ref, up_ref, down_ref, g_ref, b_ref, gate_ref, proj_ref, o_ref):
    x = x_ref[...]
    xb = x.astype(BF16)
    acc = jnp.zeros(x.shape, F32)
    for c in range(D_FF // FF_CHUNK):
        cs = slice(c * FF_CHUNK, (c + 1) * FF_CHUNK)
        hid = jnp.maximum(_dot(xb, up_ref[:, cs]), 0.0)
        acc = acc + _dot((hid * hid).astype(BF16), down_ref[cs, :])
    x2 = _layer_norm(ALPHA * x + acc, g_ref[...], b_ref[...])
    gate = _sigmoid(_dot(x2.astype(BF16), gate_ref[...]))
    o_ref[...] = x2 + gate * _dot(p_ref[...].astype(BF16), proj_ref[...])


def _mlp(x, p_all, layer, up_bf, down_bf, g, b, gate_bf, proj_bf):
    n = x.shape[0]
    tm = _tile(n, 512)
    vec = pl.BlockSpec((1, D_MODEL), lambda i: (0, 0))
    return pl.pallas_call(
        _mlp_kernel, out_shape=jax.ShapeDtypeStruct((n, D_MODEL), F32), grid=(n // tm,),
        in_specs=[_row_spec(tm, D_MODEL), pl.BlockSpec((None, tm, PLE_DIM), lambda i: (layer, i, 0)),
                  _const_spec((D_MODEL, D_FF)), _const_spec((D_FF, D_MODEL)), vec, vec,
                  _const_spec((D_MODEL, D_MODEL)), _const_spec((PLE_DIM, D_MODEL))],
        out_specs=_row_spec(tm, D_MODEL),
        compiler_params=_params("parallel"), name="mlp")(x, p_all, up_bf, down_bf, g, b, gate_bf,
                                                         proj_bf)


def _swap_index_with_lane_pos(zs, width):
    pos = lax.broadcasted_iota(jnp.int32, zs[0].shape, 1) % width
    s = width // 2
    while s >= 1:
        hi = (pos & s) != 0
        new = list(zs)
        for p in range(width):
            if p & s == 0:
                lo_arr, hi_arr = zs[p], zs[p + s]
                new[p] = jnp.where(hi, pltpu.roll(hi_arr, s, 1), lo_arr)
                new[p + s] = jnp.where(hi, hi_arr, pltpu.roll(lo_arr, LANES - s, 1))
        zs = new
        s //= 2
    return zs


def _tokens_to_chains(x, nseq, tt, out_ref):
    sub = LANES // nseq
    for sb in range(tt // sub):
        heads = []
        for g in range(D_MODEL // LANES):
            tile = jnp.concatenate(
                [x[b * tt + sb * sub:b * tt + (sb + 1) * sub, g * LANES:(g + 1) * LANES]
                 for b in range(nseq)], axis=0)
            tile_t = tile.T
            heads += [tile_t[0:RW_HEAD, :], tile_t[RW_HEAD:2 * RW_HEAD, :]]
        for t, w in enumerate(_swap_index_with_lane_pos(heads, sub)):
            out_ref[sb * sub + t] = w


def _rwkv_pre_kernel(*refs, has_vfirst, chains, want_v_tokens):
    if chains:
        x_ref, x8_ref, sh_ref = refs[:3]
        refs = refs[3:]
    else:
        x_ref, xp_ref = refs[:2]
        refs = refs[2:]
    vf_ref = vt_o = v0_ref = v1_ref = v2_ref = None
    if has_vfirst:
        vf_ref, refs = refs[0], refs[1:]
    (mix_ref, wr_ref, wk_ref, wv_ref, w0_ref, w1_ref, w2_ref, a0_ref, a1_ref, a2_ref) = refs[:10]
    refs = refs[10:]
    if has_vfirst:
        v0_ref, v1_ref, v2_ref = refs[:3]
        refs = refs[3:]
    g1_ref, g2_ref, r_o, k_o, v_o, g_o, dec_o, a_o = refs[:8]
    if want_v_tokens:
        vt_o = refs[8]
    if chains:
        nseq, tt = x_ref.shape[0], x_ref.shape[1]
        x = x_ref[...].reshape(nseq * tt, D_MODEL)
        first = pl.program_id(0) == 0
        before = jnp.where(first, sh_ref[...], x8_ref[:, SUBLANES - 1, :])
        before = jnp.concatenate([jnp.broadcast_to(before[b:b + 1, :], (tt, D_MODEL))
                                  for b in range(nseq)], axis=0)
        rowid = lax.broadcasted_iota(jnp.int32, x.shape, 0)
        x_prev = jnp.where(rowid % tt == 0, before, pltpu.roll(x, 1, 0))
    else:
        x = x_ref[...]
        x_prev = xp_ref[...]
    xx = x_prev - x
    xr, xw, xk, xv, xa, xg = ((x + xx * mix_ref[m:m + 1, :]).astype(BF16) for m in range(6))
    r = _dot(xr, wr_ref[...])
    k = _dot(xk, wk_ref[...])
    v = _dot(xv, wv_ref[...])
    lw = w0_ref[...] + _dot(jnp.tanh(_dot(xw, w1_ref[...])).astype(BF16), w2_ref[...])
    w = -(jnp.maximum(-lw, 0.0) + jnp.log(1.0 + jnp.exp(-jnp.abs(lw)))) - 0.5
    if has_vfirst:
        mixv = _sigmoid(v0_ref[...] + _dot(_dot(xv, v1_ref[...]).astype(BF16), v2_ref[...]))
        vf = vf_ref[...].reshape(v.shape) if chains else vf_ref[...]
        v = v + (vf - v) * mixv
    a = _sigmoid(a0_ref[...] + _dot(_dot(xa, a1_ref[...]).astype(BF16), a2_ref[...]))
    g = _dot(_sigmoid(_dot(xg, g1_ref[...])).astype(BF16), g2_ref[...])
    dec = jnp.exp(-jnp.exp(w))
    if chains:
        g_o[...] = g.reshape(g_o.shape)
        if want_v_tokens:
            vt_o[...] = v.reshape(vt_o.shape)
        for val, ref in ((r, r_o), (k, k_o), (v, v_o), (dec, dec_o), (a, a_o)):
            _tokens_to_chains(val, nseq, tt, ref)
    else:
        r_o[...] = r
        k_o[...] = k
        v_o[...] = v
        g_o[...] = g
        dec_o[...] = dec
        a_o[...] = a
        if want_v_tokens:
            vt_o[...] = v


def _chain_layout_ok(nb, seq):
    return nb * RW_HEADS == LANES and LANES // nb == RW_HEADS and seq % RW_TBLK == 0


def _rwkv_pre(x, shift0, v_first, wts, nb, seq, want_v_tokens):
    n = x.shape[0]
    has_vfirst = v_first is not None
    chains = _chain_layout_ok(nb, seq)
    vec = pl.BlockSpec((1, D_MODEL), lambda i: (0, 0))
    if chains:
        tt = RW_TBLK
        tok3 = pl.BlockSpec((nb, tt, D_MODEL), lambda i: (0, i, 0))
        x3 = x.reshape(nb, seq, D_MODEL)
        r8 = tt // SUBLANES
        acts = [x3, x3, shift0]
        act_specs = [tok3,
                     pl.BlockSpec((nb, SUBLANES, D_MODEL), lambda i: (0, jnp.maximum(i * r8 - 1, 0), 0)),
                     pl.BlockSpec((nb, D_MODEL), lambda i: (0, 0))]
        if has_vfirst:
            acts.append(v_first.reshape(nb, seq, D_MODEL))
            act_specs.append(tok3)
        grid = (seq // tt,)
        chain_out = jax.ShapeDtypeStruct((seq, RW_HEAD, LANES), F32)
        chain_spec = pl.BlockSpec((tt, RW_HEAD, LANES), lambda i: (i, 0, 0))
        tok_out = jax.ShapeDtypeStruct((nb, seq, D_MODEL), F32)
        out_shape = [chain_out, chain_out, chain_out, tok_out, chain_out, chain_out]
        out_specs = [chain_spec, chain_spec, chain_spec, tok3, chain_spec, chain_spec]
        if want_v_tokens:
            out_shape.append(tok_out)
            out_specs.append(tok3)
    else:
        tm = _tile(n, 256)
        xs = x.reshape(nb, seq, D_MODEL)
        x_prev = jnp.concatenate([shift0[:, None, :], xs[:, :-1]], axis=1).reshape(n, D_MODEL)
        acts = [x, x_prev] + ([v_first] if has_vfirst else [])
        act_specs = [_row_spec(tm, D_MODEL)] * len(acts)
        grid = (n // tm,)
        out_shape = [jax.ShapeDtypeStruct((n, D_MODEL), F32)] * 6
        out_specs = [_row_spec(tm, D_MODEL)] * 6
    names = ['mix', 'w_r', 'w_k', 'w_v', 'w0', 'w1', 'w2', 'a0', 'a1', 'a2']
    if has_vfirst:
        names += ['v0', 'v1', 'v2']
    names += ['g1', 'g2']
    ws = [wts[nm] for nm in names]
    w_specs = [vec if w.shape == (1, D_MODEL) else _const_spec(w.shape) for w in ws]
    outs = pl.pallas_call(
        functools.partial(_rwkv_pre_kernel, has_vfirst=has_vfirst, chains=chains,
                          want_v_tokens=want_v_tokens and chains),
        out_shape=tuple(out_shape), grid=grid,
        in_specs=act_specs + w_specs, out_specs=tuple(out_specs),
        compiler_params=_params("parallel"), name="rwkv_pre")(*acts, *ws)
    outs = list(outs)
    if chains:
        outs[3] = outs[3].reshape(n, D_MODEL)
        if want_v_tokens:
            outs[6] = outs[6].reshape(n, D_MODEL)
    elif want_v_tokens:
        outs.append(outs[2])
    return outs


RW_TBLK = 32


RW_IN = 5


def _rwkv_token(ins, v_rows, prm, s_s, y_s):
    r, k, v, dec, ag = ins
    kk_p, ka_p, rk_p, lg_p, lb_p = prm
    kk = k * kk_p
    kk = kk / jnp.maximum(jnp.sqrt(jnp.sum(kk * kk, axis=0, keepdims=True)), 1e-12)
    a = -kk
    b = kk * ag
    k2 = k * (1.0 + (ag - 1.0) * ka_p)
    wr = dec * r
    br = jnp.sum(b * r, axis=0, keepdims=True)
    kr = jnp.sum(k2 * r, axis=0, keepdims=True)

    def rows(i8, _):
        i0 = pl.multiple_of(i8 * SUBLANES, SUBLANES)
        v8 = v_rows(i0)
        ys = []
        for ii in range(SUBLANES):
            si = s_s[i0 + ii]
            sa = jnp.sum(si * a, axis=0, keepdims=True)
            qs = jnp.sum(si * wr, axis=0, keepdims=True)
            vi = v8[ii:ii + 1, :]
            s_s[i0 + ii] = si * dec + sa * b + vi * k2
            ys.append(qs + sa * br + vi * kr)
        y_s[pl.ds(i0, SUBLANES), :] = jnp.concatenate(ys, axis=0)
        return 0

    lax.fori_loop(0, RW_HEAD // SUBLANES, rows, 0)
    y = y_s[...]
    mu = jnp.mean(y, axis=0, keepdims=True)
    yc = y - mu
    var = jnp.mean(yc * yc, axis=0, keepdims=True)
    yn = yc * lax.rsqrt(var + GN_EPS) * lg_p + lb_p
    return yn + jnp.sum(r * k2 * rk_p, axis=0, keepdims=True) * v


def _rwkv_scan_kernel(r_ref, k_ref, v_ref, dec_ref, a_ref, kk_ref, ka_ref, rk_ref, lg_ref, lb_ref,
                      s0_ref, y_ref, sout_ref, s_s, y_s):
    tb = pl.program_id(1)
    in_refs = (r_ref, k_ref, v_ref, dec_ref, a_ref)
    prm_refs = (kk_ref, ka_ref, rk_ref, lg_ref, lb_ref)

    @pl.when(tb == 0)
    def _():
        s_s[...] = s0_ref[...]

    def step(t, _):
        y_ref[t] = _rwkv_token([ref[t] for ref in in_refs],
                               lambda i0: v_ref[t, pl.ds(i0, SUBLANES), :],
                               [p[...] for p in prm_refs], s_s, y_s)
        return 0

    lax.fori_loop(0, r_ref.shape[0], step, 0)

    @pl.when(tb == pl.num_programs(1) - 1)
    def _():
        sout_ref[...] = s_s[...]


def _rwkv_scan_tokens_kernel(r_ref, k_ref, v_ref, dec_ref, a_ref, kk_ref, ka_ref, rk_ref, lg_ref,
                             lb_ref, s0_ref, y_ref, sout_ref, s_s, y_s, yt_s):
    tb = pl.program_id(1)
    nseq, tt = y_ref.shape[0], y_ref.shape[1]
    sub = LANES // nseq
    in_refs = (r_ref, k_ref, v_ref, dec_ref, a_ref)
    prm_refs = (kk_ref, ka_ref, rk_ref, lg_ref, lb_ref)

    @pl.when(tb == 0)
    def _():
        s_s[...] = s0_ref[...]

    def sub_block(sb, _):
        t0 = pl.multiple_of(sb * sub, sub)

        def step(t, _):
            yt_s[t] = _rwkv_token([ref[t0 + t] for ref in in_refs],
                                  lambda i0: v_ref[t0 + t, pl.ds(i0, SUBLANES), :],
                                  [p[...] for p in prm_refs], s_s, y_s)
            return 0

        lax.fori_loop(0, sub, step, 0)
        heads = _swap_index_with_lane_pos([yt_s[t] for t in range(sub)], sub)
        for g in range(D_MODEL // LANES):
            tile = jnp.concatenate([heads[2 * g], heads[2 * g + 1]], axis=0).T
            for b in range(nseq):
                y_ref[b, pl.ds(t0, sub), g * LANES:(g + 1) * LANES] = tile[b * sub:(b + 1) * sub, :]
        return 0

    lax.fori_loop(0, tt // sub, sub_block, 0)

    @pl.when(tb == pl.num_programs(1) - 1)
    def _():
        sout_ref[...] = s_s[...]


def _rwkv_scan(r, k, v, dec, a, s0_all, s_layer, wts, nb, seq):
    nl = s0_all.shape[0]
    n = nb * seq
    hd = RW_HEAD
    batch_lanes = nb % LANES == 0
    if batch_lanes:
        nbg = nb // LANES
        groups, chains = RW_HEADS * nbg, LANES
        to_l = lambda x: (x.reshape(nbg, LANES, seq, RW_HEADS, hd).transpose(0, 3, 2, 4, 1)
                          .reshape(groups, seq, hd, chains))
        from_l = lambda y: (y.reshape(nbg, RW_HEADS, seq, hd, LANES).transpose(0, 4, 2, 1, 3)
                            .reshape(n, D_MODEL))
        par = lambda p: jnp.broadcast_to(p.reshape(1, RW_HEADS, hd, 1),
                                         (nbg, RW_HEADS, hd, chains)).reshape(groups, hd, chains)
        st_in = lambda s: (s.reshape(nl, nbg, LANES, RW_HEADS, hd, hd).transpose(0, 1, 3, 4, 5, 2)
                           .reshape(nl, groups, hd, hd, chains))
        st_out = lambda s: (s.reshape(nbg, RW_HEADS, hd, hd, LANES).transpose(0, 4, 1, 2, 3)
                            .reshape(nb, RW_HEADS, hd, hd))
    else:
        groups, chains = 1, nb * RW_HEADS
        to_l = lambda x: (x.reshape(nb, seq, RW_HEADS, hd).transpose(1, 3, 0, 2)
                          .reshape(1, seq, hd, chains))
        from_l = lambda y: (y.reshape(seq, hd, nb, RW_HEADS).transpose(2, 0, 3, 1)
                            .reshape(n, D_MODEL))
        par = lambda p: jnp.broadcast_to(p.reshape(RW_HEADS, hd).T.reshape(1, hd, 1, RW_HEADS),
                                         (1, hd, nb, RW_HEADS)).reshape(1, hd, chains)
        st_in = lambda s: s.transpose(0, 3, 4, 1, 2).reshape(nl, 1, hd, hd, chains)
        st_out = lambda s: s.reshape(hd, hd, nb, RW_HEADS).transpose(2, 3, 0, 1)
    tt = _tile(seq, RW_TBLK)
    prm = pl.BlockSpec((None, hd, chains), lambda g, t: (g, 0, 0))
    st = pl.BlockSpec((None, hd, hd, chains), lambda g, t: (g, 0, 0, 0))
    st0 = pl.BlockSpec((None, None, hd, hd, chains), lambda g, t: (s_layer, g, 0, 0, 0))
    prms = [par(wts[nm]) for nm in ('k_k', 'k_a', 'r_k', 'lnx_g', 'lnx_b')]
    if _chain_layout_ok(nb, seq):
        tok = pl.BlockSpec((nb, tt, D_MODEL), lambda g, t: (0, t, 0))
        act = pl.BlockSpec((tt, hd, chains), lambda g, t: (t, 0, 0))
        y, s_new = pl.pallas_call(
            _rwkv_scan_tokens_kernel,
            out_shape=(jax.ShapeDtypeStruct((nb, seq, D_MODEL), F32),
                       jax.ShapeDtypeStruct((1, hd, hd, chains), F32)),
            grid=(1, seq // tt),
            in_specs=[act] * RW_IN + [prm] * 5 + [st0],
            out_specs=(tok, st),
            scratch_shapes=[pltpu.VMEM((hd, hd, chains), F32), pltpu.VMEM((hd, chains), F32),
                            pltpu.VMEM((RW_HEADS, hd, chains), F32)],
            compiler_params=_params("parallel", "arbitrary"),
            name="rwkv_scan_tokens")(r, k, v, dec, a, *prms, st_in(s0_all))
        return y.reshape(n, D_MODEL), st_out(s_new)
    act = pl.BlockSpec((None, tt, hd, chains), lambda g, t: (g, t, 0, 0))
    y, s_new = pl.pallas_call(
        _rwkv_scan_kernel,
        out_shape=(jax.ShapeDtypeStruct((groups, seq, hd, chains), F32),
                   jax.ShapeDtypeStruct((groups, hd, hd, chains), F32)),
        grid=(groups, seq // tt),
        in_specs=[act] * RW_IN + [prm] * 5 + [st0],
        out_specs=(act, st),
        scratch_shapes=[pltpu.VMEM((hd, hd, chains), F32), pltpu.VMEM((hd, chains), F32)],
        compiler_params=_params("parallel", "arbitrary"),
        name="rwkv_scan")(to_l(r), to_l(k), to_l(v), to_l(dec), to_l(a), *prms, st_in(s0_all))
    return from_l(y), st_out(s_new)


def _rwkv_post_kernel(y_ref, g_ref, x_ref, wo_ref, n1g_ref, n1b_ref, o_ref):
    h = _dot((y_ref[...] * g_ref[...]).astype(BF16), wo_ref[...])
    o_ref[...] = _layer_norm(ALPHA * x_ref[...] + h, n1g_ref[...], n1b_ref[...])


def _rwkv_post(y, g, x, wo_bf, n1g, n1b):
    n = x.shape[0]
    tm = _tile(n, 512)
    vec = pl.BlockSpec((1, D_MODEL), lambda i: (0, 0))
    return pl.pallas_call(
        _rwkv_post_kernel, out_shape=jax.ShapeDtypeStruct((n, D_MODEL), F32), grid=(n // tm,),
        in_specs=[_row_spec(tm, D_MODEL)] * 3 + [_const_spec((D_MODEL, D_MODEL)), vec, vec],
        out_specs=_row_spec(tm, D_MODEL),
        compiler_params=_params("parallel"), name="rwkv_post")(y, g, x, wo_bf, n1g, n1b)


def _trunk(x3, p4, hg_s0, wkv_s0, shift0, cache_k, cache_v, page_table, W):
    nb, seq, _ = x3.shape
    n = nb * seq
    x = x3.reshape(n, D_MODEL)
    new_hg, new_wkv, new_shift = [], [], []
    kv = None
    v_first = None
    for i in range(DEPTH):
        j = i // 2
        if i % 2 == 0:
            h4, q, k_all, v_all = _even_in(x, W['w_in_even'][j], j, kv)
            kv = (k_all, v_all)
            oh, s_hg = _hgrn(h4, hg_s0, min(j, hg_s0.shape[0] - 1), W['hg_lb_logits'],
                             W['hg_norm_g'][j], j, nb, seq)
            if page_table is None:
                od = _attn_prompt(q, k_all, v_all, W['lamp'][j], W['da_norm_g'][j], j, nb, seq)
            else:
                od = _attn_sample(q, k_all, v_all, cache_k, cache_v, page_table, W['lamp'][j],
                                  W['da_norm_g'][j], j, nb, seq)
            x = _even_out(oh, od, x, W['w_out_even'][j], W['ln1_g'][i], W['ln1_b'][i])
            new_hg.append(s_hg)
        else:
            need_vf = j == 0 and N_ODD > 1
            outs = _rwkv_pre(x, shift0[j], v_first, W['rw'][j], nb, seq, need_vf)
            r, k, v, g, dec, a = outs[:6]
            if need_vf:
                v_first = outs[6]
            y, s_wkv = _rwkv_scan(r, k, v, dec, a, wkv_s0, min(j, wkv_s0.shape[0] - 1),
                                  W['rw_scan'][j], nb, seq)
            new_shift.append(x.reshape(nb, seq, D_MODEL)[:, -1])
            new_wkv.append(s_wkv)
            x = _rwkv_post(y, g, x, W['rw_w_o'][j], W['ln1_g'][i], W['ln1_b'][i])
        x = _mlp(x, p4.reshape(DEPTH, n, PLE_DIM), i, W['mlp_up'][i], W['mlp_down'][i],
                 W['ln2_g'][i], W['ln2_b'][i], W['ple_gate'][i], W['ple_proj'][i])
    return (x.reshape(nb, seq, D_MODEL), kv[0].reshape(N_EVEN, nb, seq, DA_HEADS, 2, DA_DH),
            kv[1].reshape(N_EVEN, nb, seq, DA_HEADS, DA_DV), jnp.stack(new_hg),
            jnp.stack(new_wkv), jnp.stack(new_shift))


def kernel(x_prompt, x_sample, cache_k, cache_v, state_hgrn, state_wkv, state_shift, page_table,
           p_prompt, p_sample, w_in_even, w_out_even, hg_lb_logits, hg_norm_g,
           da_lam_q1, da_lam_k1, da_lam_q2, da_lam_k2, da_norm_g,
           rw_mix, rw_w_r, rw_w_k, rw_w_v, rw_w_o, rw_w0, rw_w1, rw_w2,
           rw_a0, rw_a1, rw_a2, rw_v0, rw_v1, rw_v2, rw_g1, rw_g2,
           rw_k_k, rw_k_a, rw_r_k, rw_lnx_g, rw_lnx_b,
           ln1_g, ln1_b, ln2_g, ln2_b, mlp_up, mlp_down, ple_proj, ple_gate):
    bf = lambda w: w.astype(BF16)
    vec = lambda w: w.reshape(1, -1)

    def lam_rows(j):
        pad = lambda a: jnp.pad(a.astype(F32), (0, LANES - a.shape[0]))
        lam_init = 0.8 - 0.6 * math.exp(-0.3 * (2 * j))
        rows = [pad(da_lam_q1[j]), pad(da_lam_k1[j]), pad(da_lam_q2[j]), pad(da_lam_k2[j]),
                jnp.full((LANES,), lam_init, F32)]
        return jnp.stack(rows + [jnp.zeros((LANES,), F32)] * 3)

    rw = []
    for j in range(N_ODD):
        d = dict(mix=rw_mix[j], w_r=bf(rw_w_r[j]), w_k=bf(rw_w_k[j]), w_v=bf(rw_w_v[j]),
                 w0=vec(rw_w0[j]), w1=bf(rw_w1[j]), w2=bf(rw_w2[j]),
                 a0=vec(rw_a0[j]), a1=bf(rw_a1[j]), a2=bf(rw_a2[j]),
                 g1=bf(rw_g1[j]), g2=bf(rw_g2[j]))
        if j > 0:
            d.update(v0=vec(rw_v0[j - 1]), v1=bf(rw_v1[j - 1]), v2=bf(rw_v2[j - 1]))
        rw.append(d)
    W = dict(
        w_in_even=[bf(w_in_even[j]) for j in range(N_EVEN)],
        w_out_even=[bf(w_out_even[j]) for j in range(N_EVEN)],
        hg_lb_logits=hg_lb_logits,
        hg_norm_g=[vec(hg_norm_g[j]) for j in range(N_EVEN)],
        da_norm_g=[vec(da_norm_g[j]) for j in range(N_EVEN)],
        lamp=[lam_rows(j) for j in range(N_EVEN)],
        rw=rw,
        rw_scan=[dict(k_k=rw_k_k[j], k_a=rw_k_a[j], r_k=rw_r_k[j], lnx_g=rw_lnx_g[j],
                      lnx_b=rw_lnx_b[j]) for j in range(N_ODD)],
        rw_w_o=[bf(rw_w_o[j]) for j in range(N_ODD)],
        ln1_g=[vec(ln1_g[i]) for i in range(DEPTH)], ln1_b=[vec(ln1_b[i]) for i in range(DEPTH)],
        ln2_g=[vec(ln2_g[i]) for i in range(DEPTH)], ln2_b=[vec(ln2_b[i]) for i in range(DEPTH)],
        mlp_up=[bf(mlp_up[i]) for i in range(DEPTH)], mlp_down=[bf(mlp_down[i]) for i in range(DEPTH)],
        ple_gate=[bf(ple_gate[i]) for i in range(DEPTH)], ple_proj=[bf(ple_proj[i]) for i in range(DEPTH)],
    )
    nbp = x_prompt.shape[0]
    dt = x_prompt.dtype
    hg0 = jnp.zeros((1, nbp, HG_HEADS, HG_DK, HG_DK), dt)
    wkv0 = jnp.zeros((1, nbp, RW_HEADS, RW_HEAD, RW_HEAD), dt)
    sh0 = jnp.zeros((N_ODD, nbp, D_MODEL), dt)
    y_p, k_p, v_p, hg_p, wkv_p, sh_p = _trunk(x_prompt, p_prompt, hg0, wkv0, sh0, None, None, None, W)
    ckt = jnp.transpose(cache_k, (0, 1, 3, 4, 5, 2)).reshape(
        cache_k.shape[0], cache_k.shape[1], DA_QK, PAGE_SIZE)
    cv = cache_v.reshape(cache_v.shape[0], cache_v.shape[1], PAGE_SIZE * DA_HEADS, DA_DV)
    y_s, k_s, v_s, hg_s, wkv_s, sh_s = _trunk(x_sample, p_sample, state_hgrn, state_wkv, state_shift,
                                              ckt, cv, page_table, W)
    return (y_p, y_s, k_p, v_p, k_s, v_s, hg_p, hg_s, wkv_p, wkv_s, sh_p, sh_s)
```

```python
import functools
import math

import jax
import jax.numpy as jnp
from jax import lax
from jax.experimental import pallas as pl
from jax.experimental.pallas import tpu as pltpu

F32 = jnp.float32
BF16 = jnp.bfloat16

D_MODEL = 1024
DEPTH = 4
PAGE_SIZE = 128
N_EVEN = (DEPTH + 1) // 2
N_ODD = DEPTH // 2
PLE_DIM = 256
D_FF = 4 * D_MODEL
HG_WIDTH = D_MODEL // 2
HG_HEADS = 4
HG_DK = HG_WIDTH // HG_HEADS
DA_WIDTH = D_MODEL - HG_WIDTH
DA_HEADS = 4
DA_DV = DA_WIDTH // DA_HEADS
DA_DH = DA_DV // 2
DA_QK = DA_HEADS * 2 * DA_DH
IN_COLS = 4 * HG_WIDTH + 2 * DA_QK + DA_WIDTH
RW_HEAD = 64
RW_HEADS = D_MODEL // RW_HEAD
ALPHA = (2.0 * DEPTH) ** 0.25
LN_EPS = 1e-5
RMS_EPS = 1e-6
GN_EPS = 64e-5
F_MIN = 1e-30
NEG_BIG = -1e30

LANES = 128
SUBLANES = 8
VMEM_LIMIT = 52 * 1024 * 1024
HG_CHUNK = 16
MM_ROWS = 16


def _params(*sem):
    return pltpu.CompilerParams(dimension_semantics=sem, vmem_limit_bytes=VMEM_LIMIT)


def _const_spec(shape):
    nd = len(shape)
    return pl.BlockSpec(shape, lambda *_: (0,) * nd, pipeline_mode=pl.Buffered(1))


def _row_spec(rows, cols):
    return pl.BlockSpec((rows, cols), lambda i: (i, 0))


def _tile(n, pref):
    t = min(n, pref)
    while n % t:
        t //= 2
    return t


def _dot(a, b):
    return jnp.dot(a, b, preferred_element_type=F32)


def _dot_nt(a, b):
    return lax.dot_general(a, b, (((1,), (1,)), ((), ())), preferred_element_type=F32)


def _dot_tn(a, b):
    return lax.dot_general(a, b, (((0,), (0,)), ((), ())), preferred_element_type=F32)


def _split3(x):
    h = x.astype(BF16)
    r = x - h.astype(F32)
    m = r.astype(BF16)
    l = (r - m.astype(F32)).astype(BF16)
    return h, m, l


def _dot_x3(x, rhs_bf16):
    h, m, l = _split3(x)
    return _dot(h, rhs_bf16) + _dot(m, rhs_bf16) + _dot(l, rhs_bf16)


def _dot_x3_lhs(lhs_bf16, x):
    h, m, l = _split3(x)
    return _dot(lhs_bf16, h) + _dot(lhs_bf16, m) + _dot(lhs_bf16, l)


def _layer_norm(z, g, b):
    mu = jnp.mean(z, axis=-1, keepdims=True)
    zc = z - mu
    var = jnp.mean(zc * zc, axis=-1, keepdims=True)
    return zc * lax.rsqrt(var + LN_EPS) * g + b


def _sigmoid(x):
    return 1.0 / (1.0 + jnp.exp(-x))


def _head_ones(width):
    r = lax.broadcasted_iota(jnp.int32, (LANES, LANES), 0) // width
    c = lax.broadcasted_iota(jnp.int32, (LANES, LANES), 1) // width
    return jnp.where(r == c, 1.0, 0.0).astype(BF16)


def _segsum(x, ones):
    cols = x.shape[1] // LANES
    parts = [_dot_x3(x[:, c * LANES:(c + 1) * LANES], ones) for c in range(cols)]
    return parts[0] if cols == 1 else jnp.concatenate(parts, axis=1)


def _even_in_kernel(x_ref, w_ref, *refs, k_transposed):
    h4_ref, q_ref, k_ref, v_ref = refs[-4:]
    xb = x_ref[...].astype(BF16)
    cw = HG_WIDTH
    for c in range(4):
        h4_ref[:, c * cw:(c + 1) * cw] = _dot(xb, w_ref[:, c * cw:(c + 1) * cw])
    q0 = 4 * HG_WIDTH
    q_ref[...] = _dot(xb, w_ref[:, q0:q0 + DA_QK])
    if k_transposed:
        k_ref[...] = _dot_nt(refs[0][...], xb)
    else:
        k_ref[...] = _dot(xb, w_ref[:, q0 + DA_QK:q0 + 2 * DA_QK])
    v_ref[...] = _dot(xb, w_ref[:, q0 + 2 * DA_QK:q0 + 2 * DA_QK + DA_WIDTH])


def _even_in(x, w_in_bf, wk_t_bf, layer, kv_stacks, nb, seq):
    n = x.shape[0]
    k_transposed = wk_t_bf is not None
    tm = _tile(seq if k_transposed else n, 512)
    slab =lambda width: pl.BlockSpec((None, tm, width), lambda i: (layer, i, 0))
    in_specs = [_row_spec(tm, D_MODEL), _const_spec((D_MODEL, IN_COLS))]
    args = [x, w_in_bf]
    if k_transposed:
        tps = seq // tm
        k_shape = (N_EVEN, nb, DA_QK, seq)
        k_spec = pl.BlockSpec((None, None, DA_QK, tm), lambda i: (layer, i // tps, 0, i % tps))
        in_specs.append(_const_spec((DA_QK, D_MODEL)))
        args.append(wk_t_bf)
    else:
        k_shape = (N_EVEN, n, DA_QK)
        k_spec = slab(DA_QK)
    outs = (jax.ShapeDtypeStruct((n, 4 * HG_WIDTH), F32), jax.ShapeDtypeStruct((n, DA_QK), F32),
            jax.ShapeDtypeStruct(k_shape, F32), jax.ShapeDtypeStruct((N_EVEN, n, DA_WIDTH), F32))
    aliases = {}
    if kv_stacks is not None:
        aliases = {len(args): 2, len(args) + 1: 3}
        in_specs += [pl.BlockSpec(memory_space=pl.ANY)] * 2
        args += list(kv_stacks)
    return pl.pallas_call(
        functools.partial(_even_in_kernel, k_transposed=k_transposed), out_shape=outs,
        grid=(n // tm,), in_specs=in_specs,
        out_specs=(_row_spec(tm, 4 * HG_WIDTH), _row_spec(tm, DA_QK), k_spec, slab(DA_WIDTH)),
        input_output_aliases=aliases,
        compiler_params=_params("parallel"), name="even_in")(*args)


def _pad_rows(x, rows):
    if x.shape[0] >= rows:
        return x
    return jnp.concatenate([x, jnp.zeros((rows - x.shape[0], x.shape[1]), x.dtype)], axis=0)


def _hgrn_kernel(h4_ref, s0_ref, lbl_ref, ng_ref, o_ref, sout_ref,
                 st_ref, q_s, k_s, b_s, *, layer, chunk, carry):
    tl = h4_ref.shape[0]
    nchunk = tl // chunk
    w = HG_WIDTH

    rows = [lbl_ref[i:i + 1, :] for i in range(N_EVEN)]
    mx = functools.reduce(jnp.maximum, rows)
    es = [jnp.exp(r - mx) for r in rows]
    den = functools.reduce(lambda a, b: a + b, es)
    sm = [e / den for e in es]
    cs = functools.reduce(lambda a, b: a + b, sm[:layer + 1])
    lb = cs - sm[0]

    hq = h4_ref[:, 0:w]
    hf = h4_ref[:, w:2 * w]
    q_s[...] = hq * _sigmoid(hq)
    f = lb + (1.0 - lb) * _sigmoid(hf)
    g = jnp.log(jnp.maximum(f, F_MIN))
    k_s[...] = 1.0 - f
    ri = lax.broadcasted_iota(jnp.int32, (tl, tl), 0)
    ci = lax.broadcasted_iota(jnp.int32, (tl, tl), 1)
    tril = jnp.where((ri // chunk == ci // chunk) & (ci <= ri), 1.0, 0.0).astype(BF16)
    b_s[...] = _dot_x3_lhs(tril, g)

    if carry:
        @pl.when(pl.program_id(1) == 0)
        def _():
            for h in range(HG_HEADS):
                st_ref[h] = s0_ref[0, h].T

    rowid = lax.broadcasted_iota(jnp.int32, (chunk, 1), 0)
    ng = ng_ref[...]

    def do_chunk(c, _):
        r0 = pl.multiple_of(c * chunk, chunk)
        rs = pl.ds(r0, chunk)
        for h in range(HG_HEADS):
            cs_ = slice(h * HG_DK, (h + 1) * HG_DK)
            q = q_s[rs, cs_]
            k = k_s[rs, cs_]
            b = b_s[rs, cs_]
            v = h4_ref[rs, 2 * w + h * HG_DK:2 * w + (h + 1) * HG_DK]
            hg = h4_ref[rs, 3 * w + h * HG_DK:3 * w + (h + 1) * HG_DK]
            if carry:
                st = st_ref[h]
            else:
                st = s0_ref[c, h].T
            qd = _pad_rows((q * jnp.exp(b)).astype(BF16), MM_ROWS)
            o = _dot_nt(qd, st.astype(BF16))[:chunk]
            for s in range(chunk):
                keep = rowid >= s
                e = jnp.where(keep, jnp.exp(jnp.where(keep, b - b[s:s + 1, :], 0.0)), 0.0)
                col = jnp.sum(q * e * k[s:s + 1, :], axis=-1, keepdims=True)
                o = o + col * v[s:s + 1, :]
            b_last = b[chunk - 1:chunk, :]
            kd = _pad_rows((k * jnp.exp(b_last - b)).astype(BF16), MM_ROWS)
            vd = _pad_rows(v.astype(BF16), MM_ROWS)
            st_new = st * jnp.exp(b_last) + _dot_tn(vd, kd)
            if carry:
                st_ref[h] = st_new
            else:
                sout_ref[c, h] = st_new.T
            on = o * lax.rsqrt(jnp.mean(o * o, axis=-1, keepdims=True) + RMS_EPS) * ng
            o_ref[rs, cs_] = on * _sigmoid(hg)
        return 0

    lax.fori_loop(0, nchunk, do_chunk, 0)

    if carry:
        @pl.when(pl.program_id(1) == pl.num_programs(1) - 1)
        def _():
            for h in range(HG_HEADS):
                sout_ref[0, h] = st_ref[h].T


def _hgrn(h4, s0_all, s_layer, lb_logits, norm_g, layer, nb, seq):
    n = h4.shape[0]
    chunk = math.gcd(seq, HG_CHUNK)
    carry = seq > chunk
    if carry:
        tl = _tile(seq, 256)
        grid = (nb, seq // tl)
        row_map = lambda b, l: (b * (seq // tl) + l, 0)
        st_map = lambda b, l: (b, 0, 0, 0)
        st_blk = (1, HG_HEADS, HG_DK, HG_DK)
        sem = ("parallel", "arbitrary")
    else:
        bt = _tile(nb, 8)
        tl = bt * seq
        grid = (nb // bt, 1)
        row_map = lambda b, l: (b, 0)
        st_map = lambda b, l: (b, 0, 0, 0)
        st_blk = (bt, HG_HEADS, HG_DK, HG_DK)
        sem = ("parallel", "arbitrary")
    kern = functools.partial(_hgrn_kernel, layer=layer, chunk=chunk, carry=carry)
    return pl.pallas_call(
        kern,
        out_shape=(jax.ShapeDtypeStruct((n, HG_WIDTH), F32),
                   jax.ShapeDtypeStruct((nb, HG_HEADS, HG_DK, HG_DK), F32)),
        grid=grid,
        in_specs=[pl.BlockSpec((tl, 4 * HG_WIDTH), row_map),
                  pl.BlockSpec((None,) + st_blk, lambda b, l: (s_layer, b, 0, 0, 0)),
                  pl.BlockSpec((N_EVEN, HG_WIDTH), lambda b, l: (0, 0)),
                  pl.BlockSpec((1, HG_DK), lambda b, l: (0, 0))],
        out_specs=(pl.BlockSpec((tl, HG_WIDTH), row_map), pl.BlockSpec(st_blk, st_map)),
        scratch_shapes=[pltpu.VMEM((HG_HEADS, HG_DK, HG_DK), F32),
                        pltpu.VMEM((tl, HG_WIDTH), F32), pltpu.VMEM((tl, HG_WIDTH), F32),
                        pltpu.VMEM((tl, HG_WIDTH), F32)],
        compiler_params=_params(*sem), name="hgrn2")(h4, s0_all, lb_logits, norm_g)


def _lambda(lamp_ref):
    lam_init = lamp_ref[4:5, 0:1]
    l1 = jnp.sum(lamp_ref[0:1, :] * lamp_ref[1:2, :], axis=-1, keepdims=True)
    l2 = jnp.sum(lamp_ref[2:3, :] * lamp_ref[3:4, :], axis=-1, keepdims=True)
    return jnp.exp(l1) - jnp.exp(l2) + lam_init, lam_init


def _diff_finish(o0, o1, lam, lam_init, ng):
    o = o0 - lam * o1
    return o * lax.rsqrt(jnp.mean(o * o, axis=-1, keepdims=True) + RMS_EPS) * ng * (1.0 - lam_init)


def _lane_fold(x, op):
    parts = [x[:, c * LANES:(c + 1) * LANES] for c in range(x.shape[1] // LANES)]
    return functools.reduce(op, parts)


def _attn_prompt_kernel(q_ref, k_ref, v_ref, lamp_ref, ng_ref, o_ref, kb_s, vb_s, s_s, mp_s, lp_s,
                        acc_s):
    tq = q_ref.shape[0]
    i = pl.program_id(2)

    @pl.when(i == 0)
    def _():
        for jb in range(kb_s.shape[0]):
            kb_s[jb] = k_ref[:, jb * tq:(jb + 1) * tq].astype(BF16)
        vb_s[...] = v_ref[...].astype(BF16)

    lane = lax.broadcasted_iota(jnp.int32, (tq, DA_DV), 1)
    qs = q_ref[...] * (DA_DH ** -0.5)
    q2 = jnp.concatenate([jnp.where(lane < DA_DH, qs, 0.0), jnp.where(lane >= DA_DH, qs, 0.0)],
                         axis=0).astype(BF16)
    mp_s[...] = jnp.full(mp_s.shape, NEG_BIG, F32)

    def scores(j, masked):
        s = _dot(q2, kb_s[j])
        if masked:
            row = lax.broadcasted_iota(jnp.int32, (2 * tq, tq), 0) % tq
            col = lax.broadcasted_iota(jnp.int32, (2 * tq, tq), 1)
            s = jnp.where(col <= row, s, NEG_BIG)
        s_s[j] = s
        mp_s[...] = jnp.maximum(mp_s[...], _lane_fold(s, jnp.maximum))

    def pass1(j, c):
        scores(j, False)
        return c

    lax.fori_loop(0, i, pass1, 0)
    scores(i, True)
    m = jnp.max(mp_s[...], axis=-1, keepdims=True)
    lp_s[...] = jnp.zeros(lp_s.shape, F32)
    acc_s[...] = jnp.zeros(acc_s.shape, F32)

    def pass2(j, c):
        r0 = pl.multiple_of(j * tq, tq)
        p = jnp.exp(s_s[j] - m)
        lp_s[...] += _lane_fold(p, lambda a, b: a + b)
        acc_s[...] += _dot(p.astype(BF16), vb_s[pl.ds(r0, tq), :])
        return c

    lax.fori_loop(0, i + 1, pass2, 0)
    l = jnp.sum(lp_s[...], axis=-1, keepdims=True)
    lam, lam_init = _lambda(lamp_ref)
    o0 = acc_s[0:tq, :] / l[0:tq, :]
    o1 = acc_s[tq:2 * tq, :] / l[tq:2 * tq, :]
    o_ref[...] = _diff_finish(o0, o1, lam, lam_init, ng_ref[...])


def _attn_prompt(q, k_all, v_all, lamp, norm_g, layer, nb, seq):
    n = q.shape[0]
    tq = _tile(seq, 256)
    nq = seq // tq
    return pl.pallas_call(
        _attn_prompt_kernel, out_shape=jax.ShapeDtypeStruct((n, DA_WIDTH), F32),
        grid=(nb, DA_HEADS, nq),
        in_specs=[pl.BlockSpec((tq, DA_DV), lambda b, h, i: (b * nq + i, h)),
                  pl.BlockSpec((None, None, DA_DV, seq), lambda b, h, i: (layer, b, h, 0)),
                  pl.BlockSpec((None, seq, DA_DV), lambda b, h, i: (layer, b, h)),
                  pl.BlockSpec((8, LANES), lambda b, h, i: (0, 0)),
                  pl.BlockSpec((1, DA_DV), lambda b, h, i: (0, 0))],
        out_specs=pl.BlockSpec((tq, DA_DV), lambda b, h, i: (b * nq + i, h)),
        scratch_shapes=[pltpu.VMEM((nq, DA_DV, tq), BF16), pltpu.VMEM((seq, DA_DV), BF16),
                        pltpu.VMEM((nq, 2 * tq, tq), F32), pltpu.VMEM((2 * tq, LANES), F32),
                        pltpu.VMEM((2 * tq, LANES), F32), pltpu.VMEM((2 * tq, DA_DV), F32)],
        compiler_params=_params("parallel", "parallel", "arbitrary"),
        name="attn_prompt")(q, k_all, v_all, lamp, norm_g)


def _attn_sample_kernel(pt_ref, q_ref, *refs, n_pages):
    kp = refs[:n_pages]
    vp = refs[n_pages:2 * n_pages]
    kn_ref, vn_ref, lamp_ref, ng_ref, o_ref = refs[2 * n_pages:]
    t = q_ref.shape[0]
    nrow = 2 * DA_HEADS * t
    hrow = 2 * t

    qs = q_ref[...] * (DA_DH ** -0.5)
    qt = jnp.concatenate([qs] * (2 * DA_HEADS), axis=0)
    r = lax.broadcasted_iota(jnp.int32, (nrow, DA_QK), 0) // t
    c = lax.broadcasted_iota(jnp.int32, (nrow, DA_QK), 1) // DA_DH
    qbd = jnp.where(r == c, qt, 0.0).astype(BF16)

    s_pages = [_dot(qbd, kp[i][...].astype(BF16)) for i in range(n_pages)]
    kb = _pad_rows(kn_ref[...], PAGE_SIZE).astype(BF16)
    vb = _pad_rows(vn_ref[...], PAGE_SIZE).astype(BF16)
    r = lax.broadcasted_iota(jnp.int32, (nrow, PAGE_SIZE), 0) % t
    c = lax.broadcasted_iota(jnp.int32, (nrow, PAGE_SIZE), 1)
    s_new = jnp.where(c <= r, _dot_nt(qbd, kb), NEG_BIG)
    s_all = s_pages + [s_new]
    m = jnp.max(functools.reduce(jnp.maximum, s_all), axis=-1, keepdims=True)
    p_all = [jnp.exp(s - m) for s in s_all]
    l = jnp.sum(functools.reduce(lambda a, b: a + b, p_all), axis=-1, keepdims=True)
    pb = [p.astype(BF16) for p in p_all]

    pc = PAGE_SIZE * DA_HEADS
    spread = jnp.where(lax.broadcasted_iota(jnp.int32, (PAGE_SIZE, pc), 1) // DA_HEADS
                       == lax.broadcasted_iota(jnp.int32, (PAGE_SIZE, pc), 0), 1.0, 0.0).astype(BF16)
    own = (lax.broadcasted_iota(jnp.int32, (nrow, pc), 1) % DA_HEADS
           == lax.broadcasted_iota(jnp.int32, (nrow, pc), 0) // hrow)
    acc = jnp.concatenate([_dot(pb[n_pages][h * hrow:(h + 1) * hrow, :], vb[:, h * DA_DV:(h + 1) * DA_DV])
                           for h in range(DA_HEADS)], axis=0)
    for i in range(n_pages):
        pe = jnp.where(own, _dot(pb[i], spread), 0.0).astype(BF16)
        acc = acc + _dot(pe, vp[i][...].astype(BF16))
    o = acc / l
    lam, lam_init = _lambda(lamp_ref)
    ng = ng_ref[...]
    for h in range(DA_HEADS):
        r0 = h * hrow
        o_ref[:, h * DA_DV:(h + 1) * DA_DV] = _diff_finish(o[r0:r0 + t, :], o[r0 + t:r0 + 2 * t, :],
                                                           lam, lam_init, ng)


def _attn_sample(q, k_new, v_new, cache_kt, cache_v, page_table, lamp, norm_g, layer, nb, seq):
    n = q.shape[0]
    n_pages = page_table.shape[1]
    row_map = lambda b, pt: (b, 0)
    new_map = lambda b, pt: (layer, b, 0)
    k_specs = [pl.BlockSpec((None, None, DA_QK, PAGE_SIZE),
                            lambda b, pt, i=i: (layer, pt[b * n_pages + i], 0, 0))
               for i in range(n_pages)]
    v_specs = [pl.BlockSpec((None, None, PAGE_SIZE * DA_HEADS, DA_DV),
                            lambda b, pt, i=i: (layer, pt[b * n_pages + i], 0, 0))
               for i in range(n_pages)]
    grid_spec = pltpu.PrefetchScalarGridSpec(
        num_scalar_prefetch=1, grid=(nb,),
        in_specs=[pl.BlockSpec((seq, DA_QK), row_map)] + k_specs + v_specs
        + [pl.BlockSpec((None, seq, DA_QK), new_map), pl.BlockSpec((None, seq, DA_WIDTH), new_map),
           pl.BlockSpec((8, LANES), lambda b, pt: (0, 0)),
           pl.BlockSpec((1, DA_DV), lambda b, pt: (0, 0))],
        out_specs=pl.BlockSpec((seq, DA_WIDTH), row_map))
    return pl.pallas_call(
        functools.partial(_attn_sample_kernel, n_pages=n_pages),
        out_shape=jax.ShapeDtypeStruct((n, DA_WIDTH), F32),
        grid_spec=grid_spec, compiler_params=_params("parallel"),
        name="attn_sample")(page_table.reshape(-1), q, *([cache_kt] * n_pages),
                            *([cache_v] * n_pages), k_new, v_new, lamp, norm_g)


def _even_out_kernel(oh_ref, od_ref, x_ref, w_ref, g_ref, b_ref, o_ref):
    h = (_dot(oh_ref[...].astype(BF16), w_ref[0:HG_WIDTH, :])
         + _dot(od_ref[...].astype(BF16), w_ref[HG_WIDTH:D_MODEL, :]))
    o_ref[...] = _layer_norm(ALPHA * x_ref[...] + h, g_ref[...], b_ref[...])


def _even_out(oh, od, x, w_out_bf, g, b):
    n = x.shape[0]
    tm = _tile(n, 512)
    vec = pl.BlockSpec((1, D_MODEL), lambda i: (0, 0))
    return pl.pallas_call(
        _even_out_kernel, out_shape=jax.ShapeDtypeStruct((n, D_MODEL), F32), grid=(n // tm,),
        in_specs=[_row_spec(tm, HG_WIDTH), _row_spec(tm, DA_WIDTH), _row_spec(tm, D_MODEL),
                  _const_spec((D_MODEL, D_MODEL)), vec, vec],
        out_specs=_row_spec(tm, D_MODEL),
        compiler_params=_params("parallel"), name="even_out")(oh, od, x, w_out_bf, g, b)


FF_CHUNK = 1024


def _mlp_kernel(x_ref, p_ref, up_ref, down_ref, g_ref, b_ref, gate_ref, proj_ref, o_ref):
    x = x_ref[...]
    xb = x.astype(BF16)
    acc = jnp.zeros(x.shape, F32)
    for c in range(D_FF // FF_CHUNK):
        cs = slice(c * FF_CHUNK, (c + 1) * FF_CHUNK)
        hid = jnp.maximum(_dot(xb, up_ref[:, cs]), 0.0)
        acc = acc + _dot((hid * hid).astype(BF16), down_ref[cs, :])
    x2 = _layer_norm(ALPHA * x + acc, g_ref[...], b_ref[...])
    gate = _sigmoid(_dot(x2.astype(BF16), gate_ref[...]))
    o_ref[...] = x2 + gate * _dot(p_ref[...].astype(BF16), proj_ref[...])


def _mlp(x, p_all, layer, up_bf, down_bf, g, b, gate_bf, proj_bf):
    n = x.shape[0]
    tm = _tile(n, 512)
    vec = pl.BlockSpec((1, D_MODEL), lambda i: (0, 0))
    return pl.pallas_call(
        _mlp_kernel, out_shape=jax.ShapeDtypeStruct((n, D_MODEL), F32), grid=(n // tm,),
        in_specs=[_row_spec(tm, D_MODEL), pl.BlockSpec((None, tm, PLE_DIM), lambda i: (layer, i, 0)),
                  _const_spec((D_MODEL, D_FF)), _const_spec((D_FF, D_MODEL)), vec, vec,
                  _const_spec((D_MODEL, D_MODEL)), _const_spec((PLE_DIM, D_MODEL))],
        out_specs=_row_spec(tm, D_MODEL),
        compiler_params=_params("parallel"), name="mlp")(x, p_all, up_bf, down_bf, g, b, gate_bf,
                                                         proj_bf)


def _swap_index_with_lane_pos(zs, width):
    pos = lax.broadcasted_iota(jnp.int32, zs[0].shape, 1) % width
    s = width // 2
    while s >= 1:
        hi = (pos & s) != 0
        new = list(zs)
        for p in range(width):
            if p & s == 0:
                lo_arr, hi_arr = zs[p], zs[p + s]
                new[p] = jnp.where(hi, pltpu.roll(hi_arr, s, 1), lo_arr)
                new[p + s] = jnp.where(hi, hi_arr, pltpu.roll(lo_arr, LANES - s, 1))
        zs = new
        s //= 2
    return zs


def _tokens_to_chains(x, nseq, tt, out_ref):
    sub = LANES // nseq
    for sb in range(tt // sub):
        heads = []
        for g in range(D_MODEL // LANES):
            tile = jnp.concatenate(
                [x[b * tt + sb * sub:b * tt + (sb + 1) * sub, g * LANES:(g + 1) * LANES]
                 for b in range(nseq)], axis=0)
            tile_t = tile.T
            heads += [tile_t[0:RW_HEAD, :], tile_t[RW_HEAD:2 * RW_HEAD, :]]
        for t, w in enumerate(_swap_index_with_lane_pos(heads, sub)):
            out_ref[sb * sub + t] = w


def _rwkv_pre_kernel(*refs, has_vfirst, chains, want_v_tokens):
    if chains:
        x_ref, x8_ref, sh_ref = refs[:3]
        refs = refs[3:]
    else:
        x_ref, xp_ref = refs[:2]
        refs = refs[2:]
    vf_ref = vt_o = v0_ref = v1_ref = v2_ref = None
    if has_vfirst:
        vf_ref, refs = refs[0], refs[1:]
    (mix_ref, wr_ref, wk_ref, wv_ref, w0_ref, w1_ref, w2_ref, a0_ref, a1_ref, a2_ref) = refs[:10]
    refs = refs[10:]
    if has_vfirst:
        v0_ref, v1_ref, v2_ref = refs[:3]
        refs = refs[3:]
    g1_ref, g2_ref, r_o, k_o, v_o, g_o, dec_o, a_o = refs[:8]
    if want_v_tokens:
        vt_o = refs[8]
    if chains:
        nseq, tt = x_ref.shape[0], x_ref.shape[1]
        x = x_ref[...].reshape(nseq * tt, D_MODEL)
        first = pl.program_id(0) == 0
        before = jnp.where(first, sh_ref[...], x8_ref[:, SUBLANES - 1, :])
        before = jnp.concatenate([jnp.broadcast_to(before[b:b + 1, :], (tt, D_MODEL))
                                  for b in range(nseq)], axis=0)
        rowid = lax.broadcasted_iota(jnp.int32, x.shape, 0)
        x_prev = jnp.where(rowid % tt == 0, before, pltpu.roll(x, 1, 0))
    else:
        x = x_ref[...]
        x_prev = xp_ref[...]
    xx = x_prev - x
    xr, xw, xk, xv, xa, xg = ((x + xx * mix_ref[m:m + 1, :]).astype(BF16) for m in range(6))
    r = _dot(xr, wr_ref[...])
    k = _dot(xk, wk_ref[...])
    v = _dot(xv, wv_ref[...])
    lw = w0_ref[...] + _dot(jnp.tanh(_dot(xw, w1_ref[...])).astype(BF16), w2_ref[...])
    w = -(jnp.maximum(-lw, 0.0) + jnp.log(1.0 + jnp.exp(-jnp.abs(lw)))) - 0.5
    if has_vfirst:
        mixv = _sigmoid(v0_ref[...] + _dot(_dot(xv, v1_ref[...]).astype(BF16), v2_ref[...]))
        vf = vf_ref[...].reshape(v.shape) if chains else vf_ref[...]
        v = v + (vf - v) * mixv
    a = _sigmoid(a0_ref[...] + _dot(_dot(xa, a1_ref[...]).astype(BF16), a2_ref[...]))
    g = _dot(_sigmoid(_dot(xg, g1_ref[...])).astype(BF16), g2_ref[...])
    dec = jnp.exp(-jnp.exp(w))
    if chains:
        g_o[...] = g.reshape(g_o.shape)
        if want_v_tokens:
            vt_o[...] = v.reshape(vt_o.shape)
        for val, ref in ((r, r_o), (k, k_o), (v, v_o), (dec, dec_o), (a, a_o)):
            _tokens_to_chains(val, nseq, tt, ref)
    else:
        r_o[...] = r
        k_o[...] = k
        v_o[...] = v
        g_o[...] = g
        dec_o[...] = dec
        a_o[...] = a
        if want_v_tokens:
            vt_o[...] = v


def _chain_layout_ok(nb, seq):
    return nb * RW_HEADS == LANES and LANES // nb == RW_HEADS and seq % RW_TBLK == 0


def _rwkv_pre(x, shift0, v_first, wts, nb, seq, want_v_tokens):
    n = x.shape[0]
    has_vfirst = v_first is not None
    chains = _chain_layout_ok(nb, seq)
    vec = pl.BlockSpec((1, D_MODEL), lambda i: (0, 0))
    if chains:
        tt = RW_TBLK
        tok3 = pl.BlockSpec((nb, tt, D_MODEL), lambda i: (0, i, 0))
        x3 = x.reshape(nb, seq, D_MODEL)
        r8 = tt // SUBLANES
        acts = [x3, x3, shift0]
        act_specs = [tok3,
                     pl.BlockSpec((nb, SUBLANES, D_MODEL), lambda i: (0, jnp.maximum(i * r8 - 1, 0), 0)),
                     pl.BlockSpec((nb, D_MODEL), lambda i: (0, 0))]
        if has_vfirst:
            acts.append(v_first.reshape(nb, seq, D_MODEL))
            act_specs.append(tok3)
        grid = (seq // tt,)
        chain_out = jax.ShapeDtypeStruct((seq, RW_HEAD, LANES), F32)
        chain_spec = pl.BlockSpec((tt, RW_HEAD, LANES), lambda i: (i, 0, 0))
        tok_out = jax.ShapeDtypeStruct((nb, seq, D_MODEL), F32)
        out_shape = [chain_out, chain_out, chain_out, tok_out, chain_out, chain_out]
        out_specs = [chain_spec, chain_spec, chain_spec, tok3, chain_spec, chain_spec]
        if want_v_tokens:
            out_shape.append(tok_out)
            out_specs.append(tok3)
    else:
        tm = _tile(n, 256)
        xs = x.reshape(nb, seq, D_MODEL)
        x_prev = jnp.concatenate([shift0[:, None, :], xs[:, :-1]], axis=1).reshape(n, D_MODEL)
        acts = [x, x_prev] + ([v_first] if has_vfirst else [])
        act_specs = [_row_spec(tm, D_MODEL)] * len(acts)
        grid = (n // tm,)
        out_shape = [jax.ShapeDtypeStruct((n, D_MODEL), F32)] * 6
        out_specs = [_row_spec(tm, D_MODEL)] * 6
    names = ['mix', 'w_r', 'w_k', 'w_v', 'w0', 'w1', 'w2', 'a0', 'a1', 'a2']
    if has_vfirst:
        names += ['v0', 'v1', 'v2']
    names += ['g1', 'g2']
    ws = [wts[nm] for nm in names]
    w_specs = [vec if w.shape == (1, D_MODEL) else _const_spec(w.shape) for w in ws]
    outs = pl.pallas_call(
        functools.partial(_rwkv_pre_kernel, has_vfirst=has_vfirst, chains=chains,
                          want_v_tokens=want_v_tokens and chains),
        out_shape=tuple(out_shape), grid=grid,
        in_specs=act_specs + w_specs, out_specs=tuple(out_specs),
        compiler_params=_params("parallel"), name="rwkv_pre")(*acts, *ws)
    outs = list(outs)
    if chains:
        outs[3] = outs[3].reshape(n, D_MODEL)
        if want_v_tokens:
            outs[6] = outs[6].reshape(n, D_MODEL)
    elif want_v_tokens:
        outs.append(outs[2])
    return outs


RW_TBLK = 32


RW_IN = 5


def _rwkv_token(ins, v_rows, prm, s_s, y_s):
    r, k, v, dec, ag = ins
    kk_p, ka_p, rk_p, lg_p, lb_p = prm
    kk = k * kk_p
    kk = kk / jnp.maximum(jnp.sqrt(jnp.sum(kk * kk, axis=0, keepdims=True)), 1e-12)
    a = -kk
    b = kk * ag
    k2 = k * (1.0 + (ag - 1.0) * ka_p)
    wr = dec * r
    br = jnp.sum(b * r, axis=0, keepdims=True)
    kr = jnp.sum(k2 * r, axis=0, keepdims=True)

    def rows(i8, _):
        i0 = pl.multiple_of(i8 * SUBLANES, SUBLANES)
        v8 = v_rows(i0)
        ys = []
        for ii in range(SUBLANES):
            si = s_s[i0 + ii]
            sa = jnp.sum(si * a, axis=0, keepdims=True)
            qs = jnp.sum(si * wr, axis=0, keepdims=True)
            vi = v8[ii:ii + 1, :]
            s_s[i0 + ii] = si * dec + sa * b + vi * k2
            ys.append(qs + sa * br + vi * kr)
        y_s[pl.ds(i0, SUBLANES), :] = jnp.concatenate(ys, axis=0)
        return 0

    lax.fori_loop(0, RW_HEAD // SUBLANES, rows, 0)
    y = y_s[...]
    mu = jnp.mean(y, axis=0, keepdims=True)
    yc = y - mu
    var = jnp.mean(yc * yc, axis=0, keepdims=True)
    yn = yc * lax.rsqrt(var + GN_EPS) * lg_p + lb_p
    return yn + jnp.sum(r * k2 * rk_p, axis=0, keepdims=True) * v


RW_IBLK = 16
RW_ACC = 4


def _rwkv_token_keymajor(ins, v_rows, prm, s_s, y_s, vec_s, g_s):
    r, k, v, dec, ag = ins
    kk_p, ka_p, rk_p, lg_p, lb_p = prm
    kk = k * kk_p
    kk = kk / jnp.maximum(jnp.sqrt(jnp.sum(kk * kk, axis=0, keepdims=True)), 1e-12)
    b = kk * ag
    k2 = k * (1.0 + (ag - 1.0) * ka_p)
    br = jnp.sum(b * r, axis=0, keepdims=True)
    kr = jnp.sum(k2 * r, axis=0, keepdims=True)
    g_old = g_s[...]
    g_new = g_old * dec
    g_s[...] = g_new
    g_inv = 1.0 / g_new
    for idx, val in enumerate((-kk * g_old, r * g_new, b * g_inv, k2 * g_inv)):
        vec_s[idx] = val
    row = lambda idx, j: vec_s[idx, j:j + 1, :]

    for p in range(RW_HEAD // RW_IBLK):
        rs = pl.ds(p * RW_IBLK, RW_IBLK)
        sa = [None] * RW_ACC
        qs = [None] * RW_ACC
        for j in range(RW_HEAD):
            sj = s_s[j, rs, :]
            c = j % RW_ACC
            sa[c] = sj * row(0, j) if sa[c] is None else sa[c] + sj * row(0, j)
            qs[c] = sj * row(1, j) if qs[c] is None else qs[c] + sj * row(1, j)
        sa = functools.reduce(lambda x, y: x + y, sa)
        qs = functools.reduce(lambda x, y: x + y, qs)
        vi = v_rows(rs)
        for j in range(RW_HEAD):
            s_s[j, rs, :] = s_s[j, rs, :] + sa * row(2, j) + vi * row(3, j)
        y_s[rs, :] = qs + sa * br + vi * kr
    y = y_s[...]
    mu = jnp.mean(y, axis=0, keepdims=True)
    yc = y - mu
    var = jnp.mean(yc * yc, axis=0, keepdims=True)
    yn = yc * lax.rsqrt(var + GN_EPS) * lg_p + lb_p
    return yn + jnp.sum(r * k2 * rk_p, axis=0, keepdims=True) * v


def _rwkv_scan_kernel(r_ref, k_ref, v_ref, dec_ref, a_ref, kk_ref, ka_ref, rk_ref, lg_ref, lb_ref,
                      s0_ref, y_ref, sout_ref, s_s, y_s):
    tb = pl.program_id(1)
    in_refs = (r_ref, k_ref, v_ref, dec_ref, a_ref)
    prm_refs = (kk_ref, ka_ref, rk_ref, lg_ref, lb_ref)

    @pl.when(tb == 0)
    def _():
        s_s[...] = s0_ref[...]

    def step(t, _):
        y_ref[t] = _rwkv_token([ref[t] for ref in in_refs],
                               lambda i0: v_ref[t, pl.ds(i0, SUBLANES), :],
                               [p[...] for p in prm_refs], s_s, y_s)
        return 0

    lax.fori_loop(0, r_ref.shape[0], step, 0)

    @pl.when(tb == pl.num_programs(1) - 1)
    def _():
        sout_ref[...] = s_s[...]


def _rwkv_scan_tokens_kernel(r_ref, k_ref, v_ref, dec_ref, a_ref, kk_ref, ka_ref, rk_ref, lg_ref,
                             lb_ref, s0_ref, y_ref, sout_ref, s_s, y_s, yt_s, vec_s, g_s):
    tb = pl.program_id(1)
    nseq, tt = y_ref.shape[0], y_ref.shape[1]
    sub = LANES // nseq
    in_refs = (r_ref, k_ref, v_ref, dec_ref, a_ref)
    prm_refs = (kk_ref, ka_ref, rk_ref, lg_ref, lb_ref)

    @pl.when(tb == 0)
    def _():
        s_s[...] = s0_ref[...]
        g_s[...] = jnp.ones(g_s.shape, F32)

    def sub_block(sb, _):
        t0 = pl.multiple_of(sb * sub, sub)

        def step(t, _):
            yt_s[t] = _rwkv_token_keymajor([ref[t0 + t] for ref in in_refs],
                                           lambda rs: v_ref[t0 + t, rs, :],
                                           [p[...] for p in prm_refs], s_s, y_s, vec_s, g_s)
            return 0

        lax.fori_loop(0, sub, step, 0)
        for j in range(RW_HEAD):
            s_s[j] = s_s[j] * g_s[j:j + 1, :]
        g_s[...] = jnp.ones(g_s.shape, F32)
        heads =_swap_index_with_lane_pos([yt_s[t] for t in range(sub)], sub)
        for g in range(D_MODEL // LANES):
            tile = jnp.concatenate([heads[2 * g], heads[2 * g + 1]], axis=0).T
            for b in range(nseq):
                y_ref[b, pl.ds(t0, sub), g * LANES:(g + 1) * LANES] = tile[b * sub:(b + 1) * sub, :]
        return 0

    lax.fori_loop(0, tt // sub, sub_block, 0)

    @pl.when(tb == pl.num_programs(1) - 1)
    def _():
        sout_ref[...] = s_s[...]


def _rwkv_scan(r, k, v, dec, a, s0_all, s_layer, wts, nb, seq):
    nl = s0_all.shape[0]
    n = nb * seq
    hd = RW_HEAD
    batch_lanes = nb % LANES == 0
    if batch_lanes:
        nbg = nb // LANES
        groups, chains = RW_HEADS * nbg, LANES
        to_l = lambda x: (x.reshape(nbg, LANES, seq, RW_HEADS, hd).transpose(0, 3, 2, 4, 1)
                          .reshape(groups, seq, hd, chains))
        from_l = lambda y: (y.reshape(nbg, RW_HEADS, seq, hd, LANES).transpose(0, 4, 2, 1, 3)
                            .reshape(n, D_MODEL))
        par = lambda p: jnp.broadcast_to(p.reshape(1, RW_HEADS, hd, 1),
                                         (nbg, RW_HEADS, hd, chains)).reshape(groups, hd, chains)
        st_in = lambda s: (s.reshape(nl, nbg, LANES, RW_HEADS, hd, hd).transpose(0, 1, 3, 4, 5, 2)
                           .reshape(nl, groups, hd, hd, chains))
        st_out = lambda s: (s.reshape(nbg, RW_HEADS, hd, hd, LANES).transpose(0, 4, 1, 2, 3)
                            .reshape(nb, RW_HEADS, hd, hd))
    else:
        groups, chains = 1, nb * RW_HEADS
        to_l = lambda x: (x.reshape(nb, seq, RW_HEADS, hd).transpose(1, 3, 0, 2)
                          .reshape(1, seq, hd, chains))
        from_l = lambda y: (y.reshape(seq, hd, nb, RW_HEADS).transpose(2, 0, 3, 1)
                            .reshape(n, D_MODEL))
        par = lambda p: jnp.broadcast_to(p.reshape(RW_HEADS, hd).T.reshape(1, hd, 1, RW_HEADS),
                                         (1, hd, nb, RW_HEADS)).reshape(1, hd, chains)
        st_in = lambda s: s.transpose(0, 3, 4, 1, 2).reshape(nl, 1, hd, hd, chains)
        st_out = lambda s: s.reshape(hd, hd, nb, RW_HEADS).transpose(2, 3, 0, 1)
    tt = _tile(seq, RW_TBLK)
    prm = pl.BlockSpec((None, hd, chains), lambda g, t: (g, 0, 0))
    st = pl.BlockSpec((None, hd, hd, chains), lambda g, t: (g, 0, 0, 0))
    st0 = pl.BlockSpec((None, None, hd, hd, chains), lambda g, t: (s_layer, g, 0, 0, 0))
    prms = [par(wts[nm]) for nm in ('k_k', 'k_a', 'r_k', 'lnx_g', 'lnx_b')]
    if _chain_layout_ok(nb, seq):
        tok = pl.BlockSpec((nb, tt, D_MODEL), lambda g, t: (0, t, 0))
        act = pl.BlockSpec((tt, hd, chains), lambda g, t: (t, 0, 0))
        y, s_new = pl.pallas_call(
            _rwkv_scan_tokens_kernel,
            out_shape=(jax.ShapeDtypeStruct((nb, seq, D_MODEL), F32),
                       jax.ShapeDtypeStruct((1, hd, hd, chains), F32)),
            grid=(1, seq // tt),
            in_specs=[act] * RW_IN + [prm] * 5 + [st0],
            out_specs=(tok, st),
            scratch_shapes=[pltpu.VMEM((hd, hd, chains), F32), pltpu.VMEM((hd, chains), F32),
                            pltpu.VMEM((RW_HEADS, hd, chains), F32),
                            pltpu.VMEM((4, hd, chains), F32), pltpu.VMEM((hd, chains), F32)],
            compiler_params=_params("parallel", "arbitrary"),
            name="rwkv_scan_tokens")(r, k, v, dec, a, *prms, st_in(s0_all).swapaxes(2, 3))
        return y.reshape(n, D_MODEL), st_out(s_new.swapaxes(1, 2))
    act = pl.BlockSpec((None, tt, hd, chains), lambda g, t: (g, t, 0, 0))
    y, s_new = pl.pallas_call(
        _rwkv_scan_kernel,
        out_shape=(jax.ShapeDtypeStruct((groups, seq, hd, chains), F32),
                   jax.ShapeDtypeStruct((groups, hd, hd, chains), F32)),
        grid=(groups, seq // tt),
        in_specs=[act] * RW_IN + [prm] * 5 + [st0],
        out_specs=(act, st),
        scratch_shapes=[pltpu.VMEM((hd, hd, chains), F32), pltpu.VMEM((hd, chains), F32)],
        compiler_params=_params("parallel", "arbitrary"),
        name="rwkv_scan")(to_l(r), to_l(k), to_l(v), to_l(dec), to_l(a), *prms, st_in(s0_all))
    return from_l(y), st_out(s_new)


def _rwkv_post_kernel(y_ref, g_ref, x_ref, wo_ref, n1g_ref, n1b_ref, o_ref):
    h = _dot((y_ref[...] * g_ref[...]).astype(BF16), wo_ref[...])
    o_ref[...] = _layer_norm(ALPHA * x_ref[...] + h, n1g_ref[...], n1b_ref[...])


def _rwkv_post(y, g, x, wo_bf, n1g, n1b):
    n = x.shape[0]
    tm = _tile(n, 512)
    vec = pl.BlockSpec((1, D_MODEL), lambda i: (0, 0))
    return pl.pallas_call(
        _rwkv_post_kernel, out_shape=jax.ShapeDtypeStruct((n, D_MODEL), F32), grid=(n // tm,),
        in_specs=[_row_spec(tm, D_MODEL)] * 3 + [_const_spec((D_MODEL, D_MODEL)), vec, vec],
        out_specs=_row_spec(tm, D_MODEL),
        compiler_params=_params("parallel"), name="rwkv_post")(y, g, x, wo_bf, n1g, n1b)


def _trunk(x3, p4, hg_s0, wkv_s0, shift0, cache_k, cache_v, page_table, W):
    nb, seq, _ = x3.shape
    n = nb * seq
    x = x3.reshape(n, D_MODEL)
    new_hg, new_wkv, new_shift = [], [], []
    kv = None
    v_first = None
    for i in range(DEPTH):
        j = i // 2
        if i % 2 == 0:
            wk_t = W['wk_t'][j] if page_table is None else None
            h4, q, k_all, v_all = _even_in(x, W['w_in_even'][j], wk_t, j, kv, nb, seq)
            kv = (k_all, v_all)
            oh, s_hg = _hgrn(h4, hg_s0, min(j, hg_s0.shape[0] - 1), W['hg_lb_logits'],
                             W['hg_norm_g'][j], j, nb, seq)
            if page_table is None:
                od = _attn_prompt(q, k_all, v_all, W['lamp'][j], W['da_norm_g'][j], j, nb, seq)
            else:
                od = _attn_sample(q, k_all, v_all, cache_k, cache_v, page_table, W['lamp'][j],
                                  W['da_norm_g'][j], j, nb, seq)
            x = _even_out(oh, od, x, W['w_out_even'][j], W['ln1_g'][i], W['ln1_b'][i])
            new_hg.append(s_hg)
        else:
            need_vf = j == 0 and N_ODD > 1
            outs = _rwkv_pre(x, shift0[j], v_first, W['rw'][j], nb, seq, need_vf)
            r, k, v, g, dec, a = outs[:6]
            if need_vf:
                v_first = outs[6]
            y, s_wkv = _rwkv_scan(r, k, v, dec, a, wkv_s0, min(j, wkv_s0.shape[0] - 1),
                                  W['rw_scan'][j], nb, seq)
            new_shift.append(x.reshape(nb, seq, D_MODEL)[:, -1])
            new_wkv.append(s_wkv)
            x = _rwkv_post(y, g, x, W['rw_w_o'][j], W['ln1_g'][i], W['ln1_b'][i])
        x = _mlp(x, p4.reshape(DEPTH, n, PLE_DIM), i, W['mlp_up'][i], W['mlp_down'][i],
                 W['ln2_g'][i], W['ln2_b'][i], W['ple_gate'][i], W['ple_proj'][i])
    if page_table is None:
        k_out = kv[0].reshape(N_EVEN, nb, DA_HEADS, 2, DA_DH, seq).transpose(0, 1, 5, 2, 3, 4)
    else:
        k_out = kv[0].reshape(N_EVEN, nb, seq, DA_HEADS, 2, DA_DH)
    return (x.reshape(nb, seq, D_MODEL), k_out,
            kv[1].reshape(N_EVEN, nb, seq, DA_HEADS, DA_DV), jnp.stack(new_hg),
            jnp.stack(new_wkv), jnp.stack(new_shift))


def kernel(x_prompt, x_sample, cache_k, cache_v, state_hgrn, state_wkv, state_shift, page_table,
           p_prompt, p_sample, w_in_even, w_out_even, hg_lb_logits, hg_norm_g,
           da_lam_q1, da_lam_k1, da_lam_q2, da_lam_k2, da_norm_g,
           rw_mix, rw_w_r, rw_w_k, rw_w_v, rw_w_o, rw_w0, rw_w1, rw_w2,
           rw_a0, rw_a1, rw_a2, rw_v0, rw_v1, rw_v2, rw_g1, rw_g2,
           rw_k_k, rw_k_a, rw_r_k, rw_lnx_g, rw_lnx_b,
           ln1_g, ln1_b, ln2_g, ln2_b, mlp_up, mlp_down, ple_proj, ple_gate):
    bf = lambda w: w.astype(BF16)
    vec = lambda w: w.reshape(1, -1)

    def lam_rows(j):
        pad = lambda a: jnp.pad(a.astype(F32), (0, LANES - a.shape[0]))
        lam_init = 0.8 - 0.6 * math.exp(-0.3 * (2 * j))
        rows = [pad(da_lam_q1[j]), pad(da_lam_k1[j]), pad(da_lam_q2[j]), pad(da_lam_k2[j]),
                jnp.full((LANES,), lam_init, F32)]
        return jnp.stack(rows + [jnp.zeros((LANES,), F32)] * 3)

    rw = []
    for j in range(N_ODD):
        d = dict(mix=rw_mix[j], w_r=bf(rw_w_r[j]), w_k=bf(rw_w_k[j]), w_v=bf(rw_w_v[j]),
                 w0=vec(rw_w0[j]), w1=bf(rw_w1[j]), w2=bf(rw_w2[j]),
                 a0=vec(rw_a0[j]), a1=bf(rw_a1[j]), a2=bf(rw_a2[j]),
                 g1=bf(rw_g1[j]), g2=bf(rw_g2[j]))
        if j > 0:
            d.update(v0=vec(rw_v0[j - 1]), v1=bf(rw_v1[j - 1]), v2=bf(rw_v2[j - 1]))
        rw.append(d)
    W = dict(
        w_in_even=[bf(w_in_even[j]) for j in range(N_EVEN)],
        wk_t=[bf(w_in_even[j][:, 4 * HG_WIDTH + DA_QK:4 * HG_WIDTH + 2 * DA_QK].T) for j in range(N_EVEN)],
        w_out_even=[bf(w_out_even[j]) for j in range(N_EVEN)],
        hg_lb_logits=hg_lb_logits,
        hg_norm_g=[vec(hg_norm_g[j]) for j in range(N_EVEN)],
        da_norm_g=[vec(da_norm_g[j]) for j in range(N_EVEN)],
        lamp=[lam_rows(j) for j in range(N_EVEN)],
        rw=rw,
        rw_scan=[dict(k_k=rw_k_k[j], k_a=rw_k_a[j], r_k=rw_r_k[j], lnx_g=rw_lnx_g[j],
                      lnx_b=rw_lnx_b[j]) for j in range(N_ODD)],
        rw_w_o=[bf(rw_w_o[j]) for j in range(N_ODD)],
        ln1_g=[vec(ln1_g[i]) for i in range(DEPTH)], ln1_b=[vec(ln1_b[i]) for i in range(DEPTH)],
        ln2_g=[vec(ln2_g[i]) for i in range(DEPTH)], ln2_b=[vec(ln2_b[i]) for i in range(DEPTH)],
        mlp_up=[bf(mlp_up[i]) for i in range(DEPTH)], mlp_down=[bf(mlp_down[i]) for i in range(DEPTH)],
        ple_gate=[bf(ple_gate[i]) for i in range(DEPTH)], ple_proj=[bf(ple_proj[i]) for i in range(DEPTH)],
    )
    nbp = x_prompt.shape[0]
    dt = x_prompt.dtype
    hg0 = jnp.zeros((1, nbp, HG_HEADS, HG_DK, HG_DK), dt)
    wkv0 = jnp.zeros((1, nbp, RW_HEADS, RW_HEAD, RW_HEAD), dt)
    sh0 = jnp.zeros((N_ODD, nbp, D_MODEL), dt)
    y_p, k_p, v_p, hg_p, wkv_p, sh_p = _trunk(x_prompt, p_prompt, hg0, wkv0, sh0, None, None, None, W)
    ckt = jnp.transpose(cache_k, (0, 1, 3, 4, 5, 2)).reshape(
        cache_k.shape[0], cache_k.shape[1], DA_QK, PAGE_SIZE)
    cv = cache_v.reshape(cache_v.shape[0], cache_v.shape[1], PAGE_SIZE * DA_HEADS, DA_DV)
    y_s, k_s, v_s, hg_s, wkv_s, sh_s = _trunk(x_sample, p_sample, state_hgrn, state_wkv, state_shift,
                                              ckt, cv, page_table, W)
    return (y_p, y_s, k_p, v_p, k_s, v_s, hg_p, hg_s, wkv_p, wkv_s, sh_p, sh_s)
```

```python
import functools
import math

import jax
import jax.numpy as jnp
from jax import lax
from jax.experimental import pallas as pl
from jax.experimental.pallas import tpu as pltpu

F32 = jnp.float32
BF16 = jnp.bfloat16

D_MODEL = 1024
DEPTH = 4
PAGE_SIZE = 128
N_EVEN = (DEPTH + 1) // 2
N_ODD = DEPTH // 2
PLE_DIM = 256
D_FF = 4 * D_MODEL
HG_WIDTH = D_MODEL // 2
HG_HEADS = 4
HG_DK = HG_WIDTH // HG_HEADS
DA_WIDTH = D_MODEL - HG_WIDTH
DA_HEADS = 4
DA_DV = DA_WIDTH // DA_HEADS
DA_DH = DA_DV // 2
DA_QK = DA_HEADS * 2 * DA_DH
IN_COLS = 4 * HG_WIDTH + 2 * DA_QK + DA_WIDTH
RW_HEAD = 64
RW_HEADS = D_MODEL // RW_HEAD
ALPHA = (2.0 * DEPTH) ** 0.25
LN_EPS = 1e-5
RMS_EPS = 1e-6
GN_EPS = 64e-5
F_MIN = 1e-30
NEG_BIG = -1e30

LANES = 128
SUBLANES = 8
VMEM_LIMIT = 52 * 1024 * 1024
HG_CHUNK = 16
MM_ROWS = 16


def _params(*sem):
    return pltpu.CompilerParams(dimension_semantics=sem, vmem_limit_bytes=VMEM_LIMIT)


def _const_spec(shape):
    nd = len(shape)
    return pl.BlockSpec(shape, lambda *_: (0,) * nd, pipeline_mode=pl.Buffered(1))


def _row_spec(rows, cols):
    return pl.BlockSpec((rows, cols), lambda i: (i, 0))


def _tile(n, pref):
    t = min(n, pref)
    while n % t:
        t //= 2
    return t


def _dot(a, b):
    return jnp.dot(a, b, preferred_element_type=F32)


def _dot_nt(a, b):
    return lax.dot_general(a, b, (((1,), (1,)), ((), ())), preferred_element_type=F32)


def _dot_tn(a, b):
    return lax.dot_general(a, b, (((0,), (0,)), ((), ())), preferred_element_type=F32)


def _split3(x):
    h = x.astype(BF16)
    r = x - h.astype(F32)
    m = r.astype(BF16)
    l = (r - m.astype(F32)).astype(BF16)
    return h, m, l


def _dot_x3(x, rhs_bf16):
    h, m, l = _split3(x)
    return _dot(h, rhs_bf16) + _dot(m, rhs_bf16) + _dot(l, rhs_bf16)


def _dot_x3_lhs(lhs_bf16, x):
    h, m, l = _split3(x)
    return _dot(lhs_bf16, h) + _dot(lhs_bf16, m) + _dot(lhs_bf16, l)


def _layer_norm(z, g, b):
    mu = jnp.mean(z, axis=-1, keepdims=True)
    zc = z - mu
    var = jnp.mean(zc * zc, axis=-1, keepdims=True)
    return zc * lax.rsqrt(var + LN_EPS) * g + b


def _sigmoid(x):
    return 1.0 / (1.0 + jnp.exp(-x))


def _head_ones(width):
    r = lax.broadcasted_iota(jnp.int32, (LANES, LANES), 0) // width
    c = lax.broadcasted_iota(jnp.int32, (LANES, LANES), 1) // width
    return jnp.where(r == c, 1.0, 0.0).astype(BF16)


def _segsum(x, ones):
    cols = x.shape[1] // LANES
    parts = [_dot_x3(x[:, c * LANES:(c + 1) * LANES], ones) for c in range(cols)]
    return parts[0] if cols == 1 else jnp.concatenate(parts, axis=1)


def _even_in_kernel(x_ref, w_ref, *refs, k_transposed):
    h4_ref, q_ref, k_ref, v_ref = refs[-4:]
    xb = x_ref[...].astype(BF16)
    cw = HG_WIDTH
    for c in range(4):
        h4_ref[:, c * cw:(c + 1) * cw] = _dot(xb, w_ref[:, c * cw:(c + 1) * cw])
    q0 = 4 * HG_WIDTH
    q_ref[...] = _dot(xb, w_ref[:, q0:q0 + DA_QK])
    if k_transposed:
        k_ref[...] = _dot_nt(refs[0][...], xb)
    else:
        k_ref[...] = _dot(xb, w_ref[:, q0 + DA_QK:q0 + 2 * DA_QK])
    v_ref[...] = _dot(xb, w_ref[:, q0 + 2 * DA_QK:q0 + 2 * DA_QK + DA_WIDTH])


def _even_in(x, w_in_bf, wk_t_bf, layer, kv_stacks, nb, seq):
    n = x.shape[0]
    k_transposed = wk_t_bf is not None
    tm = _tile(seq if k_transposed else n, 512)
    slab =lambda width: pl.BlockSpec((None, tm, width), lambda i: (layer, i, 0))
    in_specs = [_row_spec(tm, D_MODEL), _const_spec((D_MODEL, IN_COLS))]
    args = [x, w_in_bf]
    if k_transposed:
        tps = seq // tm
        k_shape = (N_EVEN, nb, DA_QK, seq)
        k_spec = pl.BlockSpec((None, None, DA_QK, tm), lambda i: (layer, i // tps, 0, i % tps))
        in_specs.append(_const_spec((DA_QK, D_MODEL)))
        args.append(wk_t_bf)
    else:
        k_shape = (N_EVEN, n, DA_QK)
        k_spec = slab(DA_QK)
    outs = (jax.ShapeDtypeStruct((n, 4 * HG_WIDTH), F32), jax.ShapeDtypeStruct((n, DA_QK), F32),
            jax.ShapeDtypeStruct(k_shape, F32), jax.ShapeDtypeStruct((N_EVEN, n, DA_WIDTH), F32))
    aliases = {}
    if kv_stacks is not None:
        aliases = {len(args): 2, len(args) + 1: 3}
        in_specs += [pl.BlockSpec(memory_space=pl.ANY)] * 2
        args += list(kv_stacks)
    return pl.pallas_call(
        functools.partial(_even_in_kernel, k_transposed=k_transposed), out_shape=outs,
        grid=(n // tm,), in_specs=in_specs,
        out_specs=(_row_spec(tm, 4 * HG_WIDTH), _row_spec(tm, DA_QK), k_spec, slab(DA_WIDTH)),
        input_output_aliases=aliases,
        compiler_params=_params("parallel"), name="even_in")(*args)


def _pad_rows(x, rows):
    if x.shape[0] >= rows:
        return x
    return jnp.concatenate([x, jnp.zeros((rows - x.shape[0], x.shape[1]), x.dtype)], axis=0)


def _hgrn_kernel(h4_ref, s0_ref, lbl_ref, ng_ref, o_ref, sout_ref,
                 st_ref, q_s, k_s, b_s, *, layer, chunk, carry):
    tl = h4_ref.shape[0]
    nchunk = tl // chunk
    w = HG_WIDTH

    rows = [lbl_ref[i:i + 1, :] for i in range(N_EVEN)]
    mx = functools.reduce(jnp.maximum, rows)
    es = [jnp.exp(r - mx) for r in rows]
    den = functools.reduce(lambda a, b: a + b, es)
    sm = [e / den for e in es]
    cs = functools.reduce(lambda a, b: a + b, sm[:layer + 1])
    lb = cs - sm[0]

    hq = h4_ref[:, 0:w]
    hf = h4_ref[:, w:2 * w]
    q_s[...] = hq * _sigmoid(hq)
    f = lb + (1.0 - lb) * _sigmoid(hf)
    g = jnp.log(jnp.maximum(f, F_MIN))
    k_s[...] = 1.0 - f
    ri = lax.broadcasted_iota(jnp.int32, (tl, tl), 0)
    ci = lax.broadcasted_iota(jnp.int32, (tl, tl), 1)
    tril = jnp.where((ri // chunk == ci // chunk) & (ci <= ri), 1.0, 0.0).astype(BF16)
    b_s[...] = _dot_x3_lhs(tril, g)

    if carry:
        @pl.when(pl.program_id(1) == 0)
        def _():
            for h in range(HG_HEADS):
                st_ref[h] = s0_ref[0, h].T

    rowid = lax.broadcasted_iota(jnp.int32, (chunk, 1), 0)
    ng = ng_ref[...]

    def do_chunk(c, _):
        r0 = pl.multiple_of(c * chunk, chunk)
        rs = pl.ds(r0, chunk)
        for h in range(HG_HEADS):
            cs_ = slice(h * HG_DK, (h + 1) * HG_DK)
            q = q_s[rs, cs_]
            k = k_s[rs, cs_]
            b = b_s[rs, cs_]
            v = h4_ref[rs, 2 * w + h * HG_DK:2 * w + (h + 1) * HG_DK]
            hg = h4_ref[rs, 3 * w + h * HG_DK:3 * w + (h + 1) * HG_DK]
            if carry:
                st = st_ref[h]
            else:
                st = s0_ref[c, h].T
            qd = _pad_rows((q * jnp.exp(b)).astype(BF16), MM_ROWS)
            o = _dot_nt(qd, st.astype(BF16))[:chunk]
            for s in range(chunk):
                keep = rowid >= s
                e = jnp.where(keep, jnp.exp(jnp.where(keep, b - b[s:s + 1, :], 0.0)), 0.0)
                col = jnp.sum(q * e * k[s:s + 1, :], axis=-1, keepdims=True)
                o = o + col * v[s:s + 1, :]
            b_last = b[chunk - 1:chunk, :]
            kd = _pad_rows((k * jnp.exp(b_last - b)).astype(BF16), MM_ROWS)
            vd = _pad_rows(v.astype(BF16), MM_ROWS)
            st_new = st * jnp.exp(b_last) + _dot_tn(vd, kd)
            if carry:
                st_ref[h] = st_new
            else:
                sout_ref[c, h] = st_new.T
            on = o * lax.rsqrt(jnp.mean(o * o, axis=-1, keepdims=True) + RMS_EPS) * ng
            o_ref[rs, cs_] = on * _sigmoid(hg)
        return 0

    lax.fori_loop(0, nchunk, do_chunk, 0)

    if carry:
        @pl.when(pl.program_id(1) == pl.num_programs(1) - 1)
        def _():
            for h in range(HG_HEADS):
                sout_ref[0, h] = st_ref[h].T


def _hgrn(h4, s0_all, s_layer, lb_logits, norm_g, layer, nb, seq):
    n = h4.shape[0]
    chunk = math.gcd(seq, HG_CHUNK)
    carry = seq > chunk
    if carry:
        tl = _tile(seq, 256)
        grid = (nb, seq // tl)
        row_map = lambda b, l: (b * (seq // tl) + l, 0)
        st_map = lambda b, l: (b, 0, 0, 0)
        st_blk = (1, HG_HEADS, HG_DK, HG_DK)
        sem = ("parallel", "arbitrary")
    else:
        bt = _tile(nb, 8)
        tl = bt * seq
        grid = (nb // bt, 1)
        row_map = lambda b, l: (b, 0)
        st_map = lambda b, l: (b, 0, 0, 0)
        st_blk = (bt, HG_HEADS, HG_DK, HG_DK)
        sem = ("parallel", "arbitrary")
    kern = functools.partial(_hgrn_kernel, layer=layer, chunk=chunk, carry=carry)
    return pl.pallas_call(
        kern,
        out_shape=(jax.ShapeDtypeStruct((n, HG_WIDTH), F32),
                   jax.ShapeDtypeStruct((nb, HG_HEADS, HG_DK, HG_DK), F32)),
        grid=grid,
        in_specs=[pl.BlockSpec((tl, 4 * HG_WIDTH), row_map),
                  pl.BlockSpec((None,) + st_blk, lambda b, l: (s_layer, b, 0, 0, 0)),
                  pl.BlockSpec((N_EVEN, HG_WIDTH), lambda b, l: (0, 0)),
                  pl.BlockSpec((1, HG_DK), lambda b, l: (0, 0))],
        out_specs=(pl.BlockSpec((tl, HG_WIDTH), row_map), pl.BlockSpec(st_blk, st_map)),
        scratch_shapes=[pltpu.VMEM((HG_HEADS, HG_DK, HG_DK), F32),
                        pltpu.VMEM((tl, HG_WIDTH), F32), pltpu.VMEM((tl, HG_WIDTH), F32),
                        pltpu.VMEM((tl, HG_WIDTH), F32)],
        compiler_params=_params(*sem), name="hgrn2")(h4, s0_all, lb_logits, norm_g)


def _lambda(lamp_ref):
    lam_init = lamp_ref[4:5, 0:1]
    l1 = jnp.sum(lamp_ref[0:1, :] * lamp_ref[1:2, :], axis=-1, keepdims=True)
    l2 = jnp.sum(lamp_ref[2:3, :] * lamp_ref[3:4, :], axis=-1, keepdims=True)
    return jnp.exp(l1) - jnp.exp(l2) + lam_init, lam_init


def _diff_finish(o0, o1, lam, lam_init, ng):
    o = o0 - lam * o1
    return o * lax.rsqrt(jnp.mean(o * o, axis=-1, keepdims=True) + RMS_EPS) * ng * (1.0 - lam_init)


def _lane_fold(x, op):
    parts = [x[:, c * LANES:(c + 1) * LANES] for c in range(x.shape[1] // LANES)]
    return functools.reduce(op, parts)


def _attn_prompt_kernel(q_ref, k_ref, v_ref, lamp_ref, ng_ref, o_ref, kb_s, vb_s, s_s, mp_s, lp_s,
                        acc_s):
    tq = q_ref.shape[0]
    i = pl.program_id(2)

    @pl.when(i == 0)
    def _():
        for jb in range(kb_s.shape[0]):
            kb_s[jb] = k_ref[:, jb * tq:(jb + 1) * tq].astype(BF16)
        vb_s[...] = v_ref[...].astype(BF16)

    lane = lax.broadcasted_iota(jnp.int32, (tq, DA_DV), 1)
    qs = q_ref[...] * (DA_DH ** -0.5)
    q2 = jnp.concatenate([jnp.where(lane < DA_DH, qs, 0.0), jnp.where(lane >= DA_DH, qs, 0.0)],
                         axis=0).astype(BF16)
    mp_s[...] = jnp.full(mp_s.shape, NEG_BIG, F32)

    def scores(j, masked):
        s = _dot(q2, kb_s[j])
        if masked:
            row = lax.broadcasted_iota(jnp.int32, (2 * tq, tq), 0) % tq
            col = lax.broadcasted_iota(jnp.int32, (2 * tq, tq), 1)
            s = jnp.where(col <= row, s, NEG_BIG)
        s_s[j] = s
        mp_s[...] = jnp.maximum(mp_s[...], _lane_fold(s, jnp.maximum))

    def pass1(j, c):
        scores(j, False)
        return c

    lax.fori_loop(0, i, pass1, 0)
    scores(i, True)
    m = jnp.max(mp_s[...], axis=-1, keepdims=True)
    lp_s[...] = jnp.zeros(lp_s.shape, F32)
    acc_s[...] = jnp.zeros(acc_s.shape, F32)

    def pass2(j, c):
        r0 = pl.multiple_of(j * tq, tq)
        p = jnp.exp(s_s[j] - m)
        lp_s[...] += _lane_fold(p, lambda a, b: a + b)
        acc_s[...] += _dot(p.astype(BF16), vb_s[pl.ds(r0, tq), :])
        return c

    lax.fori_loop(0, i + 1, pass2, 0)
    l = jnp.sum(lp_s[...], axis=-1, keepdims=True)
    lam, lam_init = _lambda(lamp_ref)
    o0 = acc_s[0:tq, :] / l[0:tq, :]
    o1 = acc_s[tq:2 * tq, :] / l[tq:2 * tq, :]
    o_ref[...] = _diff_finish(o0, o1, lam, lam_init, ng_ref[...])


def _attn_prompt(q, k_all, v_all, lamp, norm_g, layer, nb, seq):
    n = q.shape[0]
    tq = _tile(seq, 512)
    nq = seq // tq
    return pl.pallas_call(
        _attn_prompt_kernel, out_shape=jax.ShapeDtypeStruct((n, DA_WIDTH), F32),
        grid=(nb, DA_HEADS, nq),
        in_specs=[pl.BlockSpec((tq, DA_DV), lambda b, h, i: (b * nq + i, h)),
                  pl.BlockSpec((None, None, DA_DV, seq), lambda b, h, i: (layer, b, h, 0)),
                  pl.BlockSpec((None, seq, DA_DV), lambda b, h, i: (layer, b, h)),
                  pl.BlockSpec((8, LANES), lambda b, h, i: (0, 0)),
                  pl.BlockSpec((1, DA_DV), lambda b, h, i: (0, 0))],
        out_specs=pl.BlockSpec((tq, DA_DV), lambda b, h, i: (b * nq + i, h)),
        scratch_shapes=[pltpu.VMEM((nq, DA_DV, tq), BF16), pltpu.VMEM((seq, DA_DV), BF16),
                        pltpu.VMEM((nq, 2 * tq, tq), F32), pltpu.VMEM((2 * tq, LANES), F32),
                        pltpu.VMEM((2 * tq, LANES), F32), pltpu.VMEM((2 * tq, DA_DV), F32)],
        compiler_params=_params("parallel", "parallel", "arbitrary"),
        name="attn_prompt")(q, k_all, v_all, lamp, norm_g)


def _attn_sample_kernel(pt_ref, q_ref, *refs, n_pages):
    kp = refs[:n_pages]
    vp = refs[n_pages:2 * n_pages]
    kn_ref, vn_ref, lamp_ref, ng_ref, o_ref, kt_s, v_s = refs[2 * n_pages:]
    t = q_ref.shape[0]
    nrow = 2 * DA_HEADS * t
    hrow = 2 * t

    qs = q_ref[...] * (DA_DH ** -0.5)
    qt = jnp.concatenate([qs] * (2 * DA_HEADS), axis=0)
    r = lax.broadcasted_iota(jnp.int32, (nrow, DA_QK), 0) // t
    c = lax.broadcasted_iota(jnp.int32, (nrow, DA_QK), 1) // DA_DH
    qbd = jnp.where(r == c, qt, 0.0).astype(BF16)

    pc = PAGE_SIZE * DA_HEADS
    for i in range(n_pages):
        kt_s[:, i * PAGE_SIZE:(i + 1) * PAGE_SIZE] = kp[i][...].astype(BF16)
        v_s[i * pc:(i + 1) * pc, :] = vp[i][...].astype(BF16)
    s_old = _dot(qbd, kt_s[...])
    kb = _pad_rows(kn_ref[...], PAGE_SIZE).astype(BF16)
    vb = _pad_rows(vn_ref[...], PAGE_SIZE).astype(BF16)
    r = lax.broadcasted_iota(jnp.int32, (nrow, PAGE_SIZE), 0) % t
    c = lax.broadcasted_iota(jnp.int32, (nrow, PAGE_SIZE), 1)
    s_new = jnp.where(c <= r, _dot_nt(qbd, kb), NEG_BIG)
    m = jnp.maximum(jnp.max(s_old, axis=-1, keepdims=True), jnp.max(s_new, axis=-1, keepdims=True))
    p_old = jnp.exp(s_old - m)
    p_new = jnp.exp(s_new - m)
    l = jnp.sum(p_old, axis=-1, keepdims=True) + jnp.sum(p_new, axis=-1, keepdims=True)
    pb_old = p_old.astype(BF16)
    pb_new = p_new.astype(BF16)

    spread = jnp.where(lax.broadcasted_iota(jnp.int32, (PAGE_SIZE, pc), 1) // DA_HEADS
                       == lax.broadcasted_iota(jnp.int32, (PAGE_SIZE, pc), 0), 1.0, 0.0).astype(BF16)
    own = (lax.broadcasted_iota(jnp.int32, (n_pages * nrow, pc), 1) % DA_HEADS
           == (lax.broadcasted_iota(jnp.int32, (n_pages * nrow, pc), 0) % nrow) // hrow)
    p_rows = jnp.concatenate([pb_old[:, i * PAGE_SIZE:(i + 1) * PAGE_SIZE] for i in range(n_pages)],
                             axis=0)
    pe = jnp.where(own, _dot(p_rows, spread), 0.0).astype(BF16)
    pe = jnp.concatenate([pe[i * nrow:(i + 1) * nrow, :] for i in range(n_pages)], axis=1)
    acc = _dot(pe, v_s[...])
    acc = acc + jnp.concatenate([_dot(pb_new[h * hrow:(h + 1) * hrow, :], vb[:, h * DA_DV:(h + 1) * DA_DV])
                                 for h in range(DA_HEADS)], axis=0)
    o = acc / l
    lam, lam_init = _lambda(lamp_ref)
    ng = ng_ref[...]
    for h in range(DA_HEADS):
        r0 = h * hrow
        o_ref[:, h * DA_DV:(h + 1) * DA_DV] = _diff_finish(o[r0:r0 + t, :], o[r0 + t:r0 + 2 * t, :],
                                                           lam, lam_init, ng)


def _attn_sample(q, k_new, v_new, cache_kt, cache_v, page_table, lamp, norm_g, layer, nb, seq):
    n = q.shape[0]
    n_pages = page_table.shape[1]
    row_map = lambda b, pt: (b, 0)
    new_map = lambda b, pt: (layer, b, 0)
    k_specs = [pl.BlockSpec((None, None, DA_QK, PAGE_SIZE),
                            lambda b, pt, i=i: (layer, pt[b * n_pages + i], 0, 0))
               for i in range(n_pages)]
    v_specs = [pl.BlockSpec((None, None, PAGE_SIZE * DA_HEADS, DA_DV),
                            lambda b, pt, i=i: (layer, pt[b * n_pages + i], 0, 0))
               for i in range(n_pages)]
    grid_spec = pltpu.PrefetchScalarGridSpec(
        num_scalar_prefetch=1, grid=(nb,),
        in_specs=[pl.BlockSpec((seq, DA_QK), row_map)] + k_specs + v_specs
        + [pl.BlockSpec((None, seq, DA_QK), new_map), pl.BlockSpec((None, seq, DA_WIDTH), new_map),
           pl.BlockSpec((8, LANES), lambda b, pt: (0, 0)),
           pl.BlockSpec((1, DA_DV), lambda b, pt: (0, 0))],
        out_specs=pl.BlockSpec((seq, DA_WIDTH), row_map),
        scratch_shapes=[pltpu.VMEM((DA_QK, n_pages * PAGE_SIZE), BF16),
                        pltpu.VMEM((n_pages * PAGE_SIZE * DA_HEADS, DA_DV), BF16)])
    return pl.pallas_call(
        functools.partial(_attn_sample_kernel, n_pages=n_pages),
        out_shape=jax.ShapeDtypeStruct((n, DA_WIDTH), F32),
        grid_spec=grid_spec, compiler_params=_params("parallel"),
        name="attn_sample")(page_table.reshape(-1), q, *([cache_kt] * n_pages),
                            *([cache_v] * n_pages), k_new, v_new, lamp, norm_g)


def _even_out_kernel(oh_ref, od_ref, x_ref, w_ref, g_ref, b_ref, o_ref):
    h = (_dot(oh_ref[...].astype(BF16), w_ref[0:HG_WIDTH, :])
         + _dot(od_ref[...].astype(BF16), w_ref[HG_WIDTH:D_MODEL, :]))
    o_ref[...] = _layer_norm(ALPHA * x_ref[...] + h, g_ref[...], b_ref[...])


def _even_out(oh, od, x, w_out_bf, g, b):
    n = x.shape[0]
    tm = _tile(n, 512)
    vec = pl.BlockSpec((1, D_MODEL), lambda i: (0, 0))
    return pl.pallas_call(
        _even_out_kernel, out_shape=jax.ShapeDtypeStruct((n, D_MODEL), F32), grid=(n // tm,),
        in_specs=[_row_spec(tm, HG_WIDTH), _row_spec(tm, DA_WIDTH), _row_spec(tm, D_MODEL),
                  _const_spec((D_MODEL, D_MODEL)), vec, vec],
        out_specs=_row_spec(tm, D_MODEL),
        compiler_params=_params("parallel"), name="even_out")(oh, od, x, w_out_bf, g, b)


FF_CHUNK = 1024


def _mlp_kernel(x_ref, p_ref, up_ref, down_ref, g_ref, b_ref, gate_ref, proj_ref, o_ref):
    x = x_ref[...]
    xb = x.astype(BF16)
    acc = jnp.zeros(x.shape, F32)
    for c in range(D_FF // FF_CHUNK):
        cs = slice(c * FF_CHUNK, (c + 1) * FF_CHUNK)
        hid = jnp.maximum(_dot(xb, up_ref[:, cs]), 0.0)
        acc = acc + _dot((hid * hid).astype(BF16), down_ref[cs, :])
    x2 = _layer_norm(ALPHA * x + acc, g_ref[...], b_ref[...])
    gate = _sigmoid(_dot(x2.astype(BF16), gate_ref[...]))
    o_ref[...] = x2 + gate * _dot(p_ref[...].astype(BF16), proj_ref[...])


def _mlp(x, p_all, layer, up_bf, down_bf, g, b, gate_bf, proj_bf):
    n = x.shape[0]
    tm = _tile(n, 512)
    vec = pl.BlockSpec((1, D_MODEL), lambda i: (0, 0))
    return pl.pallas_call(
        _mlp_kernel, out_shape=jax.ShapeDtypeStruct((n, D_MODEL), F32), grid=(n // tm,),
        in_specs=[_row_spec(tm, D_MODEL), pl.BlockSpec((None, tm, PLE_DIM), lambda i: (layer, i, 0)),
                  _const_spec((D_MODEL, D_FF)), _const_spec((D_FF, D_MODEL)), vec, vec,
                  _const_spec((D_MODEL, D_MODEL)), _const_spec((PLE_DIM, D_MODEL))],
        out_specs=_row_spec(tm, D_MODEL),
        compiler_params=_params("parallel"), name="mlp")(x, p_all, up_bf, down_bf, g, b, gate_bf,
                                                         proj_bf)


def _swap_index_with_lane_pos(zs, width):
    pos = lax.broadcasted_iota(jnp.int32, zs[0].shape, 1) % width
    s = width // 2
    while s >= 1:
        hi = (pos & s) != 0
        new = list(zs)
        for p in range(width):
            if p & s == 0:
                lo_arr, hi_arr = zs[p], zs[p + s]
                new[p] = jnp.where(hi, pltpu.roll(hi_arr, s, 1), lo_arr)
                new[p + s] = jnp.where(hi, hi_arr, pltpu.roll(lo_arr, LANES - s, 1))
        zs = new
        s //= 2
    return zs


def _tokens_to_chains(x, nseq, tt, out_ref):
    sub = LANES // nseq
    for sb in range(tt // sub):
        heads = []
        for g in range(D_MODEL // LANES):
            tile = jnp.concatenate(
                [x[b * tt + sb * sub:b * tt + (sb + 1) * sub, g * LANES:(g + 1) * LANES]
                 for b in range(nseq)], axis=0)
            tile_t = tile.T
            heads += [tile_t[0:RW_HEAD, :], tile_t[RW_HEAD:2 * RW_HEAD, :]]
        for t, w in enumerate(_swap_index_with_lane_pos(heads, sub)):
            out_ref[sb * sub + t] = w


def _rwkv_pre_kernel(*refs, has_vfirst, chains, want_v_tokens):
    if chains:
        x_ref, x8_ref, sh_ref = refs[:3]
        refs = refs[3:]
    else:
        x_ref, xp_ref = refs[:2]
        refs = refs[2:]
    vf_ref = vt_o = v0_ref = v1_ref = v2_ref = None
    if has_vfirst:
        vf_ref, refs = refs[0], refs[1:]
    (mix_ref, wr_ref, wk_ref, wv_ref, w0_ref, w1_ref, w2_ref, a0_ref, a1_ref, a2_ref) = refs[:10]
    refs = refs[10:]
    if has_vfirst:
        v0_ref, v1_ref, v2_ref = refs[:3]
        refs = refs[3:]
    g1_ref, g2_ref, r_o, k_o, v_o, g_o, dec_o, a_o = refs[:8]
    if want_v_tokens:
        vt_o = refs[8]
    if chains:
        nseq, tt = x_ref.shape[0], x_ref.shape[1]
        x = x_ref[...].reshape(nseq * tt, D_MODEL)
        first = pl.program_id(0) == 0
        before = jnp.where(first, sh_ref[...], x8_ref[:, SUBLANES - 1, :])
        before = jnp.concatenate([jnp.broadcast_to(before[b:b + 1, :], (tt, D_MODEL))
                                  for b in range(nseq)], axis=0)
        rowid = lax.broadcasted_iota(jnp.int32, x.shape, 0)
        x_prev = jnp.where(rowid % tt == 0, before, pltpu.roll(x, 1, 0))
    else:
        x = x_ref[...]
        x_prev = xp_ref[...]
    xx = x_prev - x
    xr, xw, xk, xv, xa, xg = ((x + xx * mix_ref[m:m + 1, :]).astype(BF16) for m in range(6))
    r = _dot(xr, wr_ref[...])
    k = _dot(xk, wk_ref[...])
    v = _dot(xv, wv_ref[...])
    lw = w0_ref[...] + _dot(jnp.tanh(_dot(xw, w1_ref[...])).astype(BF16), w2_ref[...])
    w = -(jnp.maximum(-lw, 0.0) + jnp.log(1.0 + jnp.exp(-jnp.abs(lw)))) - 0.5
    if has_vfirst:
        mixv = _sigmoid(v0_ref[...] + _dot(_dot(xv, v1_ref[...]).astype(BF16), v2_ref[...]))
        vf = vf_ref[...].reshape(v.shape) if chains else vf_ref[...]
        v = v + (vf - v) * mixv
    a = _sigmoid(a0_ref[...] + _dot(_dot(xa, a1_ref[...]).astype(BF16), a2_ref[...]))
    g = _dot(_sigmoid(_dot(xg, g1_ref[...])).astype(BF16), g2_ref[...])
    dec = jnp.exp(-jnp.exp(w))
    if chains:
        g_o[...] = g.reshape(g_o.shape)
        if want_v_tokens:
            vt_o[...] = v.reshape(vt_o.shape)
        for val, ref in ((r, r_o), (k, k_o), (v, v_o), (dec, dec_o), (a, a_o)):
            _tokens_to_chains(val, nseq, tt, ref)
    else:
        r_o[...] = r
        k_o[...] = k
        v_o[...] = v
        g_o[...] = g
        dec_o[...] = dec
        a_o[...] = a
        if want_v_tokens:
            vt_o[...] = v


def _chain_layout_ok(nb, seq):
    return nb * RW_HEADS == LANES and LANES // nb == RW_HEADS and seq % RW_TBLK == 0


def _rwkv_pre(x, shift0, v_first, wts, nb, seq, want_v_tokens):
    n = x.shape[0]
    has_vfirst = v_first is not None
    chains = _chain_layout_ok(nb, seq)
    vec = pl.BlockSpec((1, D_MODEL), lambda i: (0, 0))
    if chains:
        tt = RW_TBLK
        tok3 = pl.BlockSpec((nb, tt, D_MODEL), lambda i: (0, i, 0))
        x3 = x.reshape(nb, seq, D_MODEL)
        r8 = tt // SUBLANES
        acts = [x3, x3, shift0]
        act_specs = [tok3,
                     pl.BlockSpec((nb, SUBLANES, D_MODEL), lambda i: (0, jnp.maximum(i * r8 - 1, 0), 0)),
                     pl.BlockSpec((nb, D_MODEL), lambda i: (0, 0))]
        if has_vfirst:
            acts.append(v_first.reshape(nb, seq, D_MODEL))
            act_specs.append(tok3)
        grid = (seq // tt,)
        chain_out = jax.ShapeDtypeStruct((seq, RW_HEAD, LANES), F32)
        chain_spec = pl.BlockSpec((tt, RW_HEAD, LANES), lambda i: (i, 0, 0))
        tok_out = jax.ShapeDtypeStruct((nb, seq, D_MODEL), F32)
        out_shape = [chain_out, chain_out, chain_out, tok_out, chain_out, chain_out]
        out_specs = [chain_spec, chain_spec, chain_spec, tok3, chain_spec, chain_spec]
        if want_v_tokens:
            out_shape.append(tok_out)
            out_specs.append(tok3)
    else:
        tm = _tile(n, 256)
        xs = x.reshape(nb, seq, D_MODEL)
        x_prev = jnp.concatenate([shift0[:, None, :], xs[:, :-1]], axis=1).reshape(n, D_MODEL)
        acts = [x, x_prev] + ([v_first] if has_vfirst else [])
        act_specs = [_row_spec(tm, D_MODEL)] * len(acts)
        grid = (n // tm,)
        out_shape = [jax.ShapeDtypeStruct((n, D_MODEL), F32)] * 6
        out_specs = [_row_spec(tm, D_MODEL)] * 6
    names = ['mix', 'w_r', 'w_k', 'w_v', 'w0', 'w1', 'w2', 'a0', 'a1', 'a2']
    if has_vfirst:
        names += ['v0', 'v1', 'v2']
    names += ['g1', 'g2']
    ws = [wts[nm] for nm in names]
    w_specs = [vec if w.shape == (1, D_MODEL) else _const_spec(w.shape) for w in ws]
    outs = pl.pallas_call(
        functools.partial(_rwkv_pre_kernel, has_vfirst=has_vfirst, chains=chains,
                          want_v_tokens=want_v_tokens and chains),
        out_shape=tuple(out_shape), grid=grid,
        in_specs=act_specs + w_specs, out_specs=tuple(out_specs),
        compiler_params=_params("parallel"), name="rwkv_pre")(*acts, *ws)
    outs = list(outs)
    if chains:
        outs[3] = outs[3].reshape(n, D_MODEL)
        if want_v_tokens:
            outs[6] = outs[6].reshape(n, D_MODEL)
    elif want_v_tokens:
        outs.append(outs[2])
    return outs


RW_TBLK = 32


RW_IN = 5


def _rwkv_token(ins, v_rows, prm, s_s, y_s):
    r, k, v, dec, ag = ins
    kk_p, ka_p, rk_p, lg_p, lb_p = prm
    kk = k * kk_p
    kk = kk / jnp.maximum(jnp.sqrt(jnp.sum(kk * kk, axis=0, keepdims=True)), 1e-12)
    a = -kk
    b = kk * ag
    k2 = k * (1.0 + (ag - 1.0) * ka_p)
    wr = dec * r
    br = jnp.sum(b * r, axis=0, keepdims=True)
    kr = jnp.sum(k2 * r, axis=0, keepdims=True)

    def rows(i8, _):
        i0 = pl.multiple_of(i8 * SUBLANES, SUBLANES)
        v8 = v_rows(i0)
        ys = []
        for ii in range(SUBLANES):
            si = s_s[i0 + ii]
            sa = jnp.sum(si * a, axis=0, keepdims=True)
            qs = jnp.sum(si * wr, axis=0, keepdims=True)
            vi = v8[ii:ii + 1, :]
            s_s[i0 + ii] = si * dec + sa * b + vi * k2
            ys.append(qs + sa * br + vi * kr)
        y_s[pl.ds(i0, SUBLANES), :] = jnp.concatenate(ys, axis=0)
        return 0

    lax.fori_loop(0, RW_HEAD // SUBLANES, rows, 0)
    y = y_s[...]
    mu = jnp.mean(y, axis=0, keepdims=True)
    yc = y - mu
    var = jnp.mean(yc * yc, axis=0, keepdims=True)
    yn = yc * lax.rsqrt(var + GN_EPS) * lg_p + lb_p
    return yn + jnp.sum(r * k2 * rk_p, axis=0, keepdims=True) * v


RW_IBLK = 16
RW_ACC = 4


def _rwkv_token_keymajor(ins, v_rows, prm, s_s, y_s, vec_s, g_s):
    r, k, v, dec, ag = ins
    kk_p, ka_p, rk_p, lg_p, lb_p = prm
    kk = k * kk_p
    kk = kk / jnp.maximum(jnp.sqrt(jnp.sum(kk * kk, axis=0, keepdims=True)), 1e-12)
    b = kk * ag
    k2 = k * (1.0 + (ag - 1.0) * ka_p)
    br = jnp.sum(b * r, axis=0, keepdims=True)
    kr = jnp.sum(k2 * r, axis=0, keepdims=True)
    g_old = g_s[...]
    g_new = g_old * dec
    g_s[...] = g_new
    g_inv = 1.0 / g_new
    for idx, val in enumerate((-kk * g_old, r * g_new, b * g_inv, k2 * g_inv)):
        vec_s[idx] = val
    row = lambda idx, j: vec_s[idx, j:j + 1, :]

    for p in range(RW_HEAD // RW_IBLK):
        rs = pl.ds(p * RW_IBLK, RW_IBLK)
        sa = [None] * RW_ACC
        qs = [None] * RW_ACC
        for j in range(RW_HEAD):
            sj = s_s[j, rs, :]
            c = j % RW_ACC
            sa[c] = sj * row(0, j) if sa[c] is None else sa[c] + sj * row(0, j)
            qs[c] = sj * row(1, j) if qs[c] is None else qs[c] + sj * row(1, j)
        sa = functools.reduce(lambda x, y: x + y, sa)
        qs = functools.reduce(lambda x, y: x + y, qs)
        vi = v_rows(rs)
        for j in range(RW_HEAD):
            s_s[j, rs, :] = s_s[j, rs, :] + sa * row(2, j) + vi * row(3, j)
        y_s[rs, :] = qs + sa * br + vi * kr
    y = y_s[...]
    mu = jnp.mean(y, axis=0, keepdims=True)
    yc = y - mu
    var = jnp.mean(yc * yc, axis=0, keepdims=True)
    yn = yc * lax.rsqrt(var + GN_EPS) * lg_p + lb_p
    return yn + jnp.sum(r * k2 * rk_p, axis=0, keepdims=True) * v


def _rwkv_scan_kernel(r_ref, k_ref, v_ref, dec_ref, a_ref, kk_ref, ka_ref, rk_ref, lg_ref, lb_ref,
                      s0_ref, y_ref, sout_ref, s_s, y_s):
    tb = pl.program_id(1)
    in_refs = (r_ref, k_ref, v_ref, dec_ref, a_ref)
    prm_refs = (kk_ref, ka_ref, rk_ref, lg_ref, lb_ref)

    @pl.when(tb == 0)
    def _():
        s_s[...] = s0_ref[...]

    def step(t, _):
        y_ref[t] = _rwkv_token([ref[t] for ref in in_refs],
                               lambda i0: v_ref[t, pl.ds(i0, SUBLANES), :],
                               [p[...] for p in prm_refs], s_s, y_s)
        return 0

    lax.fori_loop(0, r_ref.shape[0], step, 0)

    @pl.when(tb == pl.num_programs(1) - 1)
    def _():
        sout_ref[...] = s_s[...]


def _rwkv_scan_tokens_kernel(r_ref, k_ref, v_ref, dec_ref, a_ref, kk_ref, ka_ref, rk_ref, lg_ref,
                             lb_ref, s0_ref, y_ref, sout_ref, s_s, y_s, yt_s, vec_s, g_s):
    tb = pl.program_id(1)
    nseq, tt = y_ref.shape[0], y_ref.shape[1]
    sub = LANES // nseq
    in_refs = (r_ref, k_ref, v_ref, dec_ref, a_ref)
    prm_refs = (kk_ref, ka_ref, rk_ref, lg_ref, lb_ref)

    @pl.when(tb == 0)
    def _():
        s_s[...] = s0_ref[...]
        g_s[...] = jnp.ones(g_s.shape, F32)

    def sub_block(sb, _):
        t0 = pl.multiple_of(sb * sub, sub)

        def step(t, _):
            yt_s[t] = _rwkv_token_keymajor([ref[t0 + t] for ref in in_refs],
                                           lambda rs: v_ref[t0 + t, rs, :],
                                           [p[...] for p in prm_refs], s_s, y_s, vec_s, g_s)
            return 0

        lax.fori_loop(0, sub, step, 0)
        for j in range(RW_HEAD):
            s_s[j] = s_s[j] * g_s[j:j + 1, :]
        g_s[...] = jnp.ones(g_s.shape, F32)
        heads =_swap_index_with_lane_pos([yt_s[t] for t in range(sub)], sub)
        for g in range(D_MODEL // LANES):
            tile = jnp.concatenate([heads[2 * g], heads[2 * g + 1]], axis=0).T
            for b in range(nseq):
                y_ref[b, pl.ds(t0, sub), g * LANES:(g + 1) * LANES] = tile[b * sub:(b + 1) * sub, :]
        return 0

    lax.fori_loop(0, tt // sub, sub_block, 0)

    @pl.when(tb == pl.num_programs(1) - 1)
    def _():
        sout_ref[...] = s_s[...]


def _rwkv_scan(r, k, v, dec, a, s0_all, s_layer, wts, nb, seq):
    nl = s0_all.shape[0]
    n = nb * seq
    hd = RW_HEAD
    batch_lanes = nb % LANES == 0
    if batch_lanes:
        nbg = nb // LANES
        groups, chains = RW_HEADS * nbg, LANES
        to_l = lambda x: (x.reshape(nbg, LANES, seq, RW_HEADS, hd).transpose(0, 3, 2, 4, 1)
                          .reshape(groups, seq, hd, chains))
        from_l = lambda y: (y.reshape(nbg, RW_HEADS, seq, hd, LANES).transpose(0, 4, 2, 1, 3)
                            .reshape(n, D_MODEL))
        par = lambda p: jnp.broadcast_to(p.reshape(1, RW_HEADS, hd, 1),
                                         (nbg, RW_HEADS, hd, chains)).reshape(groups, hd, chains)
        st_in = lambda s: (s.reshape(nl, nbg, LANES, RW_HEADS, hd, hd).transpose(0, 1, 3, 4, 5, 2)
                           .reshape(nl, groups, hd, hd, chains))
        st_out = lambda s: (s.reshape(nbg, RW_HEADS, hd, hd, LANES).transpose(0, 4, 1, 2, 3)
                            .reshape(nb, RW_HEADS, hd, hd))
    else:
        groups, chains = 1, nb * RW_HEADS
        to_l = lambda x: (x.reshape(nb, seq, RW_HEADS, hd).transpose(1, 3, 0, 2)
                          .reshape(1, seq, hd, chains))
        from_l = lambda y: (y.reshape(seq, hd, nb, RW_HEADS).transpose(2, 0, 3, 1)
                            .reshape(n, D_MODEL))
        par = lambda p: jnp.broadcast_to(p.reshape(RW_HEADS, hd).T.reshape(1, hd, 1, RW_HEADS),
                                         (1, hd, nb, RW_HEADS)).reshape(1, hd, chains)
        st_in = lambda s: s.transpose(0, 3, 4, 1, 2).reshape(nl, 1, hd, hd, chains)
        st_out = lambda s: s.reshape(hd, hd, nb, RW_HEADS).transpose(2, 3, 0, 1)
    tt = _tile(seq, RW_TBLK)
    prm = pl.BlockSpec((None, hd, chains), lambda g, t: (g, 0, 0))
    st = pl.BlockSpec((None, hd, hd, chains), lambda g, t: (g, 0, 0, 0))
    st0 = pl.BlockSpec((None, None, hd, hd, chains), lambda g, t: (s_layer, g, 0, 0, 0))
    prms = [par(wts[nm]) for nm in ('k_k', 'k_a', 'r_k', 'lnx_g', 'lnx_b')]
    if _chain_layout_ok(nb, seq):
        tok = pl.BlockSpec((nb, tt, D_MODEL), lambda g, t: (0, t, 0))
        act = pl.BlockSpec((tt, hd, chains), lambda g, t: (t, 0, 0))
        y, s_new = pl.pallas_call(
            _rwkv_scan_tokens_kernel,
            out_shape=(jax.ShapeDtypeStruct((nb, seq, D_MODEL), F32),
                       jax.ShapeDtypeStruct((1, hd, hd, chains), F32)),
            grid=(1, seq // tt),
            in_specs=[act] * RW_IN + [prm] * 5 + [st0],
            out_specs=(tok, st),
            scratch_shapes=[pltpu.VMEM((hd, hd, chains), F32), pltpu.VMEM((hd, chains), F32),
                            pltpu.VMEM((RW_HEADS, hd, chains), F32),
                            pltpu.VMEM((4, hd, chains), F32), pltpu.VMEM((hd, chains), F32)],
            compiler_params=_params("parallel", "arbitrary"),
            name="rwkv_scan_tokens")(r, k, v, dec, a, *prms, st_in(s0_all).swapaxes(2, 3))
        return y.reshape(n, D_MODEL), st_out(s_new.swapaxes(1, 2))
    act = pl.BlockSpec((None, tt, hd, chains), lambda g, t: (g, t, 0, 0))
    y, s_new = pl.pallas_call(
        _rwkv_scan_kernel,
        out_shape=(jax.ShapeDtypeStruct((groups, seq, hd, chains), F32),
                   jax.ShapeDtypeStruct((groups, hd, hd, chains), F32)),
        grid=(groups, seq // tt),
        in_specs=[act] * RW_IN + [prm] * 5 + [st0],
        out_specs=(act, st),
        scratch_shapes=[pltpu.VMEM((hd, hd, chains), F32), pltpu.VMEM((hd, chains), F32)],
        compiler_params=_params("parallel", "arbitrary"),
        name="rwkv_scan")(to_l(r), to_l(k), to_l(v), to_l(dec), to_l(a), *prms, st_in(s0_all))
    return from_l(y), st_out(s_new)


def _rwkv_post_kernel(y_ref, g_ref, x_ref, wo_ref, n1g_ref, n1b_ref, o_ref):
    h = _dot((y_ref[...] * g_ref[...]).astype(BF16), wo_ref[...])
    o_ref[...] = _layer_norm(ALPHA * x_ref[...] + h, n1g_ref[...], n1b_ref[...])


def _rwkv_post(y, g, x, wo_bf, n1g, n1b):
    n = x.shape[0]
    tm = _tile(n, 512)
    vec = pl.BlockSpec((1, D_MODEL), lambda i: (0, 0))
    return pl.pallas_call(
        _rwkv_post_kernel, out_shape=jax.ShapeDtypeStruct((n, D_MODEL), F32), grid=(n // tm,),
        in_specs=[_row_spec(tm, D_MODEL)] * 3 + [_const_spec((D_MODEL, D_MODEL)), vec, vec],
        out_specs=_row_spec(tm, D_MODEL),
        compiler_params=_params("parallel"), name="rwkv_post")(y, g, x, wo_bf, n1g, n1b)


def _trunk(x3, p4, hg_s0, wkv_s0, shift0, cache_k, cache_v, page_table, W):
    nb, seq, _ = x3.shape
    n = nb * seq
    x = x3.reshape(n, D_MODEL)
    new_hg, new_wkv, new_shift = [], [], []
    kv = None
    v_first = None
    for i in range(DEPTH):
        j = i // 2
        if i % 2 == 0:
            wk_t = W['wk_t'][j] if page_table is None else None
            h4, q, k_all, v_all = _even_in(x, W['w_in_even'][j], wk_t, j, kv, nb, seq)
            kv = (k_all, v_all)
            oh, s_hg = _hgrn(h4, hg_s0, min(j, hg_s0.shape[0] - 1), W['hg_lb_logits'],
                             W['hg_norm_g'][j], j, nb, seq)
            if page_table is None:
                od = _attn_prompt(q, k_all, v_all, W['lamp'][j], W['da_norm_g'][j], j, nb, seq)
            else:
                od = _attn_sample(q, k_all, v_all, cache_k, cache_v, page_table, W['lamp'][j],
                                  W['da_norm_g'][j], j, nb, seq)
            x = _even_out(oh, od, x, W['w_out_even'][j], W['ln1_g'][i], W['ln1_b'][i])
            new_hg.append(s_hg)
        else:
            need_vf = j == 0 and N_ODD > 1
            outs = _rwkv_pre(x, shift0[j], v_first, W['rw'][j], nb, seq, need_vf)
            r, k, v, g, dec, a = outs[:6]
            if need_vf:
                v_first = outs[6]
            y, s_wkv = _rwkv_scan(r, k, v, dec, a, wkv_s0, min(j, wkv_s0.shape[0] - 1),
                                  W['rw_scan'][j], nb, seq)
            new_shift.append(x.reshape(nb, seq, D_MODEL)[:, -1])
            new_wkv.append(s_wkv)
            x = _rwkv_post(y, g, x, W['rw_w_o'][j], W['ln1_g'][i], W['ln1_b'][i])
        x = _mlp(x, p4.reshape(DEPTH, n, PLE_DIM), i, W['mlp_up'][i], W['mlp_down'][i],
                 W['ln2_g'][i], W['ln2_b'][i], W['ple_gate'][i], W['ple_proj'][i])
    if page_table is None:
        k_out = kv[0].reshape(N_EVEN, nb, DA_HEADS, 2, DA_DH, seq).transpose(0, 1, 5, 2, 3, 4)
    else:
        k_out = kv[0].reshape(N_EVEN, nb, seq, DA_HEADS, 2, DA_DH)
    return (x.reshape(nb, seq, D_MODEL), k_out,
            kv[1].reshape(N_EVEN, nb, seq, DA_HEADS, DA_DV), jnp.stack(new_hg),
            jnp.stack(new_wkv), jnp.stack(new_shift))


def kernel(x_prompt, x_sample, cache_k, cache_v, state_hgrn, state_wkv, state_shift, page_table,
           p_prompt, p_sample, w_in_even, w_out_even, hg_lb_logits, hg_norm_g,
           da_lam_q1, da_lam_k1, da_lam_q2, da_lam_k2, da_norm_g,
           rw_mix, rw_w_r, rw_w_k, rw_w_v, rw_w_o, rw_w0, rw_w1, rw_w2,
           rw_a0, rw_a1, rw_a2, rw_v0, rw_v1, rw_v2, rw_g1, rw_g2,
           rw_k_k, rw_k_a, rw_r_k, rw_lnx_g, rw_lnx_b,
           ln1_g, ln1_b, ln2_g, ln2_b, mlp_up, mlp_down, ple_proj, ple_gate):
    bf = lambda w: w.astype(BF16)
    vec = lambda w: w.reshape(1, -1)

    def lam_rows(j):
        pad = lambda a: jnp.pad(a.astype(F32), (0, LANES - a.shape[0]))
        lam_init = 0.8 - 0.6 * math.exp(-0.3 * (2 * j))
        rows = [pad(da_lam_q1[j]), pad(da_lam_k1[j]), pad(da_lam_q2[j]), pad(da_lam_k2[j]),
                jnp.full((LANES,), lam_init, F32)]
        return jnp.stack(rows + [jnp.zeros((LANES,), F32)] * 3)

    rw = []
    for j in range(N_ODD):
        d = dict(mix=rw_mix[j], w_r=bf(rw_w_r[j]), w_k=bf(rw_w_k[j]), w_v=bf(rw_w_v[j]),
                 w0=vec(rw_w0[j]), w1=bf(rw_w1[j]), w2=bf(rw_w2[j]),
                 a0=vec(rw_a0[j]), a1=bf(rw_a1[j]), a2=bf(rw_a2[j]),
                 g1=bf(rw_g1[j]), g2=bf(rw_g2[j]))
        if j > 0:
            d.update(v0=vec(rw_v0[j - 1]), v1=bf(rw_v1[j - 1]), v2=bf(rw_v2[j - 1]))
        rw.append(d)
    W = dict(
        w_in_even=[bf(w_in_even[j]) for j in range(N_EVEN)],
        wk_t=[bf(w_in_even[j][:, 4 * HG_WIDTH + DA_QK:4 * HG_WIDTH + 2 * DA_QK].T) for j in range(N_EVEN)],
        w_out_even=[bf(w_out_even[j]) for j in range(N_EVEN)],
        hg_lb_logits=hg_lb_logits,
        hg_norm_g=[vec(hg_norm_g[j]) for j in range(N_EVEN)],
        da_norm_g=[vec(da_norm_g[j]) for j in range(N_EVEN)],
        lamp=[lam_rows(j) for j in range(N_EVEN)],
        rw=rw,
        rw_scan=[dict(k_k=rw_k_k[j], k_a=rw_k_a[j], r_k=rw_r_k[j], lnx_g=rw_lnx_g[j],
                      lnx_b=rw_lnx_b[j]) for j in range(N_ODD)],
        rw_w_o=[bf(rw_w_o[j]) for j in range(N_ODD)],
        ln1_g=[vec(ln1_g[i]) for i in range(DEPTH)], ln1_b=[vec(ln1_b[i]) for i in range(DEPTH)],
        ln2_g=[vec(ln2_g[i]) for i in range(DEPTH)], ln2_b=[vec(ln2_b[i]) for i in range(DEPTH)],
        mlp_up=[bf(mlp_up[i]) for i in range(DEPTH)], mlp_down=[bf(mlp_down[i]) for i in range(DEPTH)],
        ple_gate=[bf(ple_gate[i]) for i in range(DEPTH)], ple_proj=[bf(ple_proj[i]) for i in range(DEPTH)],
    )
    nbp = x_prompt.shape[0]
    dt = x_prompt.dtype
    hg0 = jnp.zeros((1, nbp, HG_HEADS, HG_DK, HG_DK), dt)
    wkv0 = jnp.zeros((1, nbp, RW_HEADS, RW_HEAD, RW_HEAD), dt)
    sh0 = jnp.zeros((N_ODD, nbp, D_MODEL), dt)
    y_p, k_p, v_p, hg_p, wkv_p, sh_p = _trunk(x_prompt, p_prompt, hg0, wkv0, sh0, None, None, None, W)
    ckt = jnp.transpose(cache_k, (0, 1, 3, 4, 5, 2)).reshape(
        cache_k.shape[0], cache_k.shape[1], DA_QK, PAGE_SIZE)
    cv = cache_v.reshape(cache_v.shape[0], cache_v.shape[1], PAGE_SIZE * DA_HEADS, DA_DV)
    y_s, k_s, v_s, hg_s, wkv_s, sh_s = _trunk(x_sample, p_sample, state_hgrn, state_wkv, state_shift,
                                              ckt, cv, page_table, W)
    return (y_p, y_s, k_p, v_p, k_s, v_s, hg_p, hg_s, wkv_p, wkv_s, sh_p, sh_s)
```

```python
import functools
import math

import jax
import jax.numpy as jnp
from jax import lax
from jax.experimental import pallas as pl
from jax.experimental.pallas import tpu as pltpu

F32 = jnp.float32
BF16 = jnp.bfloat16

D_MODEL = 1024
DEPTH = 4
PAGE_SIZE = 128
N_EVEN = (DEPTH + 1) // 2
N_ODD = DEPTH // 2
PLE_DIM = 256
D_FF = 4 * D_MODEL
HG_WIDTH = D_MODEL // 2
HG_HEADS = 4
HG_DK = HG_WIDTH // HG_HEADS
DA_WIDTH = D_MODEL - HG_WIDTH
DA_HEADS = 4
DA_DV = DA_WIDTH // DA_HEADS
DA_DH = DA_DV // 2
DA_QK = DA_HEADS * 2 * DA_DH
IN_COLS = 4 * HG_WIDTH + 2 * DA_QK + DA_WIDTH
RW_HEAD = 64
RW_HEADS = D_MODEL // RW_HEAD
ALPHA = (2.0 * DEPTH) ** 0.25
LN_EPS = 1e-5
RMS_EPS = 1e-6
GN_EPS = 64e-5
F_MIN = 1e-30
NEG_BIG = -1e30

LANES = 128
SUBLANES = 8
VMEM_LIMIT = 52 * 1024 * 1024
HG_CHUNK = 16
MM_ROWS = 16


def _params(*sem):
    return pltpu.CompilerParams(dimension_semantics=sem, vmem_limit_bytes=VMEM_LIMIT)


def _const_spec(shape):
    nd = len(shape)
    return pl.BlockSpec(shape, lambda *_: (0,) * nd, pipeline_mode=pl.Buffered(1))


def _row_spec(rows, cols):
    return pl.BlockSpec((rows, cols), lambda i: (i, 0))


def _tile(n, pref):
    t = min(n, pref)
    while n % t:
        t //= 2
    return t


def _dot(a, b):
    return jnp.dot(a, b, preferred_element_type=F32)


def _dot_nt(a, b):
    return lax.dot_general(a, b, (((1,), (1,)), ((), ())), preferred_element_type=F32)


def _dot_tn(a, b):
    return lax.dot_general(a, b, (((0,), (0,)), ((), ())), preferred_element_type=F32)


def _split3(x):
    h = x.astype(BF16)
    r = x - h.astype(F32)
    m = r.astype(BF16)
    l = (r - m.astype(F32)).astype(BF16)
    return h, m, l


def _dot_x3(x, rhs_bf16):
    h, m, l = _split3(x)
    return _dot(h, rhs_bf16) + _dot(m, rhs_bf16) + _dot(l, rhs_bf16)


def _dot_x3_lhs(lhs_bf16, x):
    h, m, l = _split3(x)
    return _dot(lhs_bf16, h) + _dot(lhs_bf16, m) + _dot(lhs_bf16, l)


def _layer_norm(z, g, b):
    mu = jnp.mean(z, axis=-1, keepdims=True)
    zc = z - mu
    var = jnp.mean(zc * zc, axis=-1, keepdims=True)
    return zc * lax.rsqrt(var + LN_EPS) * g + b


def _sigmoid(x):
    return 1.0 / (1.0 + jnp.exp(-x))


def _head_ones(width):
    r = lax.broadcasted_iota(jnp.int32, (LANES, LANES), 0) // width
    c = lax.broadcasted_iota(jnp.int32, (LANES, LANES), 1) // width
    return jnp.where(r == c, 1.0, 0.0).astype(BF16)


def _segsum(x, ones):
    cols = x.shape[1] // LANES
    parts = [_dot_x3(x[:, c * LANES:(c + 1) * LANES], ones) for c in range(cols)]
    return parts[0] if cols == 1 else jnp.concatenate(parts, axis=1)


def _even_in_kernel(x_ref, w_ref, *refs, k_transposed):
    h4_ref, q_ref, k_ref, v_ref = refs[-4:]
    xb = x_ref[...].astype(BF16)
    cw = HG_WIDTH
    for c in range(4):
        h4_ref[:, c * cw:(c + 1) * cw] = _dot(xb, w_ref[:, c * cw:(c + 1) * cw])
    q0 = 4 * HG_WIDTH
    q_ref[...] = _dot(xb, w_ref[:, q0:q0 + DA_QK])
    if k_transposed:
        k_ref[...] = _dot_nt(refs[0][...], xb)
    else:
        k_ref[...] = _dot(xb, w_ref[:, q0 + DA_QK:q0 + 2 * DA_QK])
    v_ref[...] = _dot(xb, w_ref[:, q0 + 2 * DA_QK:q0 + 2 * DA_QK + DA_WIDTH])


def _even_in(x, w_in_bf, wk_t_bf, layer, kv_stacks, nb, seq):
    n = x.shape[0]
    k_transposed = wk_t_bf is not None
    tm = _tile(seq if k_transposed else n, 512)
    slab =lambda width: pl.BlockSpec((None, tm, width), lambda i: (layer, i, 0))
    in_specs = [_row_spec(tm, D_MODEL), _const_spec((D_MODEL, IN_COLS))]
    args = [x, w_in_bf]
    if k_transposed:
        tps = seq // tm
        k_shape = (N_EVEN, nb, DA_QK, seq)
        k_spec = pl.BlockSpec((None, None, DA_QK, tm), lambda i: (layer, i // tps, 0, i % tps))
        in_specs.append(_const_spec((DA_QK, D_MODEL)))
        args.append(wk_t_bf)
    else:
        k_shape = (N_EVEN, n, DA_QK)
        k_spec = slab(DA_QK)
    outs = (jax.ShapeDtypeStruct((n, 4 * HG_WIDTH), F32), jax.ShapeDtypeStruct((n, DA_QK), F32),
            jax.ShapeDtypeStruct(k_shape, F32), jax.ShapeDtypeStruct((N_EVEN, n, DA_WIDTH), F32))
    if kv_stacks is None:
        kv_stacks = (jnp.zeros(k_shape, F32), jnp.zeros((N_EVEN, n, DA_WIDTH), F32))
    aliases = {len(args): 2, len(args) + 1: 3}
    in_specs += [pl.BlockSpec(memory_space=pl.ANY)] * 2
    args += list(kv_stacks)
    return pl.pallas_call(
        functools.partial(_even_in_kernel, k_transposed=k_transposed), out_shape=outs,
        grid=(n // tm,), in_specs=in_specs,
        out_specs=(_row_spec(tm, 4 * HG_WIDTH), _row_spec(tm, DA_QK), k_spec, slab(DA_WIDTH)),
        input_output_aliases=aliases,
        compiler_params=_params("parallel"), name="even_in")(*args)


def _pad_rows(x, rows):
    if x.shape[0] >= rows:
        return x
    return jnp.concatenate([x, jnp.zeros((rows - x.shape[0], x.shape[1]), x.dtype)], axis=0)


def _hgrn_kernel(h4_ref, s0_ref, lbl_ref, ng_ref, o_ref, sout_ref,
                 st_ref, q_s, k_s, b_s, u_s, *, layer, chunk, carry):
    tl = h4_ref.shape[0]
    nchunk = tl // chunk
    w = HG_WIDTH

    rows = [lbl_ref[i:i + 1, :] for i in range(N_EVEN)]
    mx = functools.reduce(jnp.maximum, rows)
    es = [jnp.exp(r - mx) for r in rows]
    den = functools.reduce(lambda a, b: a + b, es)
    sm = [e / den for e in es]
    cs = functools.reduce(lambda a, b: a + b, sm[:layer + 1])
    lb = cs - sm[0]

    hq = h4_ref[:, 0:w]
    hf = h4_ref[:, w:2 * w]
    q_s[...] = hq * _sigmoid(hq)
    f = lb + (1.0 - lb) * _sigmoid(hf)
    g = jnp.log(jnp.maximum(f, F_MIN))
    k_s[...] = 1.0 - f
    ri = lax.broadcasted_iota(jnp.int32, (tl, tl), 0)
    ci = lax.broadcasted_iota(jnp.int32, (tl, tl), 1)
    tril = jnp.where((ri // chunk == ci // chunk) & (ci <= ri), 1.0, 0.0).astype(BF16)
    b_s[...] = _dot_x3_lhs(tril, g)

    if carry:
        @pl.when(pl.program_id(1) == 0)
        def _():
            for h in range(HG_HEADS):
                st_ref[h] = s0_ref[0, h].T

    rowid = lax.broadcasted_iota(jnp.int32, (chunk, 1), 0)
    ng = ng_ref[...]

    def chunk_slices(c, h):
        rs = pl.ds(pl.multiple_of(c * chunk, chunk), chunk)
        cs_ = slice(h * HG_DK, (h + 1) * HG_DK)
        v_cols = slice(2 * w + h * HG_DK, 2 * w + (h + 1) * HG_DK)
        return rs, cs_, v_cols

    def state_free(c, _):
        for h in range(HG_HEADS):
            rs, cs_, v_cols = chunk_slices(c, h)
            q, k, b, v = q_s[rs, cs_], k_s[rs, cs_], b_s[rs, cs_], h4_ref[rs, v_cols]
            groups = [(g0, min(g0 + SUBLANES, chunk)) for g0 in range(0, chunk, SUBLANES)]
            o_g = [None] * len(groups)
            for s in range(chunk):
                for gi, (g0, g1) in enumerate(groups):
                    if g1 <= s:
                        continue
                    rel = b[g0:g1, :] - b[s:s + 1, :]
                    if g0 > s:
                        e = jnp.exp(rel)
                    else:
                        keep = rowid[g0:g1, :] >= s
                        e = jnp.where(keep, jnp.exp(jnp.where(keep, rel, 0.0)), 0.0)
                    col = jnp.sum(q[g0:g1, :] * e * k[s:s + 1, :], axis=-1, keepdims=True)
                    term = col * v[s:s + 1, :]
                    o_g[gi] = term if o_g[gi] is None else o_g[gi] + term
            o_ref[rs, cs_] = o_g[0] if len(o_g) == 1 else jnp.concatenate(o_g, axis=0)
            kd = _pad_rows((k * jnp.exp(b[chunk - 1:chunk, :] - b)).astype(BF16), MM_ROWS)
            vd = _pad_rows(v.astype(BF16), MM_ROWS)
            u_s[c * HG_HEADS + h] = _dot_tn(vd, kd)
        return 0

    def with_state(c, _):
        for h in range(HG_HEADS):
            rs, cs_, _ = chunk_slices(c, h)
            q, b = q_s[rs, cs_], b_s[rs, cs_]
            hg = h4_ref[rs, 3 * w + h * HG_DK:3 * w + (h + 1) * HG_DK]
            st = st_ref[h] if carry else s0_ref[c, h].T
            st_new = st * jnp.exp(b[chunk - 1:chunk, :]) + u_s[c * HG_HEADS + h]
            if carry:
                st_ref[h] = st_new
            else:
                sout_ref[c, h] = st_new.T
            qd = _pad_rows((q * jnp.exp(b)).astype(BF16), MM_ROWS)
            o = o_ref[rs, cs_] + _dot_nt(qd, st.astype(BF16))[:chunk]
            on = o * lax.rsqrt(jnp.mean(o * o, axis=-1, keepdims=True) + RMS_EPS) * ng
            o_ref[rs, cs_] = on * _sigmoid(hg)
        return 0

    lax.fori_loop(0, nchunk, state_free, 0, unroll=2)
    lax.fori_loop(0, nchunk, with_state, 0, unroll=True)

    if carry:
        @pl.when(pl.program_id(1) == pl.num_programs(1) - 1)
        def _():
            for h in range(HG_HEADS):
                sout_ref[0, h] = st_ref[h].T


def _hgrn(h4, s0_all, s_layer, lb_logits, norm_g, layer, nb, seq):
    n = h4.shape[0]
    chunk = math.gcd(seq, HG_CHUNK)
    carry = seq > chunk
    if carry:
        tl = _tile(seq, 256)
        grid = (nb, seq // tl)
        row_map = lambda b, l: (b * (seq // tl) + l, 0)
        st_map = lambda b, l: (b, 0, 0, 0)
        st_blk = (1, HG_HEADS, HG_DK, HG_DK)
        sem = ("parallel", "arbitrary")
    else:
        bt = _tile(nb, 8)
        tl = bt * seq
        grid = (nb // bt, 1)
        row_map = lambda b, l: (b, 0)
        st_map = lambda b, l: (b, 0, 0, 0)
        st_blk = (bt, HG_HEADS, HG_DK, HG_DK)
        sem = ("parallel", "arbitrary")
    kern = functools.partial(_hgrn_kernel, layer=layer, chunk=chunk, carry=carry)
    return pl.pallas_call(
        kern,
        out_shape=(jax.ShapeDtypeStruct((n, HG_WIDTH), F32),
                   jax.ShapeDtypeStruct((nb, HG_HEADS, HG_DK, HG_DK), F32)),
        grid=grid,
        in_specs=[pl.BlockSpec((tl, 4 * HG_WIDTH), row_map),
                  pl.BlockSpec((None,) + st_blk, lambda b, l: (s_layer, b, 0, 0, 0)),
                  pl.BlockSpec((N_EVEN, HG_WIDTH), lambda b, l: (0, 0)),
                  pl.BlockSpec((1, HG_DK), lambda b, l: (0, 0))],
        out_specs=(pl.BlockSpec((tl, HG_WIDTH), row_map), pl.BlockSpec(st_blk, st_map)),
        scratch_shapes=[pltpu.VMEM((HG_HEADS, HG_DK, HG_DK), F32),
                        pltpu.VMEM((tl, HG_WIDTH), F32), pltpu.VMEM((tl, HG_WIDTH), F32),
                        pltpu.VMEM((tl, HG_WIDTH), F32),
                        pltpu.VMEM((tl // chunk * HG_HEADS, HG_DK, HG_DK), F32)],
        compiler_params=_params(*sem), name="hgrn2")(h4, s0_all, lb_logits, norm_g)


def _lambda(lamp_ref):
    lam_init = lamp_ref[4:5, 0:1]
    l1 = jnp.sum(lamp_ref[0:1, :] * lamp_ref[1:2, :], axis=-1, keepdims=True)
    l2 = jnp.sum(lamp_ref[2:3, :] * lamp_ref[3:4, :], axis=-1, keepdims=True)
    return jnp.exp(l1) - jnp.exp(l2) + lam_init, lam_init


def _diff_finish(o0, o1, lam, lam_init, ng):
    o = o0 - lam * o1
    return o * lax.rsqrt(jnp.mean(o * o, axis=-1, keepdims=True) + RMS_EPS) * ng * (1.0 - lam_init)


def _lane_fold(x, op):
    parts = [x[:, c * LANES:(c + 1) * LANES] for c in range(x.shape[1] // LANES)]
    return functools.reduce(op, parts)


def _attn_prompt_kernel(q_ref, k_ref, v_ref, lamp_ref, ng_ref, o_ref, kb_s, vb_s, s_s, mp_s, lp_s,
                        acc_s):
    tq = q_ref.shape[0]
    i = pl.program_id(2)

    @pl.when(i == 0)
    def _():
        for jb in range(kb_s.shape[0]):
            kb_s[jb] = k_ref[:, jb * tq:(jb + 1) * tq].astype(BF16)
        vb_s[...] = v_ref[...].astype(BF16)

    lane = lax.broadcasted_iota(jnp.int32, (tq, DA_DV), 1)
    qs = q_ref[...] * (DA_DH ** -0.5)
    q2 = jnp.concatenate([jnp.where(lane < DA_DH, qs, 0.0), jnp.where(lane >= DA_DH, qs, 0.0)],
                         axis=0).astype(BF16)
    mp_s[...] = jnp.full(mp_s.shape, NEG_BIG, F32)

    def scores(j, masked):
        s = _dot(q2, kb_s[j])
        if masked:
            row = lax.broadcasted_iota(jnp.int32, (2 * tq, tq), 0) % tq
            col = lax.broadcasted_iota(jnp.int32, (2 * tq, tq), 1)
            s = jnp.where(col <= row, s, NEG_BIG)
        s_s[j] = s
        mp_s[...] = jnp.maximum(mp_s[...], _lane_fold(s, jnp.maximum))

    def pass1(j, c):
        scores(j, False)
        return c

    lax.fori_loop(0, i, pass1, 0)
    scores(i, True)
    m = jnp.max(mp_s[...], axis=-1, keepdims=True)
    lp_s[...] = jnp.zeros(lp_s.shape, F32)
    acc_s[...] = jnp.zeros(acc_s.shape, F32)

    def pass2(j, c):
        r0 = pl.multiple_of(j * tq, tq)
        p = jnp.exp(s_s[j] - m)
        lp_s[...] += _lane_fold(p, lambda a, b: a + b)
        acc_s[...] += _dot(p.astype(BF16), vb_s[pl.ds(r0, tq), :])
        return c

    lax.fori_loop(0, i + 1, pass2, 0)
    l = jnp.sum(lp_s[...], axis=-1, keepdims=True)
    lam, lam_init = _lambda(lamp_ref)
    o0 = acc_s[0:tq, :] / l[0:tq, :]
    o1 = acc_s[tq:2 * tq, :] / l[tq:2 * tq, :]
    o_ref[...] = _diff_finish(o0, o1, lam, lam_init, ng_ref[...])


def _attn_prompt(q, k_all, v_all, lamp, norm_g, layer, nb, seq):
    n = q.shape[0]
    tq = _tile(seq, 512)
    nq = seq // tq
    return pl.pallas_call(
        _attn_prompt_kernel, out_shape=jax.ShapeDtypeStruct((n, DA_WIDTH), F32),
        grid=(nb, DA_HEADS, nq),
        in_specs=[pl.BlockSpec((tq, DA_DV), lambda b, h, i: (b * nq + i, h)),
                  pl.BlockSpec((None, None, DA_DV, seq), lambda b, h, i: (layer, b, h, 0)),
                  pl.BlockSpec((None, seq, DA_DV), lambda b, h, i: (layer, b, h)),
                  pl.BlockSpec((8, LANES), lambda b, h, i: (0, 0)),
                  pl.BlockSpec((1, DA_DV), lambda b, h, i: (0, 0))],
        out_specs=pl.BlockSpec((tq, DA_DV), lambda b, h, i: (b * nq + i, h)),
        scratch_shapes=[pltpu.VMEM((nq, DA_DV, tq), BF16), pltpu.VMEM((seq, DA_DV), BF16),
                        pltpu.VMEM((nq, 2 * tq, tq), F32), pltpu.VMEM((2 * tq, LANES), F32),
                        pltpu.VMEM((2 * tq, LANES), F32), pltpu.VMEM((2 * tq, DA_DV), F32)],
        compiler_params=_params("parallel", "parallel", "arbitrary"),
        name="attn_prompt")(q, k_all, v_all, lamp, norm_g)


def _attn_sample_kernel(pt_ref, q_ref, *refs, n_pages):
    kp = refs[:n_pages]
    vp = refs[n_pages:2 * n_pages]
    kn_ref, vn_ref, lamp_ref, ng_ref, o_ref, kt_s, v_s = refs[2 * n_pages:]
    t = q_ref.shape[0]
    nrow = 2 * DA_HEADS * t
    hrow = 2 * t

    qs = q_ref[...] * (DA_DH ** -0.5)
    qt = jnp.concatenate([qs] * (2 * DA_HEADS), axis=0)
    r = lax.broadcasted_iota(jnp.int32, (nrow, DA_QK), 0) // t
    c = lax.broadcasted_iota(jnp.int32, (nrow, DA_QK), 1) // DA_DH
    qbd = jnp.where(r == c, qt, 0.0).astype(BF16)

    pc = PAGE_SIZE * DA_HEADS
    for i in range(n_pages):
        kt_s[:, i * PAGE_SIZE:(i + 1) * PAGE_SIZE] = kp[i][...].astype(BF16)
        v_s[i * pc:(i + 1) * pc, :] = vp[i][...].astype(BF16)
    s_old = _dot(qbd, kt_s[...])
    kb = _pad_rows(kn_ref[...], PAGE_SIZE).astype(BF16)
    vb = _pad_rows(vn_ref[...], PAGE_SIZE).astype(BF16)
    r = lax.broadcasted_iota(jnp.int32, (nrow, PAGE_SIZE), 0) % t
    c = lax.broadcasted_iota(jnp.int32, (nrow, PAGE_SIZE), 1)
    s_new = jnp.where(c <= r, _dot_nt(qbd, kb), NEG_BIG)
    m = jnp.maximum(jnp.max(s_old, axis=-1, keepdims=True), jnp.max(s_new, axis=-1, keepdims=True))
    p_old = jnp.exp(s_old - m)
    p_new = jnp.exp(s_new - m)
    l = jnp.sum(p_old, axis=-1, keepdims=True) + jnp.sum(p_new, axis=-1, keepdims=True)
    pb_old = p_old.astype(BF16)
    pb_new = p_new.astype(BF16)

    spread = jnp.where(lax.broadcasted_iota(jnp.int32, (PAGE_SIZE, pc), 1) // DA_HEADS
                       == lax.broadcasted_iota(jnp.int32, (PAGE_SIZE, pc), 0), 1.0, 0.0).astype(BF16)
    own = (lax.broadcasted_iota(jnp.int32, (n_pages * nrow, pc), 1) % DA_HEADS
           == (lax.broadcasted_iota(jnp.int32, (n_pages * nrow, pc), 0) % nrow) // hrow)
    p_rows = jnp.concatenate([pb_old[:, i * PAGE_SIZE:(i + 1) * PAGE_SIZE] for i in range(n_pages)],
                             axis=0)
    pe = jnp.where(own, _dot(p_rows, spread), 0.0).astype(BF16)
    pe = jnp.concatenate([pe[i * nrow:(i + 1) * nrow, :] for i in range(n_pages)], axis=1)
    acc = _dot(pe, v_s[...])
    acc = acc + jnp.concatenate([_dot(pb_new[h * hrow:(h + 1) * hrow, :], vb[:, h * DA_DV:(h + 1) * DA_DV])
                                 for h in range(DA_HEADS)], axis=0)
    o = acc / l
    lam, lam_init = _lambda(lamp_ref)
    ng = ng_ref[...]
    for h in range(DA_HEADS):
        r0 = h * hrow
        o_ref[:, h * DA_DV:(h + 1) * DA_DV] = _diff_finish(o[r0:r0 + t, :], o[r0 + t:r0 + 2 * t, :],
                                                           lam, lam_init, ng)


def _attn_sample(q, k_new, v_new, cache_kt, cache_v, page_table, lamp, norm_g, layer, nb, seq):
    n = q.shape[0]
    n_pages = page_table.shape[1]
    row_map = lambda b, pt: (b, 0)
    new_map = lambda b, pt: (layer, b, 0)
    k_specs = [pl.BlockSpec((None, None, DA_QK, PAGE_SIZE),
                            lambda b, pt, i=i: (layer, pt[b * n_pages + i], 0, 0))
               for i in range(n_pages)]
    v_specs = [pl.BlockSpec((None, None, PAGE_SIZE * DA_HEADS, DA_DV),
                            lambda b, pt, i=i: (layer, pt[b * n_pages + i], 0, 0))
               for i in range(n_pages)]
    grid_spec = pltpu.PrefetchScalarGridSpec(
        num_scalar_prefetch=1, grid=(nb,),
        in_specs=[pl.BlockSpec((seq, DA_QK), row_map)] + k_specs + v_specs
        + [pl.BlockSpec((None, seq, DA_QK), new_map), pl.BlockSpec((None, seq, DA_WIDTH), new_map),
           pl.BlockSpec((8, LANES), lambda b, pt: (0, 0)),
           pl.BlockSpec((1, DA_DV), lambda b, pt: (0, 0))],
        out_specs=pl.BlockSpec((seq, DA_WIDTH), row_map),
        scratch_shapes=[pltpu.VMEM((DA_QK, n_pages * PAGE_SIZE), BF16),
                        pltpu.VMEM((n_pages * PAGE_SIZE * DA_HEADS, DA_DV), BF16)])
    return pl.pallas_call(
        functools.partial(_attn_sample_kernel, n_pages=n_pages),
        out_shape=jax.ShapeDtypeStruct((n, DA_WIDTH), F32),
        grid_spec=grid_spec, compiler_params=_params("parallel"),
        name="attn_sample")(page_table.reshape(-1), q, *([cache_kt] * n_pages),
                            *([cache_v] * n_pages), k_new, v_new, lamp, norm_g)


def _even_out_kernel(oh_ref, od_ref, x_ref, w_ref, g_ref, b_ref, o_ref):
    h = (_dot(oh_ref[...].astype(BF16), w_ref[0:HG_WIDTH, :])
         + _dot(od_ref[...].astype(BF16), w_ref[HG_WIDTH:D_MODEL, :]))
    o_ref[...] = _layer_norm(ALPHA * x_ref[...] + h, g_ref[...], b_ref[...])


def _even_out(oh, od, x, w_out_bf, g, b):
    n = x.shape[0]
    tm = _tile(n, 512)
    vec = pl.BlockSpec((1, D_MODEL), lambda i: (0, 0))
    return pl.pallas_call(
        _even_out_kernel, out_shape=jax.ShapeDtypeStruct((n, D_MODEL), F32), grid=(n // tm,),
        in_specs=[_row_spec(tm, HG_WIDTH), _row_spec(tm, DA_WIDTH), _row_spec(tm, D_MODEL),
                  _const_spec((D_MODEL, D_MODEL)), vec, vec],
        out_specs=_row_spec(tm, D_MODEL),
        compiler_params=_params("parallel"), name="even_out")(oh, od, x, w_out_bf, g, b)


FF_CHUNK = 1024


def _mlp_kernel(x_ref, p_ref, up_ref, down_ref, g_ref, b_ref, gate_ref, proj_ref, o_ref):
    x = x_ref[...]
    xb = x.astype(BF16)
    acc = jnp.zeros(x.shape, F32)
    for c in range(D_FF // FF_CHUNK):
        cs = slice(c * FF_CHUNK, (c + 1) * FF_CHUNK)
        hid = jnp.maximum(_dot(xb, up_ref[:, cs]), 0.0)
        acc = acc + _dot((hid * hid).astype(BF16), down_ref[cs, :])
    x2 = _layer_norm(ALPHA * x + acc, g_ref[...], b_ref[...])
    gate = _sigmoid(_dot(x2.astype(BF16), gate_ref[...]))
    o_ref[...] = x2 + gate * _dot(p_ref[...].astype(BF16), proj_ref[...])


def _mlp(x, p_all, layer, up_bf, down_bf, g, b, gate_bf, proj_bf):
    n = x.shape[0]
    tm = _tile(n, 512)
    vec = pl.BlockSpec((1, D_MODEL), lambda i: (0, 0))
    return pl.pallas_call(
        _mlp_kernel, out_shape=jax.ShapeDtypeStruct((n, D_MODEL), F32), grid=(n // tm,),
        in_specs=[_row_spec(tm, D_MODEL), pl.BlockSpec((None, tm, PLE_DIM), lambda i: (layer, i, 0)),
                  _const_spec((D_MODEL, D_FF)), _const_spec((D_FF, D_MODEL)), vec, vec,
                  _const_spec((D_MODEL, D_MODEL)), _const_spec((PLE_DIM, D_MODEL))],
        out_specs=_row_spec(tm, D_MODEL),
        compiler_params=_params("parallel"), name="mlp")(x, p_all, up_bf, down_bf, g, b, gate_bf,
                                                         proj_bf)


def _swap_index_with_lane_pos(zs, width):
    pos = lax.broadcasted_iota(jnp.int32, zs[0].shape, 1) % width
    s = width // 2
    while s >= 1:
        hi = (pos & s) != 0
        new = list(zs)
        for p in range(width):
            if p & s == 0:
                lo_arr, hi_arr = zs[p], zs[p + s]
                new[p] = jnp.where(hi, pltpu.roll(hi_arr, s, 1), lo_arr)
                new[p + s] = jnp.where(hi, hi_arr, pltpu.roll(lo_arr, LANES - s, 1))
        zs = new
        s //= 2
    return zs


def _tokens_to_chains(x, nseq, tt, out_ref):
    sub = LANES // nseq
    for sb in range(tt // sub):
        heads = []
        for g in range(D_MODEL // LANES):
            tile = jnp.concatenate(
                [x[b * tt + sb * sub:b * tt + (sb + 1) * sub, g * LANES:(g + 1) * LANES]
                 for b in range(nseq)], axis=0)
            tile_t = tile.T
            heads += [tile_t[0:RW_HEAD, :], tile_t[RW_HEAD:2 * RW_HEAD, :]]
        for t, w in enumerate(_swap_index_with_lane_pos(heads, sub)):
            out_ref[sb * sub + t] = w


def _rwkv_pre_kernel(*refs, has_vfirst, chains, want_v_tokens):
    if chains:
        x_ref, x8_ref, sh_ref = refs[:3]
        refs = refs[3:]
    else:
        x_ref, xp_ref = refs[:2]
        refs = refs[2:]
    vf_ref = vt_o = v0_ref = v1_ref = v2_ref = None
    if has_vfirst:
        vf_ref, refs = refs[0], refs[1:]
    (mix_ref, wr_ref, wk_ref, wv_ref, w0_ref, w1_ref, w2_ref, a0_ref, a1_ref, a2_ref) = refs[:10]
    refs = refs[10:]
    if has_vfirst:
        v0_ref, v1_ref, v2_ref = refs[:3]
        refs = refs[3:]
    g1_ref, g2_ref, r_o, k_o, v_o, g_o, dec_o, a_o = refs[:8]
    if want_v_tokens:
        vt_o = refs[8]
    if chains:
        nseq, tt = x_ref.shape[0], x_ref.shape[1]
        x = x_ref[...].reshape(nseq * tt, D_MODEL)
        first = pl.program_id(0) == 0
        before = jnp.where(first, sh_ref[...], x8_ref[:, SUBLANES - 1, :])
        before = jnp.concatenate([jnp.broadcast_to(before[b:b + 1, :], (tt, D_MODEL))
                                  for b in range(nseq)], axis=0)
        rowid = lax.broadcasted_iota(jnp.int32, x.shape, 0)
        x_prev = jnp.where(rowid % tt == 0, before, pltpu.roll(x, 1, 0))
    else:
        x = x_ref[...]
        x_prev = xp_ref[...]
    xx = x_prev - x
    xr, xw, xk, xv, xa, xg = ((x + xx * mix_ref[m:m + 1, :]).astype(BF16) for m in range(6))
    r = _dot(xr, wr_ref[...])
    k = _dot(xk, wk_ref[...])
    v = _dot(xv, wv_ref[...])
    lw = w0_ref[...] + _dot(jnp.tanh(_dot(xw, w1_ref[...])).astype(BF16), w2_ref[...])
    w = -(jnp.maximum(-lw, 0.0) + jnp.log(1.0 + jnp.exp(-jnp.abs(lw)))) - 0.5
    if has_vfirst:
        mixv = _sigmoid(v0_ref[...] + _dot(_dot(xv, v1_ref[...]).astype(BF16), v2_ref[...]))
        vf = vf_ref[...].reshape(v.shape) if chains else vf_ref[...]
        v = v + (vf - v) * mixv
    a = _sigmoid(a0_ref[...] + _dot(_dot(xa, a1_ref[...]).astype(BF16), a2_ref[...]))
    g = _dot(_sigmoid(_dot(xg, g1_ref[...])).astype(BF16), g2_ref[...])
    dec = jnp.exp(-jnp.exp(w))
    if chains:
        g_o[...] = g.reshape(g_o.shape)
        if want_v_tokens:
            vt_o[...] = v.reshape(vt_o.shape)
        for val, ref in ((r, r_o), (k, k_o), (v, v_o), (dec, dec_o), (a, a_o)):
            _tokens_to_chains(val, nseq, tt, ref)
    else:
        r_o[...] = r
        k_o[...] = k
        v_o[...] = v
        g_o[...] = g
        dec_o[...] = dec
        a_o[...] = a
        if want_v_tokens:
            vt_o[...] = v


def _chain_layout_ok(nb, seq):
    return nb * RW_HEADS == LANES and LANES // nb == RW_HEADS and seq % RW_TBLK == 0


def _rwkv_pre(x, shift0, v_first, wts, nb, seq, want_v_tokens):
    n = x.shape[0]
    has_vfirst = v_first is not None
    chains = _chain_layout_ok(nb, seq)
    vec = pl.BlockSpec((1, D_MODEL), lambda i: (0, 0))
    if chains:
        tt = RW_TBLK
        tok3 = pl.BlockSpec((nb, tt, D_MODEL), lambda i: (0, i, 0))
        x3 = x.reshape(nb, seq, D_MODEL)
        r8 = tt // SUBLANES
        acts = [x3, x3, shift0]
        act_specs = [tok3,
                     pl.BlockSpec((nb, SUBLANES, D_MODEL), lambda i: (0, jnp.maximum(i * r8 - 1, 0), 0)),
                     pl.BlockSpec((nb, D_MODEL), lambda i: (0, 0))]
        if has_vfirst:
            acts.append(v_first.reshape(nb, seq, D_MODEL))
            act_specs.append(tok3)
        grid = (seq // tt,)
        chain_out = jax.ShapeDtypeStruct((seq, RW_HEAD, LANES), F32)
        chain_spec = pl.BlockSpec((tt, RW_HEAD, LANES), lambda i: (i, 0, 0))
        tok_out = jax.ShapeDtypeStruct((nb, seq, D_MODEL), F32)
        out_shape = [chain_out, chain_out, chain_out, tok_out, chain_out, chain_out]
        out_specs = [chain_spec, chain_spec, chain_spec, tok3, chain_spec, chain_spec]
        if want_v_tokens:
            out_shape.append(tok_out)
            out_specs.append(tok3)
    else:
        tm = _tile(n, 256)
        xs = x.reshape(nb, seq, D_MODEL)
        x_prev = jnp.concatenate([shift0[:, None, :], xs[:, :-1]], axis=1).reshape(n, D_MODEL)
        acts = [x, x_prev] + ([v_first] if has_vfirst else [])
        act_specs = [_row_spec(tm, D_MODEL)] * len(acts)
        grid = (n // tm,)
        out_shape = [jax.ShapeDtypeStruct((n, D_MODEL), F32)] * 6
        out_specs = [_row_spec(tm, D_MODEL)] * 6
    names = ['mix', 'w_r', 'w_k', 'w_v', 'w0', 'w1', 'w2', 'a0', 'a1', 'a2']
    if has_vfirst:
        names += ['v0', 'v1', 'v2']
    names += ['g1', 'g2']
    ws = [wts[nm] for nm in names]
    w_specs = [vec if w.shape == (1, D_MODEL) else _const_spec(w.shape) for w in ws]
    outs = pl.pallas_call(
        functools.partial(_rwkv_pre_kernel, has_vfirst=has_vfirst, chains=chains,
                          want_v_tokens=want_v_tokens and chains),
        out_shape=tuple(out_shape), grid=grid,
        in_specs=act_specs + w_specs, out_specs=tuple(out_specs),
        compiler_params=_params("parallel"), name="rwkv_pre")(*acts, *ws)
    outs = list(outs)
    if chains:
        outs[3] = outs[3].reshape(n, D_MODEL)
        if want_v_tokens:
            outs[6] = outs[6].reshape(n, D_MODEL)
    elif want_v_tokens:
        outs.append(outs[2])
    return outs


RW_TBLK = 32


RW_IN = 5


def _rwkv_token(ins, v_rows, prm, s_s, y_s):
    r, k, v, dec, ag = ins
    kk_p, ka_p, rk_p, lg_p, lb_p = prm
    kk = k * kk_p
    kk = kk / jnp.maximum(jnp.sqrt(jnp.sum(kk * kk, axis=0, keepdims=True)), 1e-12)
    a = -kk
    b = kk * ag
    k2 = k * (1.0 + (ag - 1.0) * ka_p)
    wr = dec * r
    br = jnp.sum(b * r, axis=0, keepdims=True)
    kr = jnp.sum(k2 * r, axis=0, keepdims=True)

    def rows(i8, _):
        i0 = pl.multiple_of(i8 * SUBLANES, SUBLANES)
        v8 = v_rows(i0)
        ys = []
        for ii in range(SUBLANES):
            si = s_s[i0 + ii]
            sa = jnp.sum(si * a, axis=0, keepdims=True)
            qs = jnp.sum(si * wr, axis=0, keepdims=True)
            vi = v8[ii:ii + 1, :]
            s_s[i0 + ii] = si * dec + sa * b + vi * k2
            ys.append(qs + sa * br + vi * kr)
        y_s[pl.ds(i0, SUBLANES), :] = jnp.concatenate(ys, axis=0)
        return 0

    lax.fori_loop(0, RW_HEAD // SUBLANES, rows, 0)
    y = y_s[...]
    mu = jnp.mean(y, axis=0, keepdims=True)
    yc = y - mu
    var = jnp.mean(yc * yc, axis=0, keepdims=True)
    yn = yc * lax.rsqrt(var + GN_EPS) * lg_p + lb_p
    return yn + jnp.sum(r * k2 * rk_p, axis=0, keepdims=True) * v


RW_IBLK = 32
RW_ACC = 2


def _rwkv_token_keymajor(ins, v_rows, prm, s_s, y_s, vec_s, g_s):
    r, k, v, dec, ag = ins
    kk_p, ka_p, rk_p, lg_p, lb_p = prm
    kk = k * kk_p
    kk = kk / jnp.maximum(jnp.sqrt(jnp.sum(kk * kk, axis=0, keepdims=True)), 1e-12)
    b = kk * ag
    k2 = k * (1.0 + (ag - 1.0) * ka_p)
    br = jnp.sum(b * r, axis=0, keepdims=True)
    kr = jnp.sum(k2 * r, axis=0, keepdims=True)
    g_old = g_s[...]
    g_new = g_old * dec
    g_s[...] = g_new
    g_inv = 1.0 / g_new
    for idx, val in enumerate((-kk * g_old, r * g_new, b * g_inv, k2 * g_inv)):
        vec_s[idx] = val
    row = lambda idx, j: vec_s[idx, j:j + 1, :]

    for p in range(RW_HEAD // RW_IBLK):
        rs = pl.ds(p * RW_IBLK, RW_IBLK)
        sa = [None] * RW_ACC
        qs = [None] * RW_ACC
        for j in range(RW_HEAD):
            sj = s_s[j, rs, :]
            c = j % RW_ACC
            sa[c] = sj * row(0, j) if sa[c] is None else sa[c] + sj * row(0, j)
            qs[c] = sj * row(1, j) if qs[c] is None else qs[c] + sj * row(1, j)
        sa = functools.reduce(lambda x, y: x + y, sa)
        qs = functools.reduce(lambda x, y: x + y, qs)
        vi = v_rows(rs)
        for j in range(RW_HEAD):
            s_s[j, rs, :] = s_s[j, rs, :] + sa * row(2, j) + vi * row(3, j)
        y_s[rs, :] = qs + sa * br + vi * kr
    y = y_s[...]
    mu = jnp.mean(y, axis=0, keepdims=True)
    yc = y - mu
    var = jnp.mean(yc * yc, axis=0, keepdims=True)
    yn = yc * lax.rsqrt(var + GN_EPS) * lg_p + lb_p
    return yn + jnp.sum(r * k2 * rk_p, axis=0, keepdims=True) * v


def _rwkv_scan_kernel(r_ref, k_ref, v_ref, dec_ref, a_ref, kk_ref, ka_ref, rk_ref, lg_ref, lb_ref,
                      s0_ref, y_ref, sout_ref, s_s, y_s):
    tb = pl.program_id(1)
    in_refs = (r_ref, k_ref, v_ref, dec_ref, a_ref)
    prm_refs = (kk_ref, ka_ref, rk_ref, lg_ref, lb_ref)

    @pl.when(tb == 0)
    def _():
        s_s[...] = s0_ref[...]

    def step(t, _):
        y_ref[t] = _rwkv_token([ref[t] for ref in in_refs],
                               lambda i0: v_ref[t, pl.ds(i0, SUBLANES), :],
                               [p[...] for p in prm_refs], s_s, y_s)
        return 0

    lax.fori_loop(0, r_ref.shape[0], step, 0)

    @pl.when(tb == pl.num_programs(1) - 1)
    def _():
        sout_ref[...] = s_s[...]


def _rwkv_scan_tokens_kernel(r_ref, k_ref, v_ref, dec_ref, a_ref, kk_ref, ka_ref, rk_ref, lg_ref,
                             lb_ref, s0_ref, y_ref, sout_ref, s_s, y_s, yt_s, vec_s, g_s):
    tb = pl.program_id(1)
    nseq, tt = y_ref.shape[0], y_ref.shape[1]
    sub = LANES // nseq
    in_refs = (r_ref, k_ref, v_ref, dec_ref, a_ref)
    prm_refs = (kk_ref, ka_ref, rk_ref, lg_ref, lb_ref)

    @pl.when(tb == 0)
    def _():
        s_s[...] = s0_ref[...]
        g_s[...] = jnp.ones(g_s.shape, F32)

    def sub_block(sb, _):
        t0 = pl.multiple_of(sb * sub, sub)

        def step(t, _):
            yt_s[t] = _rwkv_token_keymajor([ref[t0 + t] for ref in in_refs],
                                           lambda rs: v_ref[t0 + t, rs, :],
                                           [p[...] for p in prm_refs], s_s, y_s, vec_s, g_s)
            return 0

        lax.fori_loop(0, sub, step, 0)
        for j in range(RW_HEAD):
            s_s[j] = s_s[j] * g_s[j:j + 1, :]
        g_s[...] = jnp.ones(g_s.shape, F32)
        heads =_swap_index_with_lane_pos([yt_s[t] for t in range(sub)], sub)
        for g in range(D_MODEL // LANES):
            tile = jnp.concatenate([heads[2 * g], heads[2 * g + 1]], axis=0).T
            for b in range(nseq):
                y_ref[b, pl.ds(t0, sub), g * LANES:(g + 1) * LANES] = tile[b * sub:(b + 1) * sub, :]
        return 0

    lax.fori_loop(0, tt // sub, sub_block, 0)

    @pl.when(tb == pl.num_programs(1) - 1)
    def _():
        sout_ref[...] = s_s[...]


def _rwkv_scan(r, k, v, dec, a, s0_all, s_layer, wts, nb, seq):
    nl = s0_all.shape[0]
    n = nb * seq
    hd = RW_HEAD
    batch_lanes = nb % LANES == 0
    if batch_lanes:
        nbg = nb // LANES
        groups, chains = RW_HEADS * nbg, LANES
        to_l = lambda x: (x.reshape(nbg, LANES, seq, RW_HEADS, hd).transpose(0, 3, 2, 4, 1)
                          .reshape(groups, seq, hd, chains))
        from_l = lambda y: (y.reshape(nbg, RW_HEADS, seq, hd, LANES).transpose(0, 4, 2, 1, 3)
                            .reshape(n, D_MODEL))
        par = lambda p: jnp.broadcast_to(p.reshape(1, RW_HEADS, hd, 1),
                                         (nbg, RW_HEADS, hd, chains)).reshape(groups, hd, chains)
        st_in = lambda s: (s.reshape(nl, nbg, LANES, RW_HEADS, hd, hd).transpose(0, 1, 3, 4, 5, 2)
                           .reshape(nl, groups, hd, hd, chains))
        st_out = lambda s: (s.reshape(nbg, RW_HEADS, hd, hd, LANES).transpose(0, 4, 1, 2, 3)
                            .reshape(nb, RW_HEADS, hd, hd))
    else:
        groups, chains = 1, nb * RW_HEADS
        to_l = lambda x: (x.reshape(nb, seq, RW_HEADS, hd).transpose(1, 3, 0, 2)
                          .reshape(1, seq, hd, chains))
        from_l = lambda y: (y.reshape(seq, hd, nb, RW_HEADS).transpose(2, 0, 3, 1)
                            .reshape(n, D_MODEL))
        par = lambda p: jnp.broadcast_to(p.reshape(RW_HEADS, hd).T.reshape(1, hd, 1, RW_HEADS),
                                         (1, hd, nb, RW_HEADS)).reshape(1, hd, chains)
        st_in = lambda s: s.transpose(0, 3, 4, 1, 2).reshape(nl, 1, hd, hd, chains)
        st_out = lambda s: s.reshape(hd, hd, nb, RW_HEADS).transpose(2, 3, 0, 1)
    tt = _tile(seq, RW_TBLK)
    prm = pl.BlockSpec((None, hd, chains), lambda g, t: (g, 0, 0))
    st = pl.BlockSpec((None, hd, hd, chains), lambda g, t: (g, 0, 0, 0))
    st0 = pl.BlockSpec((None, None, hd, hd, chains), lambda g, t: (s_layer, g, 0, 0, 0))
    prms = [par(wts[nm]) for nm in ('k_k', 'k_a', 'r_k', 'lnx_g', 'lnx_b')]
    if _chain_layout_ok(nb, seq):
        tok = pl.BlockSpec((nb, tt, D_MODEL), lambda g, t: (0, t, 0))
        act = pl.BlockSpec((tt, hd, chains), lambda g, t: (t, 0, 0))
        y, s_new = pl.pallas_call(
            _rwkv_scan_tokens_kernel,
            out_shape=(jax.ShapeDtypeStruct((nb, seq, D_MODEL), F32),
                       jax.ShapeDtypeStruct((1, hd, hd, chains), F32)),
            grid=(1, seq // tt),
            in_specs=[act] * RW_IN + [prm] * 5 + [st0],
            out_specs=(tok, st),
            scratch_shapes=[pltpu.VMEM((hd, hd, chains), F32), pltpu.VMEM((hd, chains), F32),
                            pltpu.VMEM((RW_HEADS, hd, chains), F32),
                            pltpu.VMEM((4, hd, chains), F32), pltpu.VMEM((hd, chains), F32)],
            compiler_params=_params("parallel", "arbitrary"),
            name="rwkv_scan_tokens")(r, k, v, dec, a, *prms, st_in(s0_all).swapaxes(2, 3))
        return y.reshape(n, D_MODEL), st_out(s_new.swapaxes(1, 2))
    act = pl.BlockSpec((None, tt, hd, chains), lambda g, t: (g, t, 0, 0))
    y, s_new = pl.pallas_call(
        _rwkv_scan_kernel,
        out_shape=(jax.ShapeDtypeStruct((groups, seq, hd, chains), F32),
                   jax.ShapeDtypeStruct((groups, hd, hd, chains), F32)),
        grid=(groups, seq // tt),
        in_specs=[act] * RW_IN + [prm] * 5 + [st0],
        out_specs=(act, st),
        scratch_shapes=[pltpu.VMEM((hd, hd, chains), F32), pltpu.VMEM((hd, chains), F32)],
        compiler_params=_params("parallel", "arbitrary"),
        name="rwkv_scan")(to_l(r), to_l(k), to_l(v), to_l(dec), to_l(a), *prms, st_in(s0_all))
    return from_l(y), st_out(s_new)


def _rwkv_post_kernel(y_ref, g_ref, x_ref, wo_ref, n1g_ref, n1b_ref, o_ref):
    h = _dot((y_ref[...] * g_ref[...]).astype(BF16), wo_ref[...])
    o_ref[...] = _layer_norm(ALPHA * x_ref[...] + h, n1g_ref[...], n1b_ref[...])


def _rwkv_post(y, g, x, wo_bf, n1g, n1b):
    n = x.shape[0]
    tm = _tile(n, 512)
    vec = pl.BlockSpec((1, D_MODEL), lambda i: (0, 0))
    return pl.pallas_call(
        _rwkv_post_kernel, out_shape=jax.ShapeDtypeStruct((n, D_MODEL), F32), grid=(n // tm,),
        in_specs=[_row_spec(tm, D_MODEL)] * 3 + [_const_spec((D_MODEL, D_MODEL)), vec, vec],
        out_specs=_row_spec(tm, D_MODEL),
        compiler_params=_params("parallel"), name="rwkv_post")(y, g, x, wo_bf, n1g, n1b)


def _trunk(x3, p4, hg_s0, wkv_s0, shift0, cache_k, cache_v, page_table, W):
    nb, seq, _ = x3.shape
    n = nb * seq
    x = x3.reshape(n, D_MODEL)
    new_hg, new_wkv, new_shift = [], [], []
    kv = None
    v_first = None
    for i in range(DEPTH):
        j = i // 2
        if i % 2 == 0:
            wk_t = W['wk_t'][j] if page_table is None else None
            h4, q, k_all, v_all = _even_in(x, W['w_in_even'][j], wk_t, j, kv, nb, seq)
            kv = (k_all, v_all)
            oh, s_hg = _hgrn(h4, hg_s0, min(j, hg_s0.shape[0] - 1), W['hg_lb_logits'],
                             W['hg_norm_g'][j], j, nb, seq)
            if page_table is None:
                od = _attn_prompt(q, k_all, v_all, W['lamp'][j], W['da_norm_g'][j], j, nb, seq)
            else:
                od = _attn_sample(q, k_all, v_all, cache_k, cache_v, page_table, W['lamp'][j],
                                  W['da_norm_g'][j], j, nb, seq)
            x = _even_out(oh, od, x, W['w_out_even'][j], W['ln1_g'][i], W['ln1_b'][i])
            new_hg.append(s_hg)
        else:
            need_vf = j == 0 and N_ODD > 1
            outs = _rwkv_pre(x, shift0[j], v_first, W['rw'][j], nb, seq, need_vf)
            r, k, v, g, dec, a = outs[:6]
            if need_vf:
                v_first = outs[6]
            y, s_wkv = _rwkv_scan(r, k, v, dec, a, wkv_s0, min(j, wkv_s0.shape[0] - 1),
                                  W['rw_scan'][j], nb, seq)
            new_shift.append(x.reshape(nb, seq, D_MODEL)[:, -1])
            new_wkv.append(s_wkv)
            x = _rwkv_post(y, g, x, W['rw_w_o'][j], W['ln1_g'][i], W['ln1_b'][i])
        x = _mlp(x, p4.reshape(DEPTH, n, PLE_DIM), i, W['mlp_up'][i], W['mlp_down'][i],
                 W['ln2_g'][i], W['ln2_b'][i], W['ple_gate'][i], W['ple_proj'][i])
    if page_table is None:
        k_out = kv[0].reshape(N_EVEN, nb, DA_HEADS, 2, DA_DH, seq).transpose(0, 1, 5, 2, 3, 4)
    else:
        k_out = kv[0].reshape(N_EVEN, nb, seq, DA_HEADS, 2, DA_DH)
    return (x.reshape(nb, seq, D_MODEL), k_out,
            kv[1].reshape(N_EVEN, nb, seq, DA_HEADS, DA_DV), jnp.stack(new_hg),
            jnp.stack(new_wkv), jnp.stack(new_shift))


def kernel(x_prompt, x_sample, cache_k, cache_v, state_hgrn, state_wkv, state_shift, page_table,
           p_prompt, p_sample, w_in_even, w_out_even, hg_lb_logits, hg_norm_g,
           da_lam_q1, da_lam_k1, da_lam_q2, da_lam_k2, da_norm_g,
           rw_mix, rw_w_r, rw_w_k, rw_w_v, rw_w_o, rw_w0, rw_w1, rw_w2,
           rw_a0, rw_a1, rw_a2, rw_v0, rw_v1, rw_v2, rw_g1, rw_g2,
           rw_k_k, rw_k_a, rw_r_k, rw_lnx_g, rw_lnx_b,
           ln1_g, ln1_b, ln2_g, ln2_b, mlp_up, mlp_down, ple_proj, ple_gate):
    bf = lambda w: w.astype(BF16)
    vec = lambda w: w.reshape(1, -1)

    def lam_rows(j):
        pad = lambda a: jnp.pad(a.astype(F32), (0, LANES - a.shape[0]))
        lam_init = 0.8 - 0.6 * math.exp(-0.3 * (2 * j))
        rows = [pad(da_lam_q1[j]), pad(da_lam_k1[j]), pad(da_lam_q2[j]), pad(da_lam_k2[j]),
                jnp.full((LANES,), lam_init, F32)]
        return jnp.stack(rows + [jnp.zeros((LANES,), F32)] * 3)

    rw = []
    for j in range(N_ODD):
        d = dict(mix=rw_mix[j], w_r=bf(rw_w_r[j]), w_k=bf(rw_w_k[j]), w_v=bf(rw_w_v[j]),
                 w0=vec(rw_w0[j]), w1=bf(rw_w1[j]), w2=bf(rw_w2[j]),
                 a0=vec(rw_a0[j]), a1=bf(rw_a1[j]), a2=bf(rw_a2[j]),
                 g1=bf(rw_g1[j]), g2=bf(rw_g2[j]))
        if j > 0:
            d.update(v0=vec(rw_v0[j - 1]), v1=bf(rw_v1[j - 1]), v2=bf(rw_v2[j - 1]))
        rw.append(d)
    W = dict(
        w_in_even=[bf(w_in_even[j]) for j in range(N_EVEN)],
        wk_t=[bf(w_in_even[j][:, 4 * HG_WIDTH + DA_QK:4 * HG_WIDTH + 2 * DA_QK].T) for j in range(N_EVEN)],
        w_out_even=[bf(w_out_even[j]) for j in range(N_EVEN)],
        hg_lb_logits=hg_lb_logits,
        hg_norm_g=[vec(hg_norm_g[j]) for j in range(N_EVEN)],
        da_norm_g=[vec(da_norm_g[j]) for j in range(N_EVEN)],
        lamp=[lam_rows(j) for j in range(N_EVEN)],
        rw=rw,
        rw_scan=[dict(k_k=rw_k_k[j], k_a=rw_k_a[j], r_k=rw_r_k[j], lnx_g=rw_lnx_g[j],
                      lnx_b=rw_lnx_b[j]) for j in range(N_ODD)],
        rw_w_o=[bf(rw_w_o[j]) for j in range(N_ODD)],
        ln1_g=[vec(ln1_g[i]) for i in range(DEPTH)], ln1_b=[vec(ln1_b[i]) for i in range(DEPTH)],
        ln2_g=[vec(ln2_g[i]) for i in range(DEPTH)], ln2_b=[vec(ln2_b[i]) for i in range(DEPTH)],
        mlp_up=[bf(mlp_up[i]) for i in range(DEPTH)], mlp_down=[bf(mlp_down[i]) for i in range(DEPTH)],
        ple_gate=[bf(ple_gate[i]) for i in range(DEPTH)], ple_proj=[bf(ple_proj[i]) for i in range(DEPTH)],
    )
    nbp = x_prompt.shape[0]
    dt = x_prompt.dtype
    hg0 = jnp.zeros((1, nbp, HG_HEADS, HG_DK, HG_DK), dt)
    wkv0 = jnp.zeros((1, nbp, RW_HEADS, RW_HEAD, RW_HEAD), dt)
    sh0 = jnp.zeros((N_ODD, nbp, D_MODEL), dt)
    y_p, k_p, v_p, hg_p, wkv_p, sh_p = _trunk(x_prompt, p_prompt, hg0, wkv0, sh0, None, None, None, W)
    ckt = jnp.transpose(cache_k, (0, 1, 3, 4, 5, 2)).reshape(
        cache_k.shape[0], cache_k.shape[1], DA_QK, PAGE_SIZE)
    cv = cache_v.reshape(cache_v.shape[0], cache_v.shape[1], PAGE_SIZE * DA_HEADS, DA_DV)
    y_s, k_s, v_s, hg_s, wkv_s, sh_s = _trunk(x_sample, p_sample, state_hgrn, state_wkv, state_shift,
                                              ckt, cv, page_table, W)
    return (y_p, y_s, k_p, v_p, k_s, v_s, hg_p, hg_s, wkv_p, wkv_s, sh_p, sh_s)
```

```python
import functools
import math

import jax
import jax.numpy as jnp
from jax import lax
from jax.experimental import pallas as pl
from jax.experimental.pallas import tpu as pltpu

F32 = jnp.float32
BF16 = jnp.bfloat16

D_MODEL = 1024
DEPTH = 4
PAGE_SIZE = 128
N_EVEN = (DEPTH + 1) // 2
N_ODD = DEPTH // 2
PLE_DIM = 256
D_FF = 4 * D_MODEL
HG_WIDTH = D_MODEL // 2
HG_HEADS = 4
HG_DK = HG_WIDTH // HG_HEADS
DA_WIDTH = D_MODEL - HG_WIDTH
DA_HEADS = 4
DA_DV = DA_WIDTH // DA_HEADS
DA_DH = DA_DV // 2
DA_QK = DA_HEADS * 2 * DA_DH
IN_COLS = 4 * HG_WIDTH + 2 * DA_QK + DA_WIDTH
RW_HEAD = 64
RW_HEADS = D_MODEL // RW_HEAD
ALPHA = (2.0 * DEPTH) ** 0.25
LN_EPS = 1e-5
RMS_EPS = 1e-6
GN_EPS = 64e-5
F_MIN = 1e-30
NEG_BIG = -1e30

LANES = 128
SUBLANES = 8
VMEM_LIMIT = 52 * 1024 * 1024
HG_CHUNK = 16
MM_ROWS = 16


def _params(*sem):
    return pltpu.CompilerParams(dimension_semantics=sem, vmem_limit_bytes=VMEM_LIMIT)


def _const_spec(shape):
    nd = len(shape)
    return pl.BlockSpec(shape, lambda *_: (0,) * nd, pipeline_mode=pl.Buffered(1))


def _row_spec(rows, cols):
    return pl.BlockSpec((rows, cols), lambda i: (i, 0))


def _tile(n, pref):
    t = min(n, pref)
    while n % t:
        t //= 2
    return t


def _dot(a, b):
    return jnp.dot(a, b, preferred_element_type=F32)


def _dot_nt(a, b):
    return lax.dot_general(a, b, (((1,), (1,)), ((), ())), preferred_element_type=F32)


def _dot_tn(a, b):
    return lax.dot_general(a, b, (((0,), (0,)), ((), ())), preferred_element_type=F32)


def _split3(x):
    h = x.astype(BF16)
    r = x - h.astype(F32)
    m = r.astype(BF16)
    l = (r - m.astype(F32)).astype(BF16)
    return h, m, l


def _dot_x3(x, rhs_bf16):
    h, m, l = _split3(x)
    return _dot(h, rhs_bf16) + _dot(m, rhs_bf16) + _dot(l, rhs_bf16)


def _dot_x3_lhs(lhs_bf16, x):
    h, m, l = _split3(x)
    return _dot(lhs_bf16, h) + _dot(lhs_bf16, m) + _dot(lhs_bf16, l)


def _layer_norm(z, g, b):
    mu = jnp.mean(z, axis=-1, keepdims=True)
    zc = z - mu
    var = jnp.mean(zc * zc, axis=-1, keepdims=True)
    return zc * lax.rsqrt(var + LN_EPS) * g + b


def _sigmoid(x):
    return 1.0 / (1.0 + jnp.exp(-x))


def _head_ones(width):
    r = lax.broadcasted_iota(jnp.int32, (LANES, LANES), 0) // width
    c = lax.broadcasted_iota(jnp.int32, (LANES, LANES), 1) // width
    return jnp.where(r == c, 1.0, 0.0).astype(BF16)


def _segsum(x, ones):
    cols = x.shape[1] // LANES
    parts = [_dot_x3(x[:, c * LANES:(c + 1) * LANES], ones) for c in range(cols)]
    return parts[0] if cols == 1 else jnp.concatenate(parts, axis=1)


def _even_in_kernel(x_ref, w_ref, *refs, k_transposed):
    h4_ref, q_ref, k_ref, v_ref = refs[-4:]
    xb = x_ref[...].astype(BF16)
    cw = HG_WIDTH
    for c in range(4):
        h4_ref[:, c * cw:(c + 1) * cw] = _dot(xb, w_ref[:, c * cw:(c + 1) * cw])
    q0 = 4 * HG_WIDTH
    q_ref[...] = _dot(xb, w_ref[:, q0:q0 + DA_QK])
    if k_transposed:
        k_ref[...] = _dot_nt(refs[0][...], xb)
    else:
        k_ref[...] = _dot(xb, w_ref[:, q0 + DA_QK:q0 + 2 * DA_QK])
    v_ref[...] = _dot(xb, w_ref[:, q0 + 2 * DA_QK:q0 + 2 * DA_QK + DA_WIDTH])


def _even_in(x, w_in_bf, wk_t_bf, layer, kv_stacks, nb, seq):
    n = x.shape[0]
    k_transposed = wk_t_bf is not None
    tm = _tile(seq if k_transposed else n, 512)
    slab =lambda width: pl.BlockSpec((None, tm, width), lambda i: (layer, i, 0))
    in_specs = [_row_spec(tm, D_MODEL), _const_spec((D_MODEL, IN_COLS))]
    args = [x, w_in_bf]
    if k_transposed:
        tps = seq // tm
        k_shape = (N_EVEN, nb, DA_QK, seq)
        k_spec = pl.BlockSpec((None, None, DA_QK, tm), lambda i: (layer, i // tps, 0, i % tps))
        in_specs.append(_const_spec((DA_QK, D_MODEL)))
        args.append(wk_t_bf)
    else:
        k_shape = (N_EVEN, n, DA_QK)
        k_spec = slab(DA_QK)
    outs = (jax.ShapeDtypeStruct((n, 4 * HG_WIDTH), F32), jax.ShapeDtypeStruct((n, DA_QK), F32),
            jax.ShapeDtypeStruct(k_shape, F32), jax.ShapeDtypeStruct((N_EVEN, n, DA_WIDTH), F32))
    if kv_stacks is None:
        kv_stacks = (jnp.zeros(k_shape, F32), jnp.zeros((N_EVEN, n, DA_WIDTH), F32))
    aliases = {len(args): 2, len(args) + 1: 3}
    in_specs += [pl.BlockSpec(memory_space=pl.ANY)] * 2
    args += list(kv_stacks)
    return pl.pallas_call(
        functools.partial(_even_in_kernel, k_transposed=k_transposed), out_shape=outs,
        grid=(n // tm,), in_specs=in_specs,
        out_specs=(_row_spec(tm, 4 * HG_WIDTH), _row_spec(tm, DA_QK), k_spec, slab(DA_WIDTH)),
        input_output_aliases=aliases,
        compiler_params=_params("parallel"), name="even_in")(*args)


def _pad_rows(x, rows):
    if x.shape[0] >= rows:
        return x
    return jnp.concatenate([x, jnp.zeros((rows - x.shape[0], x.shape[1]), x.dtype)], axis=0)


def _hgrn_kernel(h4_ref, s0_ref, lbl_ref, ng_ref, o_ref, sout_ref,
                 st_ref, q_s, k_s, b_s, u_s, *, layer, chunk, carry):
    tl = h4_ref.shape[0]
    nchunk = tl // chunk
    w = HG_WIDTH

    rows = [lbl_ref[i:i + 1, :] for i in range(N_EVEN)]
    mx = functools.reduce(jnp.maximum, rows)
    es = [jnp.exp(r - mx) for r in rows]
    den = functools.reduce(lambda a, b: a + b, es)
    sm = [e / den for e in es]
    cs = functools.reduce(lambda a, b: a + b, sm[:layer + 1])
    lb = cs - sm[0]

    hq = h4_ref[:, 0:w]
    hf = h4_ref[:, w:2 * w]
    q_s[...] = hq * _sigmoid(hq)
    f = lb + (1.0 - lb) * _sigmoid(hf)
    g = jnp.log(jnp.maximum(f, F_MIN))
    k_s[...] = 1.0 - f
    ri = lax.broadcasted_iota(jnp.int32, (tl, tl), 0)
    ci = lax.broadcasted_iota(jnp.int32, (tl, tl), 1)
    tril = jnp.where((ri // chunk == ci // chunk) & (ci <= ri), 1.0, 0.0).astype(BF16)
    b_s[...] = _dot_x3_lhs(tril, g)

    if carry:
        @pl.when(pl.program_id(1) == 0)
        def _():
            for h in range(HG_HEADS):
                st_ref[h] = s0_ref[0, h].T

    rowid = lax.broadcasted_iota(jnp.int32, (chunk, 1), 0)
    ng = ng_ref[...]

    def chunk_slices(c, h):
        rs = pl.ds(pl.multiple_of(c * chunk, chunk), chunk)
        cs_ = slice(h * HG_DK, (h + 1) * HG_DK)
        v_cols = slice(2 * w + h * HG_DK, 2 * w + (h + 1) * HG_DK)
        return rs, cs_, v_cols

    def state_free(c, _):
        for h in range(HG_HEADS):
            rs, cs_, v_cols = chunk_slices(c, h)
            q, k, b, v = q_s[rs, cs_], k_s[rs, cs_], b_s[rs, cs_], h4_ref[rs, v_cols]
            groups = [(g0, min(g0 + SUBLANES, chunk)) for g0 in range(0, chunk, SUBLANES)]
            o_g = [None] * len(groups)
            for s in range(chunk):
                for gi, (g0, g1) in enumerate(groups):
                    if g1 <= s:
                        continue
                    rel = b[g0:g1, :] - b[s:s + 1, :]
                    if g0 > s:
                        e = jnp.exp(rel)
                    else:
                        keep = rowid[g0:g1, :] >= s
                        e = jnp.where(keep, jnp.exp(jnp.where(keep, rel, 0.0)), 0.0)
                    col = jnp.sum(q[g0:g1, :] * e * k[s:s + 1, :], axis=-1, keepdims=True)
                    term = col * v[s:s + 1, :]
                    o_g[gi] = term if o_g[gi] is None else o_g[gi] + term
            o_ref[rs, cs_] = o_g[0] if len(o_g) == 1 else jnp.concatenate(o_g, axis=0)
            kd = _pad_rows((k * jnp.exp(b[chunk - 1:chunk, :] - b)).astype(BF16), MM_ROWS)
            vd = _pad_rows(v.astype(BF16), MM_ROWS)
            u_s[c * HG_HEADS + h] = _dot_tn(vd, kd)
        return 0

    def with_state(c, _):
        for h in range(HG_HEADS):
            rs, cs_, _ = chunk_slices(c, h)
            q, b = q_s[rs, cs_], b_s[rs, cs_]
            hg = h4_ref[rs, 3 * w + h * HG_DK:3 * w + (h + 1) * HG_DK]
            st = st_ref[h] if carry else s0_ref[c, h].T
            st_new = st * jnp.exp(b[chunk - 1:chunk, :]) + u_s[c * HG_HEADS + h]
            if carry:
                st_ref[h] = st_new
            else:
                sout_ref[c, h] = st_new.T
            qd = _pad_rows((q * jnp.exp(b)).astype(BF16), MM_ROWS)
            o = o_ref[rs, cs_] + _dot_nt(qd, st.astype(BF16))[:chunk]
            on = o * lax.rsqrt(jnp.mean(o * o, axis=-1, keepdims=True) + RMS_EPS) * ng
            o_ref[rs, cs_] = on * _sigmoid(hg)
        return 0

    lax.fori_loop(0, nchunk, state_free, 0, unroll=2)
    lax.fori_loop(0, nchunk, with_state, 0, unroll=True)

    if carry:
        @pl.when(pl.program_id(1) == pl.num_programs(1) - 1)
        def _():
            for h in range(HG_HEADS):
                sout_ref[0, h] = st_ref[h].T


def _hgrn(h4, s0_all, s_layer, lb_logits, norm_g, layer, nb, seq):
    n = h4.shape[0]
    chunk = math.gcd(seq, HG_CHUNK)
    carry = seq > chunk
    if carry:
        tl = _tile(seq, 256)
        grid = (nb, seq // tl)
        row_map = lambda b, l: (b * (seq // tl) + l, 0)
        st_map = lambda b, l: (b, 0, 0, 0)
        st_blk = (1, HG_HEADS, HG_DK, HG_DK)
        sem = ("parallel", "arbitrary")
    else:
        bt = _tile(nb, 8)
        tl = bt * seq
        grid = (nb // bt, 1)
        row_map = lambda b, l: (b, 0)
        st_map = lambda b, l: (b, 0, 0, 0)
        st_blk = (bt, HG_HEADS, HG_DK, HG_DK)
        sem = ("parallel", "arbitrary")
    kern = functools.partial(_hgrn_kernel, layer=layer, chunk=chunk, carry=carry)
    return pl.pallas_call(
        kern,
        out_shape=(jax.ShapeDtypeStruct((n, HG_WIDTH), F32),
                   jax.ShapeDtypeStruct((nb, HG_HEADS, HG_DK, HG_DK), F32)),
        grid=grid,
        in_specs=[pl.BlockSpec((tl, 4 * HG_WIDTH), row_map),
                  pl.BlockSpec((None,) + st_blk, lambda b, l: (s_layer, b, 0, 0, 0)),
                  pl.BlockSpec((N_EVEN, HG_WIDTH), lambda b, l: (0, 0)),
                  pl.BlockSpec((1, HG_DK), lambda b, l: (0, 0))],
        out_specs=(pl.BlockSpec((tl, HG_WIDTH), row_map), pl.BlockSpec(st_blk, st_map)),
        scratch_shapes=[pltpu.VMEM((HG_HEADS, HG_DK, HG_DK), F32),
                        pltpu.VMEM((tl, HG_WIDTH), F32), pltpu.VMEM((tl, HG_WIDTH), F32),
                        pltpu.VMEM((tl, HG_WIDTH), F32),
                        pltpu.VMEM((tl // chunk * HG_HEADS, HG_DK, HG_DK), F32)],
        compiler_params=_params(*sem), name="hgrn2")(h4, s0_all, lb_logits, norm_g)


def _lambda(lamp_ref):
    lam_init = lamp_ref[4:5, 0:1]
    l1 = jnp.sum(lamp_ref[0:1, :] * lamp_ref[1:2, :], axis=-1, keepdims=True)
    l2 = jnp.sum(lamp_ref[2:3, :] * lamp_ref[3:4, :], axis=-1, keepdims=True)
    return jnp.exp(l1) - jnp.exp(l2) + lam_init, lam_init


def _diff_finish(o0, o1, lam, lam_init, ng):
    o = o0 - lam * o1
    return o * lax.rsqrt(jnp.mean(o * o, axis=-1, keepdims=True) + RMS_EPS) * ng * (1.0 - lam_init)


def _lane_fold(x, op):
    parts = [x[:, c * LANES:(c + 1) * LANES] for c in range(x.shape[1] // LANES)]
    return functools.reduce(op, parts)


def _attn_prompt_kernel(q_ref, k_ref, v_ref, lamp_ref, ng_ref, o_ref, kb_s, vb_s, s_s, mp_s, lp_s,
                        acc_s):
    tq = q_ref.shape[0]
    i = pl.program_id(2)

    @pl.when(i == 0)
    def _():
        for jb in range(kb_s.shape[0]):
            kb_s[jb] = k_ref[:, jb * tq:(jb + 1) * tq].astype(BF16)
        vb_s[...] = v_ref[...].astype(BF16)

    lane = lax.broadcasted_iota(jnp.int32, (tq, DA_DV), 1)
    qs = q_ref[...] * (DA_DH ** -0.5)
    q2 = jnp.concatenate([jnp.where(lane < DA_DH, qs, 0.0), jnp.where(lane >= DA_DH, qs, 0.0)],
                         axis=0).astype(BF16)
    mp_s[...] = jnp.full(mp_s.shape, NEG_BIG, F32)

    def scores(j, masked):
        s = _dot(q2, kb_s[j])
        if masked:
            row = lax.broadcasted_iota(jnp.int32, (2 * tq, tq), 0) % tq
            col = lax.broadcasted_iota(jnp.int32, (2 * tq, tq), 1)
            s = jnp.where(col <= row, s, NEG_BIG)
        s_s[j] = s
        mp_s[...] = jnp.maximum(mp_s[...], _lane_fold(s, jnp.maximum))

    def pass1(j, c):
        scores(j, False)
        return c

    lax.fori_loop(0, i, pass1, 0)
    scores(i, True)
    m = jnp.max(mp_s[...], axis=-1, keepdims=True)
    lp_s[...] = jnp.zeros(lp_s.shape, F32)
    acc_s[...] = jnp.zeros(acc_s.shape, F32)

    def pass2(j, c):
        r0 = pl.multiple_of(j * tq, tq)
        p = jnp.exp(s_s[j] - m)
        lp_s[...] += _lane_fold(p, lambda a, b: a + b)
        acc_s[...] += _dot(p.astype(BF16), vb_s[pl.ds(r0, tq), :])
        return c

    lax.fori_loop(0, i + 1, pass2, 0)
    l = jnp.sum(lp_s[...], axis=-1, keepdims=True)
    lam, lam_init = _lambda(lamp_ref)
    o0 = acc_s[0:tq, :] / l[0:tq, :]
    o1 = acc_s[tq:2 * tq, :] / l[tq:2 * tq, :]
    o_ref[...] = _diff_finish(o0, o1, lam, lam_init, ng_ref[...])


def _attn_prompt(q, k_all, v_all, lamp, norm_g, layer, nb, seq):
    n = q.shape[0]
    tq = _tile(seq, 512)
    nq = seq // tq
    return pl.pallas_call(
        _attn_prompt_kernel, out_shape=jax.ShapeDtypeStruct((n, DA_WIDTH), F32),
        grid=(nb, DA_HEADS, nq),
        in_specs=[pl.BlockSpec((tq, DA_DV), lambda b, h, i: (b * nq + i, h)),
                  pl.BlockSpec((None, None, DA_DV, seq), lambda b, h, i: (layer, b, h, 0)),
                  pl.BlockSpec((None, seq, DA_DV), lambda b, h, i: (layer, b, h)),
                  pl.BlockSpec((8, LANES), lambda b, h, i: (0, 0)),
                  pl.BlockSpec((1, DA_DV), lambda b, h, i: (0, 0))],
        out_specs=pl.BlockSpec((tq, DA_DV), lambda b, h, i: (b * nq + i, h)),
        scratch_shapes=[pltpu.VMEM((nq, DA_DV, tq), BF16), pltpu.VMEM((seq, DA_DV), BF16),
                        pltpu.VMEM((nq, 2 * tq, tq), F32), pltpu.VMEM((2 * tq, LANES), F32),
                        pltpu.VMEM((2 * tq, LANES), F32), pltpu.VMEM((2 * tq, DA_DV), F32)],
        compiler_params=_params("parallel", "parallel", "arbitrary"),
        name="attn_prompt")(q, k_all, v_all, lamp, norm_g)


def _attn_sample_kernel(pt_ref, q_ref, *refs, n_pages):
    kp = refs[:n_pages]
    vp = refs[n_pages:2 * n_pages]
    kn_ref, vn_ref, lamp_ref, ng_ref, o_ref, kt_s, v_s = refs[2 * n_pages:]
    t = q_ref.shape[0]
    nrow = 2 * DA_HEADS * t
    hrow = 2 * t

    qs = q_ref[...] * (DA_DH ** -0.5)
    qt = jnp.concatenate([qs] * (2 * DA_HEADS), axis=0)
    r = lax.broadcasted_iota(jnp.int32, (nrow, DA_QK), 0) // t
    c = lax.broadcasted_iota(jnp.int32, (nrow, DA_QK), 1) // DA_DH
    qbd = jnp.where(r == c, qt, 0.0).astype(BF16)

    pc = PAGE_SIZE * DA_HEADS
    for i in range(n_pages):
        kt_s[:, i * PAGE_SIZE:(i + 1) * PAGE_SIZE] = kp[i][...].astype(BF16)
        v_s[i * pc:(i + 1) * pc, :] = vp[i][...].astype(BF16)
    s_old = _dot(qbd, kt_s[...])
    kb = _pad_rows(kn_ref[...], PAGE_SIZE).astype(BF16)
    vb = _pad_rows(vn_ref[...], PAGE_SIZE).astype(BF16)
    r = lax.broadcasted_iota(jnp.int32, (nrow, PAGE_SIZE), 0) % t
    c = lax.broadcasted_iota(jnp.int32, (nrow, PAGE_SIZE), 1)
    s_new = jnp.where(c <= r, _dot_nt(qbd, kb), NEG_BIG)
    m = jnp.maximum(jnp.max(s_old, axis=-1, keepdims=True), jnp.max(s_new, axis=-1, keepdims=True))
    p_old = jnp.exp(s_old - m)
    p_new = jnp.exp(s_new - m)
    l = jnp.sum(p_old, axis=-1, keepdims=True) + jnp.sum(p_new, axis=-1, keepdims=True)
    pb_old = p_old.astype(BF16)
    pb_new = p_new.astype(BF16)

    spread = jnp.where(lax.broadcasted_iota(jnp.int32, (PAGE_SIZE, pc), 1) // DA_HEADS
                       == lax.broadcasted_iota(jnp.int32, (PAGE_SIZE, pc), 0), 1.0, 0.0).astype(BF16)
    own = (lax.broadcasted_iota(jnp.int32, (n_pages * nrow, pc), 1) % DA_HEADS
           == (lax.broadcasted_iota(jnp.int32, (n_pages * nrow, pc), 0) % nrow) // hrow)
    p_rows = jnp.concatenate([pb_old[:, i * PAGE_SIZE:(i + 1) * PAGE_SIZE] for i in range(n_pages)],
                             axis=0)
    pe = jnp.where(own, _dot(p_rows, spread), 0.0).astype(BF16)
    pe = jnp.concatenate([pe[i * nrow:(i + 1) * nrow, :] for i in range(n_pages)], axis=1)
    acc = _dot(pe, v_s[...])
    acc = acc + jnp.concatenate([_dot(pb_new[h * hrow:(h + 1) * hrow, :], vb[:, h * DA_DV:(h + 1) * DA_DV])
                                 for h in range(DA_HEADS)], axis=0)
    o = acc / l
    lam, lam_init = _lambda(lamp_ref)
    ng = ng_ref[...]
    for h in range(DA_HEADS):
        r0 = h * hrow
        o_ref[:, h * DA_DV:(h + 1) * DA_DV] = _diff_finish(o[r0:r0 + t, :], o[r0 + t:r0 + 2 * t, :],
                                                           lam, lam_init, ng)


def _attn_sample(q, k_new, v_new, cache_kt, cache_v, page_table, lamp, norm_g, layer, nb, seq):
    n = q.shape[0]
    n_pages = page_table.shape[1]
    row_map = lambda b, pt: (b, 0)
    new_map = lambda b, pt: (layer, b, 0)
    k_specs = [pl.BlockSpec((None, None, DA_QK, PAGE_SIZE),
                            lambda b, pt, i=i: (layer, pt[b * n_pages + i], 0, 0))
               for i in range(n_pages)]
    v_specs = [pl.BlockSpec((None, None, PAGE_SIZE * DA_HEADS, DA_DV),
                            lambda b, pt, i=i: (layer, pt[b * n_pages + i], 0, 0))
               for i in range(n_pages)]
    grid_spec = pltpu.PrefetchScalarGridSpec(
        num_scalar_prefetch=1, grid=(nb,),
        in_specs=[pl.BlockSpec((seq, DA_QK), row_map)] + k_specs + v_specs
        + [pl.BlockSpec((None, seq, DA_QK), new_map), pl.BlockSpec((None, seq, DA_WIDTH), new_map),
           pl.BlockSpec((8, LANES), lambda b, pt: (0, 0)),
           pl.BlockSpec((1, DA_DV), lambda b, pt: (0, 0))],
        out_specs=pl.BlockSpec((seq, DA_WIDTH), row_map),
        scratch_shapes=[pltpu.VMEM((DA_QK, n_pages * PAGE_SIZE), BF16),
                        pltpu.VMEM((n_pages * PAGE_SIZE * DA_HEADS, DA_DV), BF16)])
    return pl.pallas_call(
        functools.partial(_attn_sample_kernel, n_pages=n_pages),
        out_shape=jax.ShapeDtypeStruct((n, DA_WIDTH), F32),
        grid_spec=grid_spec, compiler_params=_params("parallel"),
        name="attn_sample")(page_table.reshape(-1), q, *([cache_kt] * n_pages),
                            *([cache_v] * n_pages), k_new, v_new, lamp, norm_g)


def _even_out_kernel(oh_ref, od_ref, x_ref, w_ref, g_ref, b_ref, o_ref):
    h = (_dot(oh_ref[...].astype(BF16), w_ref[0:HG_WIDTH, :])
         + _dot(od_ref[...].astype(BF16), w_ref[HG_WIDTH:D_MODEL, :]))
    o_ref[...] = _layer_norm(ALPHA * x_ref[...] + h, g_ref[...], b_ref[...])


def _even_out(oh, od, x, w_out_bf, g, b):
    n = x.shape[0]
    tm = _tile(n, 512)
    vec = pl.BlockSpec((1, D_MODEL), lambda i: (0, 0))
    return pl.pallas_call(
        _even_out_kernel, out_shape=jax.ShapeDtypeStruct((n, D_MODEL), F32), grid=(n // tm,),
        in_specs=[_row_spec(tm, HG_WIDTH), _row_spec(tm, DA_WIDTH), _row_spec(tm, D_MODEL),
                  _const_spec((D_MODEL, D_MODEL)), vec, vec],
        out_specs=_row_spec(tm, D_MODEL),
        compiler_params=_params("parallel"), name="even_out")(oh, od, x, w_out_bf, g, b)


FF_CHUNK = 1024


def _mlp_kernel(x_ref, p_ref, up_ref, down_ref, g_ref, b_ref, gate_ref, proj_ref, o_ref):
    x = x_ref[...]
    xb = x.astype(BF16)
    acc = jnp.zeros(x.shape, F32)
    for c in range(D_FF // FF_CHUNK):
        cs = slice(c * FF_CHUNK, (c + 1) * FF_CHUNK)
        hid = jnp.maximum(_dot(xb, up_ref[:, cs]), 0.0)
        acc = acc + _dot((hid * hid).astype(BF16), down_ref[cs, :])
    x2 = _layer_norm(ALPHA * x + acc, g_ref[...], b_ref[...])
    gate = _sigmoid(_dot(x2.astype(BF16), gate_ref[...]))
    o_ref[...] = x2 + gate * _dot(p_ref[...].astype(BF16), proj_ref[...])


def _mlp(x, p_all, layer, up_bf, down_bf, g, b, gate_bf, proj_bf):
    n = x.shape[0]
    tm = _tile(n, 512)
    vec = pl.BlockSpec((1, D_MODEL), lambda i: (0, 0))
    return pl.pallas_call(
        _mlp_kernel, out_shape=jax.ShapeDtypeStruct((n, D_MODEL), F32), grid=(n // tm,),
        in_specs=[_row_spec(tm, D_MODEL), pl.BlockSpec((None, tm, PLE_DIM), lambda i: (layer, i, 0)),
                  _const_spec((D_MODEL, D_FF)), _const_spec((D_FF, D_MODEL)), vec, vec,
                  _const_spec((D_MODEL, D_MODEL)), _const_spec((PLE_DIM, D_MODEL))],
        out_specs=_row_spec(tm, D_MODEL),
        compiler_params=_params("parallel"), name="mlp")(x, p_all, up_bf, down_bf, g, b, gate_bf,
                                                         proj_bf)


def _transpose_atom_grid(p):
    shape = p[0][0].shape
    sub = lax.broadcasted_iota(jnp.int32, shape, 0)
    half = lax.broadcasted_iota(jnp.int32, shape, 1) // (LANES // 2)
    s0 = sub & 1
    na, nc = len(p), len(p[0])

    def bit0(x):
        z = pltpu.roll(x, LANES // 2, 1)
        moved = jnp.where(s0 == 1, pltpu.roll(z, 1, 0), pltpu.roll(z, SUBLANES - 1, 0))
        return jnp.where(s0 == half, x, moved)

    p = [[bit0(t) for t in row] for row in p]
    for shift, cbit in ((2, 1), (4, 2)):
        up = (sub & shift) != 0
        for a in range(na):
            for c in range(nc):
                if c & cbit == 0:
                    lo, hi = p[a][c], p[a][c + cbit]
                    p[a][c] = jnp.where(up, pltpu.roll(hi, shift, 0), lo)
                    p[a][c + cbit] = jnp.where(up, hi, pltpu.roll(lo, SUBLANES - shift, 0))
    q = nc // na
    return [[p[c // q][a2 * q + c % q] for c in range(nc)] for a2 in range(na)]


def _tokens_to_chains(x, nseq, tt, out_ref):
    sub = LANES // nseq
    ntile = D_MODEL // LANES
    for sb in range(tt // sub):
        grids = []
        for b in range(nseq):
            r0 = b * tt + sb * sub
            grids.append(_transpose_atom_grid(
                [[x[r0 + a * SUBLANES:r0 + (a + 1) * SUBLANES, c * LANES:(c + 1) * LANES]
                  for c in range(ntile)] for a in range(sub // SUBLANES)]))
        for tau in range(ntile):
            tile = jnp.concatenate([g[a][tau] for g in grids for a in range(sub // SUBLANES)],
                                   axis=0).T
            out_ref[sb * sub + 2 * tau] = tile[0:RW_HEAD, :]
            out_ref[sb * sub + 2 * tau + 1] = tile[RW_HEAD:2 * RW_HEAD, :]


def _rwkv_pre_kernel(*refs, has_vfirst, chains, want_v_tokens):
    if chains:
        x_ref, x8_ref, sh_ref = refs[:3]
        refs = refs[3:]
    else:
        x_ref, xp_ref = refs[:2]
        refs = refs[2:]
    vf_ref = vt_o = v0_ref = v1_ref = v2_ref = None
    if has_vfirst:
        vf_ref, refs = refs[0], refs[1:]
    (mix_ref, wr_ref, wk_ref, wv_ref, w0_ref, w1_ref, w2_ref, a0_ref, a1_ref, a2_ref) = refs[:10]
    refs = refs[10:]
    if has_vfirst:
        v0_ref, v1_ref, v2_ref = refs[:3]
        refs = refs[3:]
    g1_ref, g2_ref, r_o, k_o, v_o, g_o, dec_o, a_o = refs[:8]
    if want_v_tokens:
        vt_o = refs[8]
    if chains:
        nseq, tt = x_ref.shape[0], x_ref.shape[1]
        x = x_ref[...].reshape(nseq * tt, D_MODEL)
        first = pl.program_id(0) == 0
        before = jnp.where(first, sh_ref[...], x8_ref[:, SUBLANES - 1, :])
        before = jnp.concatenate([jnp.broadcast_to(before[b:b + 1, :], (tt, D_MODEL))
                                  for b in range(nseq)], axis=0)
        rowid = lax.broadcasted_iota(jnp.int32, x.shape, 0)
        x_prev = jnp.where(rowid % tt == 0, before, pltpu.roll(x, 1, 0))
    else:
        x = x_ref[...]
        x_prev = xp_ref[...]
    xx = x_prev - x
    xr, xw, xk, xv, xa, xg = ((x + xx * mix_ref[m:m + 1, :]).astype(BF16) for m in range(6))
    r = _dot(xr, wr_ref[...])
    k = _dot(xk, wk_ref[...])
    v = _dot(xv, wv_ref[...])
    lw = w0_ref[...] + _dot(jnp.tanh(_dot(xw, w1_ref[...])).astype(BF16), w2_ref[...])
    w = -(jnp.maximum(-lw, 0.0) + jnp.log(1.0 + jnp.exp(-jnp.abs(lw)))) - 0.5
    if has_vfirst:
        mixv = _sigmoid(v0_ref[...] + _dot(_dot(xv, v1_ref[...]).astype(BF16), v2_ref[...]))
        vf = vf_ref[...].reshape(v.shape) if chains else vf_ref[...]
        v = v + (vf - v) * mixv
    a = _sigmoid(a0_ref[...] + _dot(_dot(xa, a1_ref[...]).astype(BF16), a2_ref[...]))
    g = _dot(_sigmoid(_dot(xg, g1_ref[...])).astype(BF16), g2_ref[...])
    dec = jnp.exp(-jnp.exp(w))
    if chains:
        g_o[...] = g.reshape(g_o.shape)
        if want_v_tokens:
            vt_o[...] = v.reshape(vt_o.shape)
        for val, ref in ((r, r_o), (k, k_o), (v, v_o), (dec, dec_o), (a, a_o)):
            _tokens_to_chains(val, nseq, tt, ref)
    else:
        r_o[...] = r
        k_o[...] = k
        v_o[...] = v
        g_o[...] = g
        dec_o[...] = dec
        a_o[...] = a
        if want_v_tokens:
            vt_o[...] = v


def _chain_layout_ok(nb, seq):
    return nb * RW_HEADS == LANES and LANES // nb == RW_HEADS and seq % RW_TBLK == 0


def _rwkv_pre(x, shift0, v_first, wts, nb, seq, want_v_tokens):
    n = x.shape[0]
    has_vfirst = v_first is not None
    chains = _chain_layout_ok(nb, seq)
    vec = pl.BlockSpec((1, D_MODEL), lambda i: (0, 0))
    if chains:
        tt = RW_TBLK
        tok3 = pl.BlockSpec((nb, tt, D_MODEL), lambda i: (0, i, 0))
        x3 = x.reshape(nb, seq, D_MODEL)
        r8 = tt // SUBLANES
        acts = [x3, x3, shift0]
        act_specs = [tok3,
                     pl.BlockSpec((nb, SUBLANES, D_MODEL), lambda i: (0, jnp.maximum(i * r8 - 1, 0), 0)),
                     pl.BlockSpec((nb, D_MODEL), lambda i: (0, 0))]
        if has_vfirst:
            acts.append(v_first.reshape(nb, seq, D_MODEL))
            act_specs.append(tok3)
        grid = (seq // tt,)
        chain_out = jax.ShapeDtypeStruct((seq, RW_HEAD, LANES), F32)
        chain_spec = pl.BlockSpec((tt, RW_HEAD, LANES), lambda i: (i, 0, 0))
        tok_out = jax.ShapeDtypeStruct((nb, seq, D_MODEL), F32)
        out_shape = [chain_out, chain_out, chain_out, tok_out, chain_out, chain_out]
        out_specs = [chain_spec, chain_spec, chain_spec, tok3, chain_spec, chain_spec]
        if want_v_tokens:
            out_shape.append(tok_out)
            out_specs.append(tok3)
    else:
        tm = _tile(n, 256)
        xs = x.reshape(nb, seq, D_MODEL)
        x_prev = jnp.concatenate([shift0[:, None, :], xs[:, :-1]], axis=1).reshape(n, D_MODEL)
        acts = [x, x_prev] + ([v_first] if has_vfirst else [])
        act_specs = [_row_spec(tm, D_MODEL)] * len(acts)
        grid = (n // tm,)
        out_shape = [jax.ShapeDtypeStruct((n, D_MODEL), F32)] * 6
        out_specs = [_row_spec(tm, D_MODEL)] * 6
    names = ['mix', 'w_r', 'w_k', 'w_v', 'w0', 'w1', 'w2', 'a0', 'a1', 'a2']
    if has_vfirst:
        names += ['v0', 'v1', 'v2']
    names += ['g1', 'g2']
    ws = [wts[nm] for nm in names]
    w_specs = [vec if w.shape == (1, D_MODEL) else _const_spec(w.shape) for w in ws]
    outs = pl.pallas_call(
        functools.partial(_rwkv_pre_kernel, has_vfirst=has_vfirst, chains=chains,
                          want_v_tokens=want_v_tokens and chains),
        out_shape=tuple(out_shape), grid=grid,
        in_specs=act_specs + w_specs, out_specs=tuple(out_specs),
        compiler_params=_params("parallel"), name="rwkv_pre")(*acts, *ws)
    outs = list(outs)
    if chains:
        outs[3] = outs[3].reshape(n, D_MODEL)
        if want_v_tokens:
            outs[6] = outs[6].reshape(n, D_MODEL)
    elif want_v_tokens:
        outs.append(outs[2])
    return outs


RW_TBLK = 32


RW_IN = 5


def _rwkv_token(ins, v_rows, prm, s_s, y_s):
    r, k, v, dec, ag = ins
    kk_p, ka_p, rk_p, lg_p, lb_p = prm
    kk = k * kk_p
    kk = kk / jnp.maximum(jnp.sqrt(jnp.sum(kk * kk, axis=0, keepdims=True)), 1e-12)
    a = -kk
    b = kk * ag
    k2 = k * (1.0 + (ag - 1.0) * ka_p)
    wr = dec * r
    br = jnp.sum(b * r, axis=0, keepdims=True)
    kr = jnp.sum(k2 * r, axis=0, keepdims=True)

    def rows(i8, _):
        i0 = pl.multiple_of(i8 * SUBLANES, SUBLANES)
        v8 = v_rows(i0)
        ys = []
        for ii in range(SUBLANES):
            si = s_s[i0 + ii]
            sa = jnp.sum(si * a, axis=0, keepdims=True)
            qs = jnp.sum(si * wr, axis=0, keepdims=True)
            vi = v8[ii:ii + 1, :]
            s_s[i0 + ii] = si * dec + sa * b + vi * k2
            ys.append(qs + sa * br + vi * kr)
        y_s[pl.ds(i0, SUBLANES), :] = jnp.concatenate(ys, axis=0)
        return 0

    lax.fori_loop(0, RW_HEAD // SUBLANES, rows, 0)
    y = y_s[...]
    mu = jnp.mean(y, axis=0, keepdims=True)
    yc = y - mu
    var = jnp.mean(yc * yc, axis=0, keepdims=True)
    yn = yc * lax.rsqrt(var + GN_EPS) * lg_p + lb_p
    return yn + jnp.sum(r * k2 * rk_p, axis=0, keepdims=True) * v


RW_IBLK = 32
RW_ACC = 2


def _rwkv_token_keymajor(ins, v_rows, prm, s_s, y_s, vec_s, g_s):
    r, k, v, dec, ag = ins
    kk_p, ka_p, rk_p, lg_p, lb_p = prm
    kk = k * kk_p
    kk = kk / jnp.maximum(jnp.sqrt(jnp.sum(kk * kk, axis=0, keepdims=True)), 1e-12)
    b = kk * ag
    k2 = k * (1.0 + (ag - 1.0) * ka_p)
    br = jnp.sum(b * r, axis=0, keepdims=True)
    kr = jnp.sum(k2 * r, axis=0, keepdims=True)
    g_old = g_s[...]
    g_new = g_old * dec
    g_s[...] = g_new
    g_inv = 1.0 / g_new
    for idx, val in enumerate((-kk * g_old, r * g_new, b * g_inv, k2 * g_inv)):
        vec_s[idx] = val
    row = lambda idx, j: vec_s[idx, j:j + 1, :]

    for p in range(RW_HEAD // RW_IBLK):
        rs = pl.ds(p * RW_IBLK, RW_IBLK)
        sa = [None] * RW_ACC
        qs = [None] * RW_ACC
        for j in range(RW_HEAD):
            sj = s_s[j, rs, :]
            c = j % RW_ACC
            sa[c] = sj * row(0, j) if sa[c] is None else sa[c] + sj * row(0, j)
            qs[c] = sj * row(1, j) if qs[c] is None else qs[c] + sj * row(1, j)
        sa = functools.reduce(lambda x, y: x + y, sa)
        qs = functools.reduce(lambda x, y: x + y, qs)
        vi = v_rows(rs)
        for j in range(RW_HEAD):
            s_s[j, rs, :] = s_s[j, rs, :] + sa * row(2, j) + vi * row(3, j)
        y_s[rs, :] = qs + sa * br + vi * kr
    y = y_s[...]
    mu = jnp.mean(y, axis=0, keepdims=True)
    yc = y - mu
    var = jnp.mean(yc * yc, axis=0, keepdims=True)
    yn = yc * lax.rsqrt(var + GN_EPS) * lg_p + lb_p
    return yn + jnp.sum(r * k2 * rk_p, axis=0, keepdims=True) * v


def _rwkv_scan_kernel(r_ref, k_ref, v_ref, dec_ref, a_ref, kk_ref, ka_ref, rk_ref, lg_ref, lb_ref,
                      s0_ref, y_ref, sout_ref, s_s, y_s):
    tb = pl.program_id(1)
    in_refs = (r_ref, k_ref, v_ref, dec_ref, a_ref)
    prm_refs = (kk_ref, ka_ref, rk_ref, lg_ref, lb_ref)

    @pl.when(tb == 0)
    def _():
        s_s[...] = s0_ref[...]

    def step(t, _):
        y_ref[t] = _rwkv_token([ref[t] for ref in in_refs],
                               lambda i0: v_ref[t, pl.ds(i0, SUBLANES), :],
                               [p[...] for p in prm_refs], s_s, y_s)
        return 0

    lax.fori_loop(0, r_ref.shape[0], step, 0)

    @pl.when(tb == pl.num_programs(1) - 1)
    def _():
        sout_ref[...] = s_s[...]


def _rwkv_scan_tokens_kernel(r_ref, k_ref, v_ref, dec_ref, a_ref, kk_ref, ka_ref, rk_ref, lg_ref,
                             lb_ref, s0_ref, y_ref, sout_ref, s_s, y_s, yt_s, vec_s, g_s):
    tb = pl.program_id(1)
    nseq, tt = y_ref.shape[0], y_ref.shape[1]
    sub = LANES // nseq
    in_refs = (r_ref, k_ref, v_ref, dec_ref, a_ref)
    prm_refs = (kk_ref, ka_ref, rk_ref, lg_ref, lb_ref)

    @pl.when(tb == 0)
    def _():
        s_s[...] = s0_ref[...]
        g_s[...] = jnp.ones(g_s.shape, F32)

    def sub_block(sb, _):
        t0 = pl.multiple_of(sb * sub, sub)

        def step(t, _):
            yt_s[t] = _rwkv_token_keymajor([ref[t0 + t] for ref in in_refs],
                                           lambda rs: v_ref[t0 + t, rs, :],
                                           [p[...] for p in prm_refs], s_s, y_s, vec_s, g_s)
            return 0

        lax.fori_loop(0, sub, step, 0)
        for j in range(RW_HEAD):
            s_s[j] = s_s[j] * g_s[j:j + 1, :]
        g_s[...] = jnp.ones(g_s.shape, F32)
        ntile = D_MODEL // LANES
        na = sub // SUBLANES
        tiles = [jnp.concatenate([yt_s[2 * tau], yt_s[2 * tau + 1]], axis=0).T for tau in range(ntile)]
        for b in range(nseq):
            grid = _transpose_atom_grid(
                [[tiles[tau][b * sub + a * SUBLANES:b * sub + (a + 1) * SUBLANES, :]
                  for tau in range(ntile)] for a in range(na)])
            for a in range(na):
                for c in range(ntile):
                    y_ref[b, pl.ds(t0 + a * SUBLANES, SUBLANES), c * LANES:(c + 1) * LANES] = grid[a][c]
        return 0

    lax.fori_loop(0, tt // sub, sub_block, 0)

    @pl.when(tb == pl.num_programs(1) - 1)
    def _():
        sout_ref[...] = s_s[...]


def _rwkv_scan(r, k, v, dec, a, s0_all, s_layer, wts, nb, seq):
    nl = s0_all.shape[0]
    n = nb * seq
    hd = RW_HEAD
    batch_lanes = nb % LANES == 0
    if batch_lanes:
        nbg = nb // LANES
        groups, chains = RW_HEADS * nbg, LANES
        to_l = lambda x: (x.reshape(nbg, LANES, seq, RW_HEADS, hd).transpose(0, 3, 2, 4, 1)
                          .reshape(groups, seq, hd, chains))
        from_l = lambda y: (y.reshape(nbg, RW_HEADS, seq, hd, LANES).transpose(0, 4, 2, 1, 3)
                            .reshape(n, D_MODEL))
        par = lambda p: jnp.broadcast_to(p.reshape(1, RW_HEADS, hd, 1),
                                         (nbg, RW_HEADS, hd, chains)).reshape(groups, hd, chains)
        st_in = lambda s: (s.reshape(nl, nbg, LANES, RW_HEADS, hd, hd).transpose(0, 1, 3, 4, 5, 2)
                           .reshape(nl, groups, hd, hd, chains))
        st_out = lambda s: (s.reshape(nbg, RW_HEADS, hd, hd, LANES).transpose(0, 4, 1, 2, 3)
                            .reshape(nb, RW_HEADS, hd, hd))
    else:
        groups, chains = 1, nb * RW_HEADS
        to_l = lambda x: (x.reshape(nb, seq, RW_HEADS, hd).transpose(1, 3, 0, 2)
                          .reshape(1, seq, hd, chains))
        from_l = lambda y: (y.reshape(seq, hd, nb, RW_HEADS).transpose(2, 0, 3, 1)
                            .reshape(n, D_MODEL))
        par = lambda p: jnp.broadcast_to(p.reshape(RW_HEADS, hd).T.reshape(1, hd, 1, RW_HEADS),
                                         (1, hd, nb, RW_HEADS)).reshape(1, hd, chains)
        st_in = lambda s: s.transpose(0, 3, 4, 1, 2).reshape(nl, 1, hd, hd, chains)
        st_out = lambda s: s.reshape(hd, hd, nb, RW_HEADS).transpose(2, 3, 0, 1)
    tt = _tile(seq, RW_TBLK)
    prm = pl.BlockSpec((None, hd, chains), lambda g, t: (g, 0, 0))
    st = pl.BlockSpec((None, hd, hd, chains), lambda g, t: (g, 0, 0, 0))
    st0 = pl.BlockSpec((None, None, hd, hd, chains), lambda g, t: (s_layer, g, 0, 0, 0))
    prms = [par(wts[nm]) for nm in ('k_k', 'k_a', 'r_k', 'lnx_g', 'lnx_b')]
    if _chain_layout_ok(nb, seq):
        tok = pl.BlockSpec((nb, tt, D_MODEL), lambda g, t: (0, t, 0))
        act = pl.BlockSpec((tt, hd, chains), lambda g, t: (t, 0, 0))
        y, s_new = pl.pallas_call(
            _rwkv_scan_tokens_kernel,
            out_shape=(jax.ShapeDtypeStruct((nb, seq, D_MODEL), F32),
                       jax.ShapeDtypeStruct((1, hd, hd, chains), F32)),
            grid=(1, seq // tt),
            in_specs=[act] * RW_IN + [prm] * 5 + [st0],
            out_specs=(tok, st),
            scratch_shapes=[pltpu.VMEM((hd, hd, chains), F32), pltpu.VMEM((hd, chains), F32),
                            pltpu.VMEM((RW_HEADS, hd, chains), F32),
                            pltpu.VMEM((4, hd, chains), F32), pltpu.VMEM((hd, chains), F32)],
            compiler_params=_params("parallel", "arbitrary"),
            name="rwkv_scan_tokens")(r, k, v, dec, a, *prms, st_in(s0_all).swapaxes(2, 3))
        return y.reshape(n, D_MODEL), st_out(s_new.swapaxes(1, 2))
    act = pl.BlockSpec((None, tt, hd, chains), lambda g, t: (g, t, 0, 0))
    y, s_new = pl.pallas_call(
        _rwkv_scan_kernel,
        out_shape=(jax.ShapeDtypeStruct((groups, seq, hd, chains), F32),
                   jax.ShapeDtypeStruct((groups, hd, hd, chains), F32)),
        grid=(groups, seq // tt),
        in_specs=[act] * RW_IN + [prm] * 5 + [st0],
        out_specs=(act, st),
        scratch_shapes=[pltpu.VMEM((hd, hd, chains), F32), pltpu.VMEM((hd, chains), F32)],
        compiler_params=_params("parallel", "arbitrary"),
        name="rwkv_scan")(to_l(r), to_l(k), to_l(v), to_l(dec), to_l(a), *prms, st_in(s0_all))
    return from_l(y), st_out(s_new)


def _rwkv_post_kernel(y_ref, g_ref, x_ref, wo_ref, n1g_ref, n1b_ref, o_ref):
    h = _dot((y_ref[...] * g_ref[...]).astype(BF16), wo_ref[...])
    o_ref[...] = _layer_norm(ALPHA * x_ref[...] + h, n1g_ref[...], n1b_ref[...])


def _rwkv_post(y, g, x, wo_bf, n1g, n1b):
    n = x.shape[0]
    tm = _tile(n, 512)
    vec = pl.BlockSpec((1, D_MODEL), lambda i: (0, 0))
    return pl.pallas_call(
        _rwkv_post_kernel, out_shape=jax.ShapeDtypeStruct((n, D_MODEL), F32), grid=(n // tm,),
        in_specs=[_row_spec(tm, D_MODEL)] * 3 + [_const_spec((D_MODEL, D_MODEL)), vec, vec],
        out_specs=_row_spec(tm, D_MODEL),
        compiler_params=_params("parallel"), name="rwkv_post")(y, g, x, wo_bf, n1g, n1b)


def _trunk(x3, p4, hg_s0, wkv_s0, shift0, cache_k, cache_v, page_table, W):
    nb, seq, _ = x3.shape
    n = nb * seq
    x = x3.reshape(n, D_MODEL)
    new_hg, new_wkv, new_shift = [], [], []
    kv = None
    v_first = None
    for i in range(DEPTH):
        j = i // 2
        if i % 2 == 0:
            wk_t = W['wk_t'][j] if page_table is None else None
            h4, q, k_all, v_all = _even_in(x, W['w_in_even'][j], wk_t, j, kv, nb, seq)
            kv = (k_all, v_all)
            oh, s_hg = _hgrn(h4, hg_s0, min(j, hg_s0.shape[0] - 1), W['hg_lb_logits'],
                             W['hg_norm_g'][j], j, nb, seq)
            if page_table is None:
                od = _attn_prompt(q, k_all, v_all, W['lamp'][j], W['da_norm_g'][j], j, nb, seq)
            else:
                od = _attn_sample(q, k_all, v_all, cache_k, cache_v, page_table, W['lamp'][j],
                                  W['da_norm_g'][j], j, nb, seq)
            x = _even_out(oh, od, x, W['w_out_even'][j], W['ln1_g'][i], W['ln1_b'][i])
            new_hg.append(s_hg)
        else:
            need_vf = j == 0 and N_ODD > 1
            outs = _rwkv_pre(x, shift0[j], v_first, W['rw'][j], nb, seq, need_vf)
            r, k, v, g, dec, a = outs[:6]
            if need_vf:
                v_first = outs[6]
            y, s_wkv = _rwkv_scan(r, k, v, dec, a, wkv_s0, min(j, wkv_s0.shape[0] - 1),
                                  W['rw_scan'][j], nb, seq)
            new_shift.append(x.reshape(nb, seq, D_MODEL)[:, -1])
            new_wkv.append(s_wkv)
            x = _rwkv_post(y, g, x, W['rw_w_o'][j], W['ln1_g'][i], W['ln1_b'][i])
        x = _mlp(x, p4.reshape(DEPTH, n, PLE_DIM), i, W['mlp_up'][i], W['mlp_down'][i],
                 W['ln2_g'][i], W['ln2_b'][i], W['ple_gate'][i], W['ple_proj'][i])
    if page_table is None:
        k_out = kv[0].reshape(N_EVEN, nb, DA_HEADS, 2, DA_DH, seq).transpose(0, 1, 5, 2, 3, 4)
    else:
        k_out = kv[0].reshape(N_EVEN, nb, seq, DA_HEADS, 2, DA_DH)
    return (x.reshape(nb, seq, D_MODEL), k_out,
            kv[1].reshape(N_EVEN, nb, seq, DA_HEADS, DA_DV), jnp.stack(new_hg),
            jnp.stack(new_wkv), jnp.stack(new_shift))


def kernel(x_prompt, x_sample, cache_k, cache_v, state_hgrn, state_wkv, state_shift, page_table,
           p_prompt, p_sample, w_in_even, w_out_even, hg_lb_logits, hg_norm_g,
           da_lam_q1, da_lam_k1, da_lam_q2, da_lam_k2, da_norm_g,
           rw_mix, rw_w_r, rw_w_k, rw_w_v, rw_w_o, rw_w0, rw_w1, rw_w2,
           rw_a0, rw_a1, rw_a2, rw_v0, rw_v1, rw_v2, rw_g1, rw_g2,
           rw_k_k, rw_k_a, rw_r_k, rw_lnx_g, rw_lnx_b,
           ln1_g, ln1_b, ln2_g, ln2_b, mlp_up, mlp_down, ple_proj, ple_gate):
    bf = lambda w: w.astype(BF16)
    vec = lambda w: w.reshape(1, -1)

    def lam_rows(j):
        pad = lambda a: jnp.pad(a.astype(F32), (0, LANES - a.shape[0]))
        lam_init = 0.8 - 0.6 * math.exp(-0.3 * (2 * j))
        rows = [pad(da_lam_q1[j]), pad(da_lam_k1[j]), pad(da_lam_q2[j]), pad(da_lam_k2[j]),
                jnp.full((LANES,), lam_init, F32)]
        return jnp.stack(rows + [jnp.zeros((LANES,), F32)] * 3)

    rw = []
    for j in range(N_ODD):
        d = dict(mix=rw_mix[j], w_r=bf(rw_w_r[j]), w_k=bf(rw_w_k[j]), w_v=bf(rw_w_v[j]),
                 w0=vec(rw_w0[j]), w1=bf(rw_w1[j]), w2=bf(rw_w2[j]),
                 a0=vec(rw_a0[j]), a1=bf(rw_a1[j]), a2=bf(rw_a2[j]),
                 g1=bf(rw_g1[j]), g2=bf(rw_g2[j]))
        if j > 0:
            d.update(v0=vec(rw_v0[j - 1]), v1=bf(rw_v1[j - 1]), v2=bf(rw_v2[j - 1]))
        rw.append(d)
    W = dict(
        w_in_even=[bf(w_in_even[j]) for j in range(N_EVEN)],
        wk_t=[bf(w_in_even[j][:, 4 * HG_WIDTH + DA_QK:4 * HG_WIDTH + 2 * DA_QK].T) for j in range(N_EVEN)],
        w_out_even=[bf(w_out_even[j]) for j in range(N_EVEN)],
        hg_lb_logits=hg_lb_logits,
        hg_norm_g=[vec(hg_norm_g[j]) for j in range(N_EVEN)],
        da_norm_g=[vec(da_norm_g[j]) for j in range(N_EVEN)],
        lamp=[lam_rows(j) for j in range(N_EVEN)],
        rw=rw,
        rw_scan=[dict(k_k=rw_k_k[j], k_a=rw_k_a[j], r_k=rw_r_k[j], lnx_g=rw_lnx_g[j],
                      lnx_b=rw_lnx_b[j]) for j in range(N_ODD)],
        rw_w_o=[bf(rw_w_o[j]) for j in range(N_ODD)],
        ln1_g=[vec(ln1_g[i]) for i in range(DEPTH)], ln1_b=[vec(ln1_b[i]) for i in range(DEPTH)],
        ln2_g=[vec(ln2_g[i]) for i in range(DEPTH)], ln2_b=[vec(ln2_b[i]) for i in range(DEPTH)],
        mlp_up=[bf(mlp_up[i]) for i in range(DEPTH)], mlp_down=[bf(mlp_down[i]) for i in range(DEPTH)],
        ple_gate=[bf(ple_gate[i]) for i in range(DEPTH)], ple_proj=[bf(ple_proj[i]) for i in range(DEPTH)],
    )
    nbp = x_prompt.shape[0]
    dt = x_prompt.dtype
    hg0 = jnp.zeros((1, nbp, HG_HEADS, HG_DK, HG_DK), dt)
    wkv0 = jnp.zeros((1, nbp, RW_HEADS, RW_HEAD, RW_HEAD), dt)
    sh0 = jnp.zeros((N_ODD, nbp, D_MODEL), dt)
    y_p, k_p, v_p, hg_p, wkv_p, sh_p = _trunk(x_prompt, p_prompt, hg0, wkv0, sh0, None, None, None, W)
    ckt = jnp.transpose(cache_k, (0, 1, 3, 4, 5, 2)).reshape(
        cache_k.shape[0], cache_k.shape[1], DA_QK, PAGE_SIZE)
    cv = cache_v.reshape(cache_v.shape[0], cache_v.shape[1], PAGE_SIZE * DA_HEADS, DA_DV)
    y_s, k_s, v_s, hg_s, wkv_s, sh_s = _trunk(x_sample, p_sample, state_hgrn, state_wkv, state_shift,
                                              ckt, cv, page_table, W)
    return (y_p, y_s, k_p, v_p, k_s, v_s, hg_p, hg_s, wkv_p, wkv_s, sh_p, sh_s)
```

```python
import functools
import math

import jax
import jax.numpy as jnp
from jax import lax
from jax.experimental import pallas as pl
from jax.experimental.pallas import tpu as pltpu

F32 = jnp.float32
BF16 = jnp.bfloat16

D_MODEL = 1024
DEPTH = 4
PAGE_SIZE = 128
N_EVEN = (DEPTH + 1) // 2
N_ODD = DEPTH // 2
PLE_DIM = 256
D_FF = 4 * D_MODEL
HG_WIDTH = D_MODEL // 2
HG_HEADS = 4
HG_DK = HG_WIDTH // HG_HEADS
DA_WIDTH = D_MODEL - HG_WIDTH
DA_HEADS = 4
DA_DV = DA_WIDTH // DA_HEADS
DA_DH = DA_DV // 2
DA_QK = DA_HEADS * 2 * DA_DH
IN_COLS = 4 * HG_WIDTH + 2 * DA_QK + DA_WIDTH
RW_HEAD = 64
RW_HEADS = D_MODEL // RW_HEAD
ALPHA = (2.0 * DEPTH) ** 0.25
LN_EPS = 1e-5
RMS_EPS = 1e-6
GN_EPS = 64e-5
F_MIN = 1e-30
NEG_BIG = -1e30

LANES = 128
SUBLANES = 8
VMEM_LIMIT = 52 * 1024 * 1024
HG_CHUNK = 16
MM_ROWS = 16


def _params(*sem):
    return pltpu.CompilerParams(dimension_semantics=sem, vmem_limit_bytes=VMEM_LIMIT)


def _const_spec(shape):
    nd = len(shape)
    return pl.BlockSpec(shape, lambda *_: (0,) * nd, pipeline_mode=pl.Buffered(1))


def _row_spec(rows, cols):
    return pl.BlockSpec((rows, cols), lambda i: (i, 0))


def _tile(n, pref):
    t = min(n, pref)
    while n % t:
        t //= 2
    return t


def _dot(a, b):
    return jnp.dot(a, b, preferred_element_type=F32)


def _dot_nt(a, b):
    return lax.dot_general(a, b, (((1,), (1,)), ((), ())), preferred_element_type=F32)


def _dot_tn(a, b):
    return lax.dot_general(a, b, (((0,), (0,)), ((), ())), preferred_element_type=F32)


def _split3(x):
    h = x.astype(BF16)
    r = x - h.astype(F32)
    m = r.astype(BF16)
    l = (r - m.astype(F32)).astype(BF16)
    return h, m, l


def _dot_x3(x, rhs_bf16):
    h, m, l = _split3(x)
    return _dot(h, rhs_bf16) + _dot(m, rhs_bf16) + _dot(l, rhs_bf16)


def _dot_x3_lhs(lhs_bf16, x):
    h, m, l = _split3(x)
    return _dot(lhs_bf16, h) + _dot(lhs_bf16, m) + _dot(lhs_bf16, l)


def _layer_norm(z, g, b):
    mu = jnp.mean(z, axis=-1, keepdims=True)
    zc = z - mu
    var = jnp.mean(zc * zc, axis=-1, keepdims=True)
    return zc * lax.rsqrt(var + LN_EPS) * g + b


def _sigmoid(x):
    return 1.0 / (1.0 + jnp.exp(-x))


def _head_ones(width):
    r = lax.broadcasted_iota(jnp.int32, (LANES, LANES), 0) // width
    c = lax.broadcasted_iota(jnp.int32, (LANES, LANES), 1) // width
    return jnp.where(r == c, 1.0, 0.0).astype(BF16)


def _segsum(x, ones):
    cols = x.shape[1] // LANES
    parts = [_dot_x3(x[:, c * LANES:(c + 1) * LANES], ones) for c in range(cols)]
    return parts[0] if cols == 1 else jnp.concatenate(parts, axis=1)


def _even_in_kernel(x_ref, w_ref, *refs, k_transposed):
    h4_ref, q_ref, k_ref, v_ref = refs[-4:]
    xb = x_ref[...].astype(BF16)
    cw = HG_WIDTH
    for c in range(4):
        h4_ref[:, c * cw:(c + 1) * cw] = _dot(xb, w_ref[:, c * cw:(c + 1) * cw])
    q0 = 4 * HG_WIDTH
    q_ref[...] = _dot(xb, w_ref[:, q0:q0 + DA_QK])
    if k_transposed:
        k_ref[...] = _dot_nt(refs[0][...], xb)
    else:
        k_ref[...] = _dot(xb, w_ref[:, q0 + DA_QK:q0 + 2 * DA_QK])
    v_ref[...] = _dot(xb, w_ref[:, q0 + 2 * DA_QK:q0 + 2 * DA_QK + DA_WIDTH])


def _even_in(x, w_in_bf, wk_t_bf, layer, kv_stacks, nb, seq):
    n = x.shape[0]
    k_transposed = wk_t_bf is not None
    tm = _tile(seq if k_transposed else n, 512)
    slab =lambda width: pl.BlockSpec((None, tm, width), lambda i: (layer, i, 0))
    in_specs = [_row_spec(tm, D_MODEL), _const_spec((D_MODEL, IN_COLS))]
    args = [x, w_in_bf]
    if k_transposed:
        tps = seq // tm
        k_shape = (N_EVEN, nb, DA_QK, seq)
        k_spec = pl.BlockSpec((None, None, DA_QK, tm), lambda i: (layer, i // tps, 0, i % tps))
        in_specs.append(_const_spec((DA_QK, D_MODEL)))
        args.append(wk_t_bf)
    else:
        k_shape = (N_EVEN, n, DA_QK)
        k_spec = slab(DA_QK)
    outs = (jax.ShapeDtypeStruct((n, 4 * HG_WIDTH), F32), jax.ShapeDtypeStruct((n, DA_QK), F32),
            jax.ShapeDtypeStruct(k_shape, F32), jax.ShapeDtypeStruct((N_EVEN, n, DA_WIDTH), F32))
    if kv_stacks is None:
        kv_stacks = (jnp.zeros(k_shape, F32), jnp.zeros((N_EVEN, n, DA_WIDTH), F32))
    aliases = {len(args): 2, len(args) + 1: 3}
    in_specs += [pl.BlockSpec(memory_space=pl.ANY)] * 2
    args += list(kv_stacks)
    return pl.pallas_call(
        functools.partial(_even_in_kernel, k_transposed=k_transposed), out_shape=outs,
        grid=(n // tm,), in_specs=in_specs,
        out_specs=(_row_spec(tm, 4 * HG_WIDTH), _row_spec(tm, DA_QK), k_spec, slab(DA_WIDTH)),
        input_output_aliases=aliases,
        compiler_params=_params("parallel"), name="even_in")(*args)


def _pad_rows(x, rows):
    if x.shape[0] >= rows:
        return x
    return jnp.concatenate([x, jnp.zeros((rows - x.shape[0], x.shape[1]), x.dtype)], axis=0)


def _hgrn_kernel(h4_ref, s0_ref, lbl_ref, ng_ref, o_ref, sout_ref,
                 st_ref, q_s, k_s, b_s, u_s, *, layer, chunk, carry):
    tl = h4_ref.shape[0]
    nchunk = tl // chunk
    w = HG_WIDTH

    rows = [lbl_ref[i:i + 1, :] for i in range(N_EVEN)]
    mx = functools.reduce(jnp.maximum, rows)
    es = [jnp.exp(r - mx) for r in rows]
    den = functools.reduce(lambda a, b: a + b, es)
    sm = [e / den for e in es]
    cs = functools.reduce(lambda a, b: a + b, sm[:layer + 1])
    lb = cs - sm[0]

    hq = h4_ref[:, 0:w]
    hf = h4_ref[:, w:2 * w]
    q_s[...] = hq * _sigmoid(hq)
    f = lb + (1.0 - lb) * _sigmoid(hf)
    g = jnp.log(jnp.maximum(f, F_MIN))
    k_s[...] = 1.0 - f
    ri = lax.broadcasted_iota(jnp.int32, (tl, tl), 0)
    ci = lax.broadcasted_iota(jnp.int32, (tl, tl), 1)
    tril = jnp.where((ri // chunk == ci // chunk) & (ci <= ri), 1.0, 0.0).astype(BF16)
    b_s[...] = _dot_x3_lhs(tril, g)

    if carry:
        @pl.when(pl.program_id(1) == 0)
        def _():
            for h in range(HG_HEADS):
                st_ref[h] = s0_ref[0, h].T

    rowid = lax.broadcasted_iota(jnp.int32, (chunk, 1), 0)
    ng = ng_ref[...]

    def chunk_slices(c, h):
        rs = pl.ds(pl.multiple_of(c * chunk, chunk), chunk)
        cs_ = slice(h * HG_DK, (h + 1) * HG_DK)
        v_cols = slice(2 * w + h * HG_DK, 2 * w + (h + 1) * HG_DK)
        return rs, cs_, v_cols

    def state_free(c, _):
        for h in range(HG_HEADS):
            rs, cs_, v_cols = chunk_slices(c, h)
            q, k, b, v = q_s[rs, cs_], k_s[rs, cs_], b_s[rs, cs_], h4_ref[rs, v_cols]
            groups = [(g0, min(g0 + SUBLANES, chunk)) for g0 in range(0, chunk, SUBLANES)]
            o_g = [None] * len(groups)
            for s in range(chunk):
                for gi, (g0, g1) in enumerate(groups):
                    if g1 <= s:
                        continue
                    rel = b[g0:g1, :] - b[s:s + 1, :]
                    if g0 > s:
                        e = jnp.exp(rel)
                    else:
                        keep = rowid[g0:g1, :] >= s
                        e = jnp.where(keep, jnp.exp(jnp.where(keep, rel, 0.0)), 0.0)
                    col = jnp.sum(q[g0:g1, :] * e * k[s:s + 1, :], axis=-1, keepdims=True)
                    term = col * v[s:s + 1, :]
                    o_g[gi] = term if o_g[gi] is None else o_g[gi] + term
            o_ref[rs, cs_] = o_g[0] if len(o_g) == 1 else jnp.concatenate(o_g, axis=0)
            kd = _pad_rows((k * jnp.exp(b[chunk - 1:chunk, :] - b)).astype(BF16), MM_ROWS)
            vd = _pad_rows(v.astype(BF16), MM_ROWS)
            u_s[c * HG_HEADS + h] = _dot_tn(vd, kd)
        return 0

    def with_state(c, _):
        for h in range(HG_HEADS):
            rs, cs_, _ = chunk_slices(c, h)
            q, b = q_s[rs, cs_], b_s[rs, cs_]
            hg = h4_ref[rs, 3 * w + h * HG_DK:3 * w + (h + 1) * HG_DK]
            st = st_ref[h] if carry else s0_ref[c, h].T
            st_new = st * jnp.exp(b[chunk - 1:chunk, :]) + u_s[c * HG_HEADS + h]
            if carry:
                st_ref[h] = st_new
            else:
                sout_ref[c, h] = st_new.T
            qd = _pad_rows((q * jnp.exp(b)).astype(BF16), MM_ROWS)
            o = o_ref[rs, cs_] + _dot_nt(qd, st.astype(BF16))[:chunk]
            on = o * lax.rsqrt(jnp.mean(o * o, axis=-1, keepdims=True) + RMS_EPS) * ng
            o_ref[rs, cs_] = on * _sigmoid(hg)
        return 0

    lax.fori_loop(0, nchunk, state_free, 0, unroll=2)
    lax.fori_loop(0, nchunk, with_state, 0, unroll=True)

    if carry:
        @pl.when(pl.program_id(1) == pl.num_programs(1) - 1)
        def _():
            for h in range(HG_HEADS):
                sout_ref[0, h] = st_ref[h].T


def _hgrn(h4, s0_all, s_layer, lb_logits, norm_g, layer, nb, seq):
    n = h4.shape[0]
    chunk = math.gcd(seq, HG_CHUNK)
    carry = seq > chunk
    if carry:
        tl = _tile(seq, 256)
        grid = (nb, seq // tl)
        row_map = lambda b, l: (b * (seq // tl) + l, 0)
        st_map = lambda b, l: (b, 0, 0, 0)
        st_blk = (1, HG_HEADS, HG_DK, HG_DK)
        sem = ("parallel", "arbitrary")
    else:
        bt = _tile(nb, 8)
        tl = bt * seq
        grid = (nb // bt, 1)
        row_map = lambda b, l: (b, 0)
        st_map = lambda b, l: (b, 0, 0, 0)
        st_blk = (bt, HG_HEADS, HG_DK, HG_DK)
        sem = ("parallel", "arbitrary")
    kern = functools.partial(_hgrn_kernel, layer=layer, chunk=chunk, carry=carry)
    return pl.pallas_call(
        kern,
        out_shape=(jax.ShapeDtypeStruct((n, HG_WIDTH), F32),
                   jax.ShapeDtypeStruct((nb, HG_HEADS, HG_DK, HG_DK), F32)),
        grid=grid,
        in_specs=[pl.BlockSpec((tl, 4 * HG_WIDTH), row_map),
                  pl.BlockSpec((None,) + st_blk, lambda b, l: (s_layer, b, 0, 0, 0)),
                  pl.BlockSpec((N_EVEN, HG_WIDTH), lambda b, l: (0, 0)),
                  pl.BlockSpec((1, HG_DK), lambda b, l: (0, 0))],
        out_specs=(pl.BlockSpec((tl, HG_WIDTH), row_map), pl.BlockSpec(st_blk, st_map)),
        scratch_shapes=[pltpu.VMEM((HG_HEADS, HG_DK, HG_DK), F32),
                        pltpu.VMEM((tl, HG_WIDTH), F32), pltpu.VMEM((tl, HG_WIDTH), F32),
                        pltpu.VMEM((tl, HG_WIDTH), F32),
                        pltpu.VMEM((tl // chunk * HG_HEADS, HG_DK, HG_DK), F32)],
        compiler_params=_params(*sem), name="hgrn2")(h4, s0_all, lb_logits, norm_g)


def _lambda(lamp_ref):
    lam_init = lamp_ref[4:5, 0:1]
    l1 = jnp.sum(lamp_ref[0:1, :] * lamp_ref[1:2, :], axis=-1, keepdims=True)
    l2 = jnp.sum(lamp_ref[2:3, :] * lamp_ref[3:4, :], axis=-1, keepdims=True)
    return jnp.exp(l1) - jnp.exp(l2) + lam_init, lam_init


def _diff_finish(o0, o1, lam, lam_init, ng):
    o = o0 - lam * o1
    return o * lax.rsqrt(jnp.mean(o * o, axis=-1, keepdims=True) + RMS_EPS) * ng * (1.0 - lam_init)


def _lane_fold(x, op):
    parts = [x[:, c * LANES:(c + 1) * LANES] for c in range(x.shape[1] // LANES)]
    return functools.reduce(op, parts)


def _attn_prompt_kernel(q_ref, k_ref, v_ref, lamp_ref, ng_ref, o_ref, kb_s, vb_s, s_s, mp_s, lp_s,
                        acc_s):
    tq = q_ref.shape[0]
    i = pl.program_id(2)

    @pl.when(i == 0)
    def _():
        for jb in range(kb_s.shape[0]):
            kb_s[jb] = k_ref[:, jb * tq:(jb + 1) * tq].astype(BF16)
        vb_s[...] = v_ref[...].astype(BF16)

    lane = lax.broadcasted_iota(jnp.int32, (tq, DA_DV), 1)
    qs = q_ref[...] * (DA_DH ** -0.5)
    q2 = jnp.concatenate([jnp.where(lane < DA_DH, qs, 0.0), jnp.where(lane >= DA_DH, qs, 0.0)],
                         axis=0).astype(BF16)
    mp_s[...] = jnp.full(mp_s.shape, NEG_BIG, F32)

    def scores(j, masked):
        s = _dot(q2, kb_s[j])
        if masked:
            row = lax.broadcasted_iota(jnp.int32, (2 * tq, tq), 0) % tq
            col = lax.broadcasted_iota(jnp.int32, (2 * tq, tq), 1)
            s = jnp.where(col <= row, s, NEG_BIG)
        s_s[j] = s
        mp_s[...] = jnp.maximum(mp_s[...], _lane_fold(s, jnp.maximum))

    def pass1(j, c):
        scores(j, False)
        return c

    lax.fori_loop(0, i, pass1, 0)
    scores(i, True)
    m = jnp.max(mp_s[...], axis=-1, keepdims=True)
    lp_s[...] = jnp.zeros(lp_s.shape, F32)
    acc_s[...] = jnp.zeros(acc_s.shape, F32)

    def pass2(j, c):
        r0 = pl.multiple_of(j * tq, tq)
        p = jnp.exp(s_s[j] - m)
        lp_s[...] += _lane_fold(p, lambda a, b: a + b)
        acc_s[...] += _dot(p.astype(BF16), vb_s[pl.ds(r0, tq), :])
        return c

    lax.fori_loop(0, i + 1, pass2, 0)
    l = jnp.sum(lp_s[...], axis=-1, keepdims=True)
    lam, lam_init = _lambda(lamp_ref)
    o0 = acc_s[0:tq, :] / l[0:tq, :]
    o1 = acc_s[tq:2 * tq, :] / l[tq:2 * tq, :]
    o_ref[...] = _diff_finish(o0, o1, lam, lam_init, ng_ref[...])


def _attn_prompt(q, k_all, v_all, lamp, norm_g, layer, nb, seq):
    n = q.shape[0]
    tq = _tile(seq, 512)
    nq = seq // tq
    return pl.pallas_call(
        _attn_prompt_kernel, out_shape=jax.ShapeDtypeStruct((n, DA_WIDTH), F32),
        grid=(nb, DA_HEADS, nq),
        in_specs=[pl.BlockSpec((tq, DA_DV), lambda b, h, i: (b * nq + i, h)),
                  pl.BlockSpec((None, None, DA_DV, seq), lambda b, h, i: (layer, b, h, 0)),
                  pl.BlockSpec((None, seq, DA_DV), lambda b, h, i: (layer, b, h)),
                  pl.BlockSpec((8, LANES), lambda b, h, i: (0, 0)),
                  pl.BlockSpec((1, DA_DV), lambda b, h, i: (0, 0))],
        out_specs=pl.BlockSpec((tq, DA_DV), lambda b, h, i: (b * nq + i, h)),
        scratch_shapes=[pltpu.VMEM((nq, DA_DV, tq), BF16), pltpu.VMEM((seq, DA_DV), BF16),
                        pltpu.VMEM((nq, 2 * tq, tq), F32), pltpu.VMEM((2 * tq, LANES), F32),
                        pltpu.VMEM((2 * tq, LANES), F32), pltpu.VMEM((2 * tq, DA_DV), F32)],
        compiler_params=_params("parallel", "parallel", "arbitrary"),
        name="attn_prompt")(q, k_all, v_all, lamp, norm_g)


def _attn_sample_kernel(pt_ref, q_ref, *refs, n_pages):
    kp = refs[:n_pages]
    vp = refs[n_pages:2 * n_pages]
    kn_ref, vn_ref, lamp_ref, ng_ref, o_ref, kt_s, v_s = refs[2 * n_pages:]
    t = q_ref.shape[0]
    nrow = 2 * DA_HEADS * t
    hrow = 2 * t

    qs = q_ref[...] * (DA_DH ** -0.5)
    qt = jnp.concatenate([qs] * (2 * DA_HEADS), axis=0)
    r = lax.broadcasted_iota(jnp.int32, (nrow, DA_QK), 0) // t
    c = lax.broadcasted_iota(jnp.int32, (nrow, DA_QK), 1) // DA_DH
    qbd = jnp.where(r == c, qt, 0.0).astype(BF16)

    pc = PAGE_SIZE * DA_HEADS
    for i in range(n_pages):
        kt_s[:, i * PAGE_SIZE:(i + 1) * PAGE_SIZE] = kp[i][...].astype(BF16)
        v_s[i * pc:(i + 1) * pc, :] = vp[i][...].astype(BF16)
    s_old = _dot(qbd, kt_s[...])
    kb = _pad_rows(kn_ref[...], PAGE_SIZE).astype(BF16)
    vb = _pad_rows(vn_ref[...], PAGE_SIZE).astype(BF16)
    r = lax.broadcasted_iota(jnp.int32, (nrow, PAGE_SIZE), 0) % t
    c = lax.broadcasted_iota(jnp.int32, (nrow, PAGE_SIZE), 1)
    s_new = jnp.where(c <= r, _dot_nt(qbd, kb), NEG_BIG)
    m = jnp.maximum(jnp.max(s_old, axis=-1, keepdims=True), jnp.max(s_new, axis=-1, keepdims=True))
    p_old = jnp.exp(s_old - m)
    p_new = jnp.exp(s_new - m)
    l = jnp.sum(p_old, axis=-1, keepdims=True) + jnp.sum(p_new, axis=-1, keepdims=True)
    pb_old = p_old.astype(BF16)
    pb_new = p_new.astype(BF16)

    spread = jnp.where(lax.broadcasted_iota(jnp.int32, (PAGE_SIZE, pc), 1) // DA_HEADS
                       == lax.broadcasted_iota(jnp.int32, (PAGE_SIZE, pc), 0), 1.0, 0.0).astype(BF16)
    own = (lax.broadcasted_iota(jnp.int32, (n_pages * nrow, pc), 1) % DA_HEADS
           == (lax.broadcasted_iota(jnp.int32, (n_pages * nrow, pc), 0) % nrow) // hrow)
    p_rows = jnp.concatenate([pb_old[:, i * PAGE_SIZE:(i + 1) * PAGE_SIZE] for i in range(n_pages)],
                             axis=0)
    pe = jnp.where(own, _dot(p_rows, spread), 0.0).astype(BF16)
    pe = jnp.concatenate([pe[i * nrow:(i + 1) * nrow, :] for i in range(n_pages)], axis=1)
    acc = _dot(pe, v_s[...])
    acc = acc + jnp.concatenate([_dot(pb_new[h * hrow:(h + 1) * hrow, :], vb[:, h * DA_DV:(h + 1) * DA_DV])
                                 for h in range(DA_HEADS)], axis=0)
    o = acc / l
    lam, lam_init = _lambda(lamp_ref)
    ng = ng_ref[...]
    for h in range(DA_HEADS):
        r0 = h * hrow
        o_ref[:, h * DA_DV:(h + 1) * DA_DV] = _diff_finish(o[r0:r0 + t, :], o[r0 + t:r0 + 2 * t, :],
                                                           lam, lam_init, ng)


def _attn_sample(q, k_new, v_new, cache_kt, cache_v, page_table, lamp, norm_g, layer, nb, seq):
    n = q.shape[0]
    n_pages = page_table.shape[1]
    row_map = lambda b, pt: (b, 0)
    new_map = lambda b, pt: (layer, b, 0)
    k_specs = [pl.BlockSpec((None, None, DA_QK, PAGE_SIZE),
                            lambda b, pt, i=i: (layer, pt[b * n_pages + i], 0, 0))
               for i in range(n_pages)]
    v_specs = [pl.BlockSpec((None, None, PAGE_SIZE * DA_HEADS, DA_DV),
                            lambda b, pt, i=i: (layer, pt[b * n_pages + i], 0, 0))
               for i in range(n_pages)]
    grid_spec = pltpu.PrefetchScalarGridSpec(
        num_scalar_prefetch=1, grid=(nb,),
        in_specs=[pl.BlockSpec((seq, DA_QK), row_map)] + k_specs + v_specs
        + [pl.BlockSpec((None, seq, DA_QK), new_map), pl.BlockSpec((None, seq, DA_WIDTH), new_map),
           pl.BlockSpec((8, LANES), lambda b, pt: (0, 0)),
           pl.BlockSpec((1, DA_DV), lambda b, pt: (0, 0))],
        out_specs=pl.BlockSpec((seq, DA_WIDTH), row_map),
        scratch_shapes=[pltpu.VMEM((DA_QK, n_pages * PAGE_SIZE), BF16),
                        pltpu.VMEM((n_pages * PAGE_SIZE * DA_HEADS, DA_DV), BF16)])
    return pl.pallas_call(
        functools.partial(_attn_sample_kernel, n_pages=n_pages),
        out_shape=jax.ShapeDtypeStruct((n, DA_WIDTH), F32),
        grid_spec=grid_spec, compiler_params=_params("parallel"),
        name="attn_sample")(page_table.reshape(-1), q, *([cache_kt] * n_pages),
                            *([cache_v] * n_pages), k_new, v_new, lamp, norm_g)


FF_CHUNK = 1024


def _mix_mlp_kernel(a1_ref, a2_ref, x_ref, wmix_ref, n1g_ref, n1b_ref, p_ref, up_ref, down_ref,
                    n2g_ref, n2b_ref, gate_ref, proj_ref, o_ref, *, even):
    if even:
        h = (_dot(a1_ref[...].astype(BF16), wmix_ref[0:HG_WIDTH, :])
             + _dot(a2_ref[...].astype(BF16), wmix_ref[HG_WIDTH:D_MODEL, :]))
    else:
        h = _dot((a1_ref[...] * a2_ref[...]).astype(BF16), wmix_ref[...])
    x = _layer_norm(ALPHA * x_ref[...] + h, n1g_ref[...], n1b_ref[...])
    xb = x.astype(BF16)
    acc = jnp.zeros(x.shape, F32)
    for c in range(D_FF // FF_CHUNK):
        cs = slice(c * FF_CHUNK, (c + 1) * FF_CHUNK)
        hid = jnp.maximum(_dot(xb, up_ref[:, cs]), 0.0)
        acc = acc + _dot((hid * hid).astype(BF16), down_ref[cs, :])
    x2 = _layer_norm(ALPHA * x + acc, n2g_ref[...], n2b_ref[...])
    gate = _sigmoid(_dot(x2.astype(BF16), gate_ref[...]))
    o_ref[...] = x2 + gate * _dot(p_ref[...].astype(BF16), proj_ref[...])


def _mix_mlp(a1, a2, x, wmix_bf, n1g, n1b, even, p_all, layer, up_bf, down_bf, n2g, n2b, gate_bf,
             proj_bf):
    n = x.shape[0]
    tm = _tile(n, 512)
    vec = pl.BlockSpec((1, D_MODEL), lambda i: (0, 0))
    return pl.pallas_call(
        functools.partial(_mix_mlp_kernel, even=even),
        out_shape=jax.ShapeDtypeStruct((n, D_MODEL), F32), grid=(n // tm,),
        in_specs=[_row_spec(tm, a1.shape[1]), _row_spec(tm, a2.shape[1]), _row_spec(tm, D_MODEL),
                  _const_spec((D_MODEL, D_MODEL)), vec, vec,
                  pl.BlockSpec((None, tm, PLE_DIM), lambda i: (layer, i, 0)),
                  _const_spec((D_MODEL, D_FF)), _const_spec((D_FF, D_MODEL)), vec, vec,
                  _const_spec((D_MODEL, D_MODEL)), _const_spec((PLE_DIM, D_MODEL))],
        out_specs=_row_spec(tm, D_MODEL),
        compiler_params=_params("parallel"), name="mix_mlp")(
            a1, a2, x, wmix_bf, n1g, n1b, p_all, up_bf, down_bf, n2g, n2b, gate_bf, proj_bf)


def _transpose_atom_grid(p):
    shape = p[0][0].shape
    sub = lax.broadcasted_iota(jnp.int32, shape, 0)
    half = lax.broadcasted_iota(jnp.int32, shape, 1) // (LANES // 2)
    s0 = sub & 1
    na, nc = len(p), len(p[0])

    def bit0(x):
        z = pltpu.roll(x, LANES // 2, 1)
        moved = jnp.where(s0 == 1, pltpu.roll(z, 1, 0), pltpu.roll(z, SUBLANES - 1, 0))
        return jnp.where(s0 == half, x, moved)

    p = [[bit0(t) for t in row] for row in p]
    for shift, cbit in ((2, 1), (4, 2)):
        up = (sub & shift) != 0
        for a in range(na):
            for c in range(nc):
                if c & cbit == 0:
                    lo, hi = p[a][c], p[a][c + cbit]
                    p[a][c] = jnp.where(up, pltpu.roll(hi, shift, 0), lo)
                    p[a][c + cbit] = jnp.where(up, hi, pltpu.roll(lo, SUBLANES - shift, 0))
    q = nc // na
    return [[p[c // q][a2 * q + c % q] for c in range(nc)] for a2 in range(na)]


def _tokens_to_chains(x, nseq, tt, out_ref):
    sub = LANES // nseq
    ntile = D_MODEL // LANES
    for sb in range(tt // sub):
        grids = []
        for b in range(nseq):
            r0 = b * tt + sb * sub
            grids.append(_transpose_atom_grid(
                [[x[r0 + a * SUBLANES:r0 + (a + 1) * SUBLANES, c * LANES:(c + 1) * LANES]
                  for c in range(ntile)] for a in range(sub // SUBLANES)]))
        for tau in range(ntile):
            tile = jnp.concatenate([g[a][tau] for g in grids for a in range(sub // SUBLANES)],
                                   axis=0).T
            out_ref[sb * sub + 2 * tau] = tile[0:RW_HEAD, :]
            out_ref[sb * sub + 2 * tau + 1] = tile[RW_HEAD:2 * RW_HEAD, :]


def _rwkv_pre_kernel(*refs, has_vfirst, chains, want_v_tokens):
    if chains:
        x_ref, x8_ref, sh_ref = refs[:3]
        refs = refs[3:]
    else:
        x_ref, xp_ref = refs[:2]
        refs = refs[2:]
    vf_ref = vt_o = v0_ref = v1_ref = v2_ref = None
    if has_vfirst:
        vf_ref, refs = refs[0], refs[1:]
    (mix_ref, wr_ref, wk_ref, wv_ref, w0_ref, w1_ref, w2_ref, a0_ref, a1_ref, a2_ref) = refs[:10]
    refs = refs[10:]
    if has_vfirst:
        v0_ref, v1_ref, v2_ref = refs[:3]
        refs = refs[3:]
    g1_ref, g2_ref, r_o, k_o, v_o, g_o, dec_o, a_o = refs[:8]
    if want_v_tokens:
        vt_o = refs[8]
    if chains:
        nseq, tt = x_ref.shape[0], x_ref.shape[1]
        x = x_ref[...].reshape(nseq * tt, D_MODEL)
        first = pl.program_id(0) == 0
        before = jnp.where(first, sh_ref[...], x8_ref[:, SUBLANES - 1, :])
        before = jnp.concatenate([jnp.broadcast_to(before[b:b + 1, :], (tt, D_MODEL))
                                  for b in range(nseq)], axis=0)
        rowid = lax.broadcasted_iota(jnp.int32, x.shape, 0)
        x_prev = jnp.where(rowid % tt == 0, before, pltpu.roll(x, 1, 0))
    else:
        x = x_ref[...]
        x_prev = xp_ref[...]
    xx = x_prev - x
    xr, xw, xk, xv, xa, xg = ((x + xx * mix_ref[m:m + 1, :]).astype(BF16) for m in range(6))
    r = _dot(xr, wr_ref[...])
    k = _dot(xk, wk_ref[...])
    v = _dot(xv, wv_ref[...])
    lw = w0_ref[...] + _dot(jnp.tanh(_dot(xw, w1_ref[...])).astype(BF16), w2_ref[...])
    w = -(jnp.maximum(-lw, 0.0) + jnp.log(1.0 + jnp.exp(-jnp.abs(lw)))) - 0.5
    if has_vfirst:
        mixv = _sigmoid(v0_ref[...] + _dot(_dot(xv, v1_ref[...]).astype(BF16), v2_ref[...]))
        vf = vf_ref[...].reshape(v.shape) if chains else vf_ref[...]
        v = v + (vf - v) * mixv
    a = _sigmoid(a0_ref[...] + _dot(_dot(xa, a1_ref[...]).astype(BF16), a2_ref[...]))
    g = _dot(_sigmoid(_dot(xg, g1_ref[...])).astype(BF16), g2_ref[...])
    dec = jnp.exp(-jnp.exp(w))
    if chains:
        g_o[...] = g.reshape(g_o.shape)
        if want_v_tokens:
            vt_o[...] = v.reshape(vt_o.shape)
        for val, ref in ((r, r_o), (k, k_o), (v, v_o), (dec, dec_o), (a, a_o)):
            _tokens_to_chains(val, nseq, tt, ref)
    else:
        r_o[...] = r
        k_o[...] = k
        v_o[...] = v
        g_o[...] = g
        dec_o[...] = dec
        a_o[...] = a
        if want_v_tokens:
            vt_o[...] = v


def _chain_layout_ok(nb, seq):
    return nb * RW_HEADS == LANES and LANES // nb == RW_HEADS and seq % RW_TBLK == 0


def _rwkv_pre(x, shift0, v_first, wts, nb, seq, want_v_tokens):
    n = x.shape[0]
    has_vfirst = v_first is not None
    chains = _chain_layout_ok(nb, seq)
    vec = pl.BlockSpec((1, D_MODEL), lambda i: (0, 0))
    if chains:
        tt = RW_TBLK
        tok3 = pl.BlockSpec((nb, tt, D_MODEL), lambda i: (0, i, 0))
        x3 = x.reshape(nb, seq, D_MODEL)
        r8 = tt // SUBLANES
        acts = [x3, x3, shift0]
        act_specs = [tok3,
                     pl.BlockSpec((nb, SUBLANES, D_MODEL), lambda i: (0, jnp.maximum(i * r8 - 1, 0), 0)),
                     pl.BlockSpec((nb, D_MODEL), lambda i: (0, 0))]
        if has_vfirst:
            acts.append(v_first.reshape(nb, seq, D_MODEL))
            act_specs.append(tok3)
        grid = (seq // tt,)
        chain_out = jax.ShapeDtypeStruct((seq, RW_HEAD, LANES), F32)
        chain_spec = pl.BlockSpec((tt, RW_HEAD, LANES), lambda i: (i, 0, 0))
        tok_out = jax.ShapeDtypeStruct((nb, seq, D_MODEL), F32)
        out_shape = [chain_out, chain_out, chain_out, tok_out, chain_out, chain_out]
        out_specs = [chain_spec, chain_spec, chain_spec, tok3, chain_spec, chain_spec]
        if want_v_tokens:
            out_shape.append(tok_out)
            out_specs.append(tok3)
    else:
        tm = _tile(n, 256)
        xs = x.reshape(nb, seq, D_MODEL)
        x_prev = jnp.concatenate([shift0[:, None, :], xs[:, :-1]], axis=1).reshape(n, D_MODEL)
        acts = [x, x_prev] + ([v_first] if has_vfirst else [])
        act_specs = [_row_spec(tm, D_MODEL)] * len(acts)
        grid = (n // tm,)
        out_shape = [jax.ShapeDtypeStruct((n, D_MODEL), F32)] * 6
        out_specs = [_row_spec(tm, D_MODEL)] * 6
    names = ['mix', 'w_r', 'w_k', 'w_v', 'w0', 'w1', 'w2', 'a0', 'a1', 'a2']
    if has_vfirst:
        names += ['v0', 'v1', 'v2']
    names += ['g1', 'g2']
    ws = [wts[nm] for nm in names]
    w_specs = [vec if w.shape == (1, D_MODEL) else _const_spec(w.shape) for w in ws]
    outs = pl.pallas_call(
        functools.partial(_rwkv_pre_kernel, has_vfirst=has_vfirst, chains=chains,
                          want_v_tokens=want_v_tokens and chains),
        out_shape=tuple(out_shape), grid=grid,
        in_specs=act_specs + w_specs, out_specs=tuple(out_specs),
        compiler_params=_params("parallel"), name="rwkv_pre")(*acts, *ws)
    outs = list(outs)
    if chains:
        outs[3] = outs[3].reshape(n, D_MODEL)
        if want_v_tokens:
            outs[6] = outs[6].reshape(n, D_MODEL)
    elif want_v_tokens:
        outs.append(outs[2])
    return outs


RW_TBLK = 32


RW_IN = 5


def _rwkv_token(ins, v_rows, prm, s_s, y_s):
    r, k, v, dec, ag = ins
    kk_p, ka_p, rk_p, lg_p, lb_p = prm
    kk = k * kk_p
    kk = kk / jnp.maximum(jnp.sqrt(jnp.sum(kk * kk, axis=0, keepdims=True)), 1e-12)
    a = -kk
    b = kk * ag
    k2 = k * (1.0 + (ag - 1.0) * ka_p)
    wr = dec * r
    br = jnp.sum(b * r, axis=0, keepdims=True)
    kr = jnp.sum(k2 * r, axis=0, keepdims=True)

    def rows(i8, _):
        i0 = pl.multiple_of(i8 * SUBLANES, SUBLANES)
        v8 = v_rows(i0)
        ys = []
        for ii in range(SUBLANES):
            si = s_s[i0 + ii]
            sa = jnp.sum(si * a, axis=0, keepdims=True)
            qs = jnp.sum(si * wr, axis=0, keepdims=True)
            vi = v8[ii:ii + 1, :]
            s_s[i0 + ii] = si * dec + sa * b + vi * k2
            ys.append(qs + sa * br + vi * kr)
        y_s[pl.ds(i0, SUBLANES), :] = jnp.concatenate(ys, axis=0)
        return 0

    lax.fori_loop(0, RW_HEAD // SUBLANES, rows, 0)
    y = y_s[...]
    mu = jnp.mean(y, axis=0, keepdims=True)
    yc = y - mu
    var = jnp.mean(yc * yc, axis=0, keepdims=True)
    yn = yc * lax.rsqrt(var + GN_EPS) * lg_p + lb_p
    return yn + jnp.sum(r * k2 * rk_p, axis=0, keepdims=True) * v


RW_IBLK = 32
RW_ACC = 2


def _rwkv_token_keymajor(ins, v_rows, prm, s_s, y_s, vec_s, g_s):
    r, k, v, dec, ag = ins
    kk_p, ka_p, rk_p, lg_p, lb_p = prm
    kk = k * kk_p
    kk = kk / jnp.maximum(jnp.sqrt(jnp.sum(kk * kk, axis=0, keepdims=True)), 1e-12)
    b = kk * ag
    k2 = k * (1.0 + (ag - 1.0) * ka_p)
    br = jnp.sum(b * r, axis=0, keepdims=True)
    kr = jnp.sum(k2 * r, axis=0, keepdims=True)
    g_old = g_s[...]
    g_new = g_old * dec
    g_s[...] = g_new
    g_inv = 1.0 / g_new
    for idx, val in enumerate((-kk * g_old, r * g_new, b * g_inv, k2 * g_inv)):
        vec_s[idx] = val
    row = lambda idx, j: vec_s[idx, j:j + 1, :]

    for p in range(RW_HEAD // RW_IBLK):
        rs = pl.ds(p * RW_IBLK, RW_IBLK)
        sa = [None] * RW_ACC
        qs = [None] * RW_ACC
        for j in range(RW_HEAD):
            sj = s_s[j, rs, :]
            c = j % RW_ACC
            sa[c] = sj * row(0, j) if sa[c] is None else sa[c] + sj * row(0, j)
            qs[c] = sj * row(1, j) if qs[c] is None else qs[c] + sj * row(1, j)
        sa = functools.reduce(lambda x, y: x + y, sa)
        qs = functools.reduce(lambda x, y: x + y, qs)
        vi = v_rows(rs)
        for j in range(RW_HEAD):
            s_s[j, rs, :] = s_s[j, rs, :] + sa * row(2, j) + vi * row(3, j)
        y_s[rs, :] = qs + sa * br + vi * kr
    y = y_s[...]
    mu = jnp.mean(y, axis=0, keepdims=True)
    yc = y - mu
    var = jnp.mean(yc * yc, axis=0, keepdims=True)
    yn = yc * lax.rsqrt(var + GN_EPS) * lg_p + lb_p
    return yn + jnp.sum(r * k2 * rk_p, axis=0, keepdims=True) * v


def _rwkv_scan_kernel(r_ref, k_ref, v_ref, dec_ref, a_ref, kk_ref, ka_ref, rk_ref, lg_ref, lb_ref,
                      s0_ref, y_ref, sout_ref, s_s, y_s):
    tb = pl.program_id(1)
    in_refs = (r_ref, k_ref, v_ref, dec_ref, a_ref)
    prm_refs = (kk_ref, ka_ref, rk_ref, lg_ref, lb_ref)

    @pl.when(tb == 0)
    def _():
        s_s[...] = s0_ref[...]

    def step(t, _):
        y_ref[t] = _rwkv_token([ref[t] for ref in in_refs],
                               lambda i0: v_ref[t, pl.ds(i0, SUBLANES), :],
                               [p[...] for p in prm_refs], s_s, y_s)
        return 0

    lax.fori_loop(0, r_ref.shape[0], step, 0)

    @pl.when(tb == pl.num_programs(1) - 1)
    def _():
        sout_ref[...] = s_s[...]


def _rwkv_scan_tokens_kernel(r_ref, k_ref, v_ref, dec_ref, a_ref, kk_ref, ka_ref, rk_ref, lg_ref,
                             lb_ref, s0_ref, y_ref, sout_ref, s_s, y_s, yt_s, vec_s, g_s):
    tb = pl.program_id(1)
    nseq, tt = y_ref.shape[0], y_ref.shape[1]
    sub = LANES // nseq
    in_refs = (r_ref, k_ref, v_ref, dec_ref, a_ref)
    prm_refs = (kk_ref, ka_ref, rk_ref, lg_ref, lb_ref)

    @pl.when(tb == 0)
    def _():
        s_s[...] = s0_ref[...]
        g_s[...] = jnp.ones(g_s.shape, F32)

    def sub_block(sb, _):
        t0 = pl.multiple_of(sb * sub, sub)

        def step(t, _):
            yt_s[t] = _rwkv_token_keymajor([ref[t0 + t] for ref in in_refs],
                                           lambda rs: v_ref[t0 + t, rs, :],
                                           [p[...] for p in prm_refs], s_s, y_s, vec_s, g_s)
            return 0

        lax.fori_loop(0, sub, step, 0)
        for j in range(RW_HEAD):
            s_s[j] = s_s[j] * g_s[j:j + 1, :]
        g_s[...] = jnp.ones(g_s.shape, F32)
        ntile = D_MODEL // LANES
        na = sub // SUBLANES
        tiles = [jnp.concatenate([yt_s[2 * tau], yt_s[2 * tau + 1]], axis=0).T for tau in range(ntile)]
        for b in range(nseq):
            grid = _transpose_atom_grid(
                [[tiles[tau][b * sub + a * SUBLANES:b * sub + (a + 1) * SUBLANES, :]
                  for tau in range(ntile)] for a in range(na)])
            for a in range(na):
                for c in range(ntile):
                    y_ref[b, pl.ds(t0 + a * SUBLANES, SUBLANES), c * LANES:(c + 1) * LANES] = grid[a][c]
        return 0

    lax.fori_loop(0, tt // sub, sub_block, 0)

    @pl.when(tb == pl.num_programs(1) - 1)
    def _():
        sout_ref[...] = s_s[...]


def _rwkv_scan(r, k, v, dec, a, s0_all, s_layer, wts, nb, seq):
    nl = s0_all.shape[0]
    n = nb * seq
    hd = RW_HEAD
    batch_lanes = nb % LANES == 0
    if batch_lanes:
        nbg = nb // LANES
        groups, chains = RW_HEADS * nbg, LANES
        to_l = lambda x: (x.reshape(nbg, LANES, seq, RW_HEADS, hd).transpose(0, 3, 2, 4, 1)
                          .reshape(groups, seq, hd, chains))
        from_l = lambda y: (y.reshape(nbg, RW_HEADS, seq, hd, LANES).transpose(0, 4, 2, 1, 3)
                            .reshape(n, D_MODEL))
        par = lambda p: jnp.broadcast_to(p.reshape(1, RW_HEADS, hd, 1),
                                         (nbg, RW_HEADS, hd, chains)).reshape(groups, hd, chains)
        st_in = lambda s: (s.reshape(nl, nbg, LANES, RW_HEADS, hd, hd).transpose(0, 1, 3, 4, 5, 2)
                           .reshape(nl, groups, hd, hd, chains))
        st_out = lambda s: (s.reshape(nbg, RW_HEADS, hd, hd, LANES).transpose(0, 4, 1, 2, 3)
                            .reshape(nb, RW_HEADS, hd, hd))
    else:
        groups, chains = 1, nb * RW_HEADS
        to_l = lambda x: (x.reshape(nb, seq, RW_HEADS, hd).transpose(1, 3, 0, 2)
                          .reshape(1, seq, hd, chains))
        from_l = lambda y: (y.reshape(seq, hd, nb, RW_HEADS).transpose(2, 0, 3, 1)
                            .reshape(n, D_MODEL))
        par = lambda p: jnp.broadcast_to(p.reshape(RW_HEADS, hd).T.reshape(1, hd, 1, RW_HEADS),
                                         (1, hd, nb, RW_HEADS)).reshape(1, hd, chains)
        st_in = lambda s: s.transpose(0, 3, 4, 1, 2).reshape(nl, 1, hd, hd, chains)
        st_out = lambda s: s.reshape(hd, hd, nb, RW_HEADS).transpose(2, 3, 0, 1)
    tt = _tile(seq, RW_TBLK)
    prm = pl.BlockSpec((None, hd, chains), lambda g, t: (g, 0, 0))
    st = pl.BlockSpec((None, hd, hd, chains), lambda g, t: (g, 0, 0, 0))
    st0 = pl.BlockSpec((None, None, hd, hd, chains), lambda g, t: (s_layer, g, 0, 0, 0))
    prms = [par(wts[nm]) for nm in ('k_k', 'k_a', 'r_k', 'lnx_g', 'lnx_b')]
    if _chain_layout_ok(nb, seq):
        tok = pl.BlockSpec((nb, tt, D_MODEL), lambda g, t: (0, t, 0))
        act = pl.BlockSpec((tt, hd, chains), lambda g, t: (t, 0, 0))
        y, s_new = pl.pallas_call(
            _rwkv_scan_tokens_kernel,
            out_shape=(jax.ShapeDtypeStruct((nb, seq, D_MODEL), F32),
                       jax.ShapeDtypeStruct((1, hd, hd, chains), F32)),
            grid=(1, seq // tt),
            in_specs=[act] * RW_IN + [prm] * 5 + [st0],
            out_specs=(tok, st),
            scratch_shapes=[pltpu.VMEM((hd, hd, chains), F32), pltpu.VMEM((hd, chains), F32),
                            pltpu.VMEM((RW_HEADS, hd, chains), F32),
                            pltpu.VMEM((4, hd, chains), F32), pltpu.VMEM((hd, chains), F32)],
            compiler_params=_params("parallel", "arbitrary"),
            name="rwkv_scan_tokens")(r, k, v, dec, a, *prms, st_in(s0_all).swapaxes(2, 3))
        return y.reshape(n, D_MODEL), st_out(s_new.swapaxes(1, 2))
    act = pl.BlockSpec((None, tt, hd, chains), lambda g, t: (g, t, 0, 0))
    y, s_new = pl.pallas_call(
        _rwkv_scan_kernel,
        out_shape=(jax.ShapeDtypeStruct((groups, seq, hd, chains), F32),
                   jax.ShapeDtypeStruct((groups, hd, hd, chains), F32)),
        grid=(groups, seq // tt),
        in_specs=[act] * RW_IN + [prm] * 5 + [st0],
        out_specs=(act, st),
        scratch_shapes=[pltpu.VMEM((hd, hd, chains), F32), pltpu.VMEM((hd, chains), F32)],
        compiler_params=_params("parallel", "arbitrary"),
        name="rwkv_scan")(to_l(r), to_l(k), to_l(v), to_l(dec), to_l(a), *prms, st_in(s0_all))
    return from_l(y), st_out(s_new)


def _trunk(x3, p4, hg_s0, wkv_s0, shift0, cache_k, cache_v, page_table, W):
    nb, seq, _ = x3.shape
    n = nb * seq
    x = x3.reshape(n, D_MODEL)
    new_hg, new_wkv, new_shift = [], [], []
    kv = None
    v_first = None
    for i in range(DEPTH):
        j = i // 2
        if i % 2 == 0:
            wk_t = W['wk_t'][j] if page_table is None else None
            h4, q, k_all, v_all = _even_in(x, W['w_in_even'][j], wk_t, j, kv, nb, seq)
            kv = (k_all, v_all)
            oh, s_hg = _hgrn(h4, hg_s0, min(j, hg_s0.shape[0] - 1), W['hg_lb_logits'],
                             W['hg_norm_g'][j], j, nb, seq)
            if page_table is None:
                od = _attn_prompt(q, k_all, v_all, W['lamp'][j], W['da_norm_g'][j], j, nb, seq)
            else:
                od = _attn_sample(q, k_all, v_all, cache_k, cache_v, page_table, W['lamp'][j],
                                  W['da_norm_g'][j], j, nb, seq)
            mix = (oh, od, W['w_out_even'][j])
            new_hg.append(s_hg)
        else:
            need_vf = j == 0 and N_ODD > 1
            outs = _rwkv_pre(x, shift0[j], v_first, W['rw'][j], nb, seq, need_vf)
            r, k, v, g, dec, a = outs[:6]
            if need_vf:
                v_first = outs[6]
            y, s_wkv = _rwkv_scan(r, k, v, dec, a, wkv_s0, min(j, wkv_s0.shape[0] - 1),
                                  W['rw_scan'][j], nb, seq)
            new_shift.append(x.reshape(nb, seq, D_MODEL)[:, -1])
            new_wkv.append(s_wkv)
            mix = (y, g, W['rw_w_o'][j])
        x = _mix_mlp(mix[0], mix[1], x, mix[2], W['ln1_g'][i], W['ln1_b'][i], i % 2 == 0,
                     p4.reshape(DEPTH, n, PLE_DIM), i, W['mlp_up'][i], W['mlp_down'][i],
                     W['ln2_g'][i], W['ln2_b'][i], W['ple_gate'][i], W['ple_proj'][i])
    if page_table is None:
        k_out = kv[0].reshape(N_EVEN, nb, DA_HEADS, 2, DA_DH, seq).transpose(0, 1, 5, 2, 3, 4)
    else:
        k_out = kv[0].reshape(N_EVEN, nb, seq, DA_HEADS, 2, DA_DH)
    return (x.reshape(nb, seq, D_MODEL), k_out,
            kv[1].reshape(N_EVEN, nb, seq, DA_HEADS, DA_DV), jnp.stack(new_hg),
            jnp.stack(new_wkv), jnp.stack(new_shift))


def kernel(x_prompt, x_sample, cache_k, cache_v, state_hgrn, state_wkv, state_shift, page_table,
           p_prompt, p_sample, w_in_even, w_out_even, hg_lb_logits, hg_norm_g,
           da_lam_q1, da_lam_k1, da_lam_q2, da_lam_k2, da_norm_g,
           rw_mix, rw_w_r, rw_w_k, rw_w_v, rw_w_o, rw_w0, rw_w1, rw_w2,
           rw_a0, rw_a1, rw_a2, rw_v0, rw_v1, rw_v2, rw_g1, rw_g2,
           rw_k_k, rw_k_a, rw_r_k, rw_lnx_g, rw_lnx_b,
           ln1_g, ln1_b, ln2_g, ln2_b, mlp_up, mlp_down, ple_proj, ple_gate):
    bf = lambda w: w.astype(BF16)
    vec = lambda w: w.reshape(1, -1)

    def lam_rows(j):
        pad = lambda a: jnp.pad(a.astype(F32), (0, LANES - a.shape[0]))
        lam_init = 0.8 - 0.6 * math.exp(-0.3 * (2 * j))
        rows = [pad(da_lam_q1[j]), pad(da_lam_k1[j]), pad(da_lam_q2[j]), pad(da_lam_k2[j]),
                jnp.full((LANES,), lam_init, F32)]
        return jnp.stack(rows + [jnp.zeros((LANES,), F32)] * 3)

    rw = []
    for j in range(N_ODD):
        d = dict(mix=rw_mix[j], w_r=bf(rw_w_r[j]), w_k=bf(rw_w_k[j]), w_v=bf(rw_w_v[j]),
                 w0=vec(rw_w0[j]), w1=bf(rw_w1[j]), w2=bf(rw_w2[j]),
                 a0=vec(rw_a0[j]), a1=bf(rw_a1[j]), a2=bf(rw_a2[j]),
                 g1=bf(rw_g1[j]), g2=bf(rw_g2[j]))
        if j > 0:
            d.update(v0=vec(rw_v0[j - 1]), v1=bf(rw_v1[j - 1]), v2=bf(rw_v2[j - 1]))
        rw.append(d)
    W = dict(
        w_in_even=[bf(w_in_even[j]) for j in range(N_EVEN)],
        wk_t=[bf(w_in_even[j][:, 4 * HG_WIDTH + DA_QK:4 * HG_WIDTH + 2 * DA_QK].T) for j in range(N_EVEN)],
        w_out_even=[bf(w_out_even[j]) for j in range(N_EVEN)],
        hg_lb_logits=hg_lb_logits,
        hg_norm_g=[vec(hg_norm_g[j]) for j in range(N_EVEN)],
        da_norm_g=[vec(da_norm_g[j]) for j in range(N_EVEN)],
        lamp=[lam_rows(j) for j in range(N_EVEN)],
        rw=rw,
        rw_scan=[dict(k_k=rw_k_k[j], k_a=rw_k_a[j], r_k=rw_r_k[j], lnx_g=rw_lnx_g[j],
                      lnx_b=rw_lnx_b[j]) for j in range(N_ODD)],
        rw_w_o=[bf(rw_w_o[j]) for j in range(N_ODD)],
        ln1_g=[vec(ln1_g[i]) for i in range(DEPTH)], ln1_b=[vec(ln1_b[i]) for i in range(DEPTH)],
        ln2_g=[vec(ln2_g[i]) for i in range(DEPTH)], ln2_b=[vec(ln2_b[i]) for i in range(DEPTH)],
        mlp_up=[bf(mlp_up[i]) for i in range(DEPTH)], mlp_down=[bf(mlp_down[i]) for i in range(DEPTH)],
        ple_gate=[bf(ple_gate[i]) for i in range(DEPTH)], ple_proj=[bf(ple_proj[i]) for i in range(DEPTH)],
    )
    nbp = x_prompt.shape[0]
    dt = x_prompt.dtype
    hg0 = jnp.zeros((1, nbp, HG_HEADS, HG_DK, HG_DK), dt)
    wkv0 = jnp.zeros((1, nbp, RW_HEADS, RW_HEAD, RW_HEAD), dt)
    sh0 = jnp.zeros((N_ODD, nbp, D_MODEL), dt)
    y_p, k_p, v_p, hg_p, wkv_p, sh_p = _trunk(x_prompt, p_prompt, hg0, wkv0, sh0, None, None, None, W)
    ckt = jnp.transpose(cache_k, (0, 1, 3, 4, 5, 2)).reshape(
        cache_k.shape[0], cache_k.shape[1], DA_QK, PAGE_SIZE)
    cv = cache_v.reshape(cache_v.shape[0], cache_v.shape[1], PAGE_SIZE * DA_HEADS, DA_DV)
    y_s, k_s, v_s, hg_s, wkv_s, sh_s = _trunk(x_sample, p_sample, state_hgrn, state_wkv, state_shift,
                                              ckt, cv, page_table, W)
    return (y_p, y_s, k_p, v_p, k_s, v_s, hg_p, hg_s, wkv_p, wkv_s, sh_p, sh_s)
```

```python
import functools
import math

import jax
import jax.numpy as jnp
from jax import lax
from jax.experimental import pallas as pl
from jax.experimental.pallas import tpu as pltpu

F32 = jnp.float32
BF16 = jnp.bfloat16

D_MODEL = 1024
DEPTH = 4
PAGE_SIZE = 128
N_EVEN = (DEPTH + 1) // 2
N_ODD = DEPTH // 2
PLE_DIM = 256
D_FF = 4 * D_MODEL
HG_WIDTH = D_MODEL // 2
HG_HEADS = 4
HG_DK = HG_WIDTH // HG_HEADS
DA_WIDTH = D_MODEL - HG_WIDTH
DA_HEADS = 4
DA_DV = DA_WIDTH // DA_HEADS
DA_DH = DA_DV // 2
DA_QK = DA_HEADS * 2 * DA_DH
IN_COLS = 4 * HG_WIDTH + 2 * DA_QK + DA_WIDTH
RW_HEAD = 64
RW_HEADS = D_MODEL // RW_HEAD
ALPHA = (2.0 * DEPTH) ** 0.25
LN_EPS = 1e-5
RMS_EPS = 1e-6
GN_EPS = 64e-5
F_MIN = 1e-30
NEG_BIG = -1e30

LANES = 128
SUBLANES = 8
VMEM_LIMIT = 52 * 1024 * 1024
HG_CHUNK = 16
MM_ROWS = 16


def _params(*sem):
    return pltpu.CompilerParams(dimension_semantics=sem, vmem_limit_bytes=VMEM_LIMIT)


def _const_spec(shape):
    nd = len(shape)
    return pl.BlockSpec(shape, lambda *_: (0,) * nd, pipeline_mode=pl.Buffered(1))


def _row_spec(rows, cols):
    return pl.BlockSpec((rows, cols), lambda i: (i, 0))


def _tile(n, pref):
    t = min(n, pref)
    while n % t:
        t //= 2
    return t


def _dot(a, b):
    return jnp.dot(a, b, preferred_element_type=F32)


def _dot_nt(a, b):
    return lax.dot_general(a, b, (((1,), (1,)), ((), ())), preferred_element_type=F32)


def _dot_tn(a, b):
    return lax.dot_general(a, b, (((0,), (0,)), ((), ())), preferred_element_type=F32)


def _split3(x):
    h = x.astype(BF16)
    r = x - h.astype(F32)
    m = r.astype(BF16)
    l = (r - m.astype(F32)).astype(BF16)
    return h, m, l


def _dot_x3(x, rhs_bf16):
    h, m, l = _split3(x)
    return _dot(h, rhs_bf16) + _dot(m, rhs_bf16) + _dot(l, rhs_bf16)


def _dot_x3_lhs(lhs_bf16, x):
    h, m, l = _split3(x)
    return _dot(lhs_bf16, h) + _dot(lhs_bf16, m) + _dot(lhs_bf16, l)


def _layer_norm(z, g, b):
    mu = jnp.mean(z, axis=-1, keepdims=True)
    zc = z - mu
    var = jnp.mean(zc * zc, axis=-1, keepdims=True)
    return zc * lax.rsqrt(var + LN_EPS) * g + b


def _sigmoid(x):
    return 1.0 / (1.0 + jnp.exp(-x))


def _head_ones(width):
    r = lax.broadcasted_iota(jnp.int32, (LANES, LANES), 0) // width
    c = lax.broadcasted_iota(jnp.int32, (LANES, LANES), 1) // width
    return jnp.where(r == c, 1.0, 0.0).astype(BF16)


def _segsum(x, ones):
    cols = x.shape[1] // LANES
    parts = [_dot_x3(x[:, c * LANES:(c + 1) * LANES], ones) for c in range(cols)]
    return parts[0] if cols == 1 else jnp.concatenate(parts, axis=1)


def _even_in_kernel(x_ref, w_ref, *refs, k_transposed, layer):
    h4_ref, q_ref, k_ref, v_ref = refs[-4:]
    xb = x_ref[...].astype(BF16)
    cw = HG_WIDTH
    for c in range(4):
        h4_ref[:, c * cw:(c + 1) * cw] = _dot(xb, w_ref[:, c * cw:(c + 1) * cw])
    q0 = 4 * HG_WIDTH
    q_ref[...] = _dot(xb, w_ref[:, q0:q0 + DA_QK])
    if k_transposed:
        k_ref[layer] = _dot_nt(refs[0][...], xb)
    else:
        k_ref[layer] = _dot(xb, w_ref[:, q0 + DA_QK:q0 + 2 * DA_QK])
    v_ref[layer] = _dot(xb, w_ref[:, q0 + 2 * DA_QK:q0 + 2 * DA_QK + DA_WIDTH])
    if layer:
        k_prev, v_prev = refs[-6:-4]
        k_ref[0:layer] = k_prev[...]
        v_ref[0:layer] = v_prev[...]


def _even_in(x, w_in_bf, wk_t_bf, layer, kv_stacks, nb, seq):
    n = x.shape[0]
    k_transposed = wk_t_bf is not None
    tm = _tile(seq if k_transposed else n, 512)
    in_specs = [_row_spec(tm, D_MODEL), _const_spec((D_MODEL, IN_COLS))]
    args = [x, w_in_bf]
    v_spec = lambda slabs: pl.BlockSpec((slabs, tm, DA_WIDTH), lambda i: (0, i, 0))
    if k_transposed:
        tps = seq // tm
        k_shape = (nb, DA_QK, seq)
        k_spec = lambda slabs: pl.BlockSpec((slabs, None, DA_QK, tm),
                                            lambda i: (0, i // tps, 0, i % tps))
        in_specs.append(_const_spec((DA_QK, D_MODEL)))
        args.append(wk_t_bf)
    else:
        k_shape = (n, DA_QK)
        k_spec = lambda slabs: pl.BlockSpec((slabs, tm, DA_QK), lambda i: (0, i, 0))
    outs = (jax.ShapeDtypeStruct((n, 4 * HG_WIDTH), F32), jax.ShapeDtypeStruct((n, DA_QK), F32),
            jax.ShapeDtypeStruct((layer + 1,) + k_shape, F32),
            jax.ShapeDtypeStruct((layer + 1, n, DA_WIDTH), F32))
    if layer:
        in_specs += [k_spec(layer), v_spec(layer)]
        args += list(kv_stacks)
    return pl.pallas_call(
        functools.partial(_even_in_kernel, k_transposed=k_transposed, layer=layer), out_shape=outs,
        grid=(n // tm,), in_specs=in_specs,
        out_specs=(_row_spec(tm, 4 * HG_WIDTH), _row_spec(tm, DA_QK), k_spec(layer + 1),
                   v_spec(layer + 1)),
        compiler_params=_params("parallel"), name="even_in")(*args)


def _pad_rows(x, rows):
    if x.shape[0] >= rows:
        return x
    return jnp.concatenate([x, jnp.zeros((rows - x.shape[0], x.shape[1]), x.dtype)], axis=0)


def _hgrn_kernel(h4_ref, s0_ref, lbl_ref, ng_ref, o_ref, sout_ref,
                 st_ref, q_s, k_s, b_s, u_s, *, layer, chunk, carry):
    tl = h4_ref.shape[0]
    nchunk = tl // chunk
    w = HG_WIDTH

    rows = [lbl_ref[i:i + 1, :] for i in range(N_EVEN)]
    mx = functools.reduce(jnp.maximum, rows)
    es = [jnp.exp(r - mx) for r in rows]
    den = functools.reduce(lambda a, b: a + b, es)
    sm = [e / den for e in es]
    cs = functools.reduce(lambda a, b: a + b, sm[:layer + 1])
    lb = cs - sm[0]

    hq = h4_ref[:, 0:w]
    hf = h4_ref[:, w:2 * w]
    q_s[...] = hq * _sigmoid(hq)
    f = lb + (1.0 - lb) * _sigmoid(hf)
    g = jnp.log(jnp.maximum(f, F_MIN))
    k_s[...] = 1.0 - f
    ri = lax.broadcasted_iota(jnp.int32, (tl, tl), 0)
    ci = lax.broadcasted_iota(jnp.int32, (tl, tl), 1)
    tril = jnp.where((ri // chunk == ci // chunk) & (ci <= ri), 1.0, 0.0).astype(BF16)
    b_s[...] = _dot_x3_lhs(tril, g)

    if carry:
        @pl.when(pl.program_id(1) == 0)
        def _():
            for h in range(HG_HEADS):
                st_ref[h] = s0_ref[0, h].T

    rowid = lax.broadcasted_iota(jnp.int32, (chunk, 1), 0)
    ng = ng_ref[...]

    def chunk_slices(c, h):
        rs = pl.ds(pl.multiple_of(c * chunk, chunk), chunk)
        cs_ = slice(h * HG_DK, (h + 1) * HG_DK)
        v_cols = slice(2 * w + h * HG_DK, 2 * w + (h + 1) * HG_DK)
        return rs, cs_, v_cols

    def state_free(c, _):
        for h in range(HG_HEADS):
            rs, cs_, v_cols = chunk_slices(c, h)
            q, k, b, v = q_s[rs, cs_], k_s[rs, cs_], b_s[rs, cs_], h4_ref[rs, v_cols]
            groups = [(g0, min(g0 + SUBLANES, chunk)) for g0 in range(0, chunk, SUBLANES)]
            o_g = [None] * len(groups)
            for s in range(chunk):
                for gi, (g0, g1) in enumerate(groups):
                    if g1 <= s:
                        continue
                    rel = b[g0:g1, :] - b[s:s + 1, :]
                    if g0 > s:
                        e = jnp.exp(rel)
                    else:
                        keep = rowid[g0:g1, :] >= s
                        e = jnp.where(keep, jnp.exp(jnp.where(keep, rel, 0.0)), 0.0)
                    col = jnp.sum(q[g0:g1, :] * e * k[s:s + 1, :], axis=-1, keepdims=True)
                    term = col * v[s:s + 1, :]
                    o_g[gi] = term if o_g[gi] is None else o_g[gi] + term
            o_ref[rs, cs_] = o_g[0] if len(o_g) == 1 else jnp.concatenate(o_g, axis=0)
            kd = _pad_rows((k * jnp.exp(b[chunk - 1:chunk, :] - b)).astype(BF16), MM_ROWS)
            vd = _pad_rows(v.astype(BF16), MM_ROWS)
            u_s[c * HG_HEADS + h] = _dot_tn(vd, kd)
        return 0

    def with_state(c, _):
        for h in range(HG_HEADS):
            rs, cs_, _ = chunk_slices(c, h)
            q, b = q_s[rs, cs_], b_s[rs, cs_]
            hg = h4_ref[rs, 3 * w + h * HG_DK:3 * w + (h + 1) * HG_DK]
            st = st_ref[h] if carry else s0_ref[c, h].T
            st_new = st * jnp.exp(b[chunk - 1:chunk, :]) + u_s[c * HG_HEADS + h]
            if carry:
                st_ref[h] = st_new
            else:
                sout_ref[c, h] = st_new.T
            qd = _pad_rows((q * jnp.exp(b)).astype(BF16), MM_ROWS)
            o = o_ref[rs, cs_] + _dot_nt(qd, st.astype(BF16))[:chunk]
            on = o * lax.rsqrt(jnp.mean(o * o, axis=-1, keepdims=True) + RMS_EPS) * ng
            o_ref[rs, cs_] = on * _sigmoid(hg)
        return 0

    lax.fori_loop(0, nchunk, state_free, 0, unroll=4)
    lax.fori_loop(0, nchunk, with_state, 0, unroll=True)

    if carry:
        @pl.when(pl.program_id(1) == pl.num_programs(1) - 1)
        def _():
            for h in range(HG_HEADS):
                sout_ref[0, h] = st_ref[h].T


def _hgrn(h4, s0_all, s_layer, lb_logits, norm_g, layer, nb, seq):
    n = h4.shape[0]
    chunk = math.gcd(seq, HG_CHUNK)
    carry = seq > chunk
    if carry:
        tl = _tile(seq, 256)
        grid = (nb, seq // tl)
        row_map = lambda b, l: (b * (seq // tl) + l, 0)
        st_map = lambda b, l: (b, 0, 0, 0)
        st_blk = (1, HG_HEADS, HG_DK, HG_DK)
        sem = ("parallel", "arbitrary")
    else:
        bt = _tile(nb, 8)
        tl = bt * seq
        grid = (nb // bt, 1)
        row_map = lambda b, l: (b, 0)
        st_map = lambda b, l: (b, 0, 0, 0)
        st_blk = (bt, HG_HEADS, HG_DK, HG_DK)
        sem = ("parallel", "arbitrary")
    kern = functools.partial(_hgrn_kernel, layer=layer, chunk=chunk, carry=carry)
    return pl.pallas_call(
        kern,
        out_shape=(jax.ShapeDtypeStruct((n, HG_WIDTH), F32),
                   jax.ShapeDtypeStruct((nb, HG_HEADS, HG_DK, HG_DK), F32)),
        grid=grid,
        in_specs=[pl.BlockSpec((tl, 4 * HG_WIDTH), row_map),
                  pl.BlockSpec((None,) + st_blk, lambda b, l: (s_layer, b, 0, 0, 0)),
                  pl.BlockSpec((N_EVEN, HG_WIDTH), lambda b, l: (0, 0)),
                  pl.BlockSpec((1, HG_DK), lambda b, l: (0, 0))],
        out_specs=(pl.BlockSpec((tl, HG_WIDTH), row_map), pl.BlockSpec(st_blk, st_map)),
        scratch_shapes=[pltpu.VMEM((HG_HEADS, HG_DK, HG_DK), F32),
                        pltpu.VMEM((tl, HG_WIDTH), F32), pltpu.VMEM((tl, HG_WIDTH), F32),
                        pltpu.VMEM((tl, HG_WIDTH), F32),
                        pltpu.VMEM((tl // chunk * HG_HEADS, HG_DK, HG_DK), F32)],
        compiler_params=_params(*sem), name="hgrn2")(h4, s0_all, lb_logits, norm_g)


def _lambda(lamp_ref):
    lam_init = lamp_ref[4:5, 0:1]
    l1 = jnp.sum(lamp_ref[0:1, :] * lamp_ref[1:2, :], axis=-1, keepdims=True)
    l2 = jnp.sum(lamp_ref[2:3, :] * lamp_ref[3:4, :], axis=-1, keepdims=True)
    return jnp.exp(l1) - jnp.exp(l2) + lam_init, lam_init


def _diff_finish(o0, o1, lam, lam_init, ng):
    o = o0 - lam * o1
    return o * lax.rsqrt(jnp.mean(o * o, axis=-1, keepdims=True) + RMS_EPS) * ng * (1.0 - lam_init)


def _lane_fold(x, op):
    parts = [x[:, c * LANES:(c + 1) * LANES] for c in range(x.shape[1] // LANES)]
    return functools.reduce(op, parts)


def _attn_prompt_kernel(q_ref, k_ref, v_ref, lamp_ref, ng_ref, o_ref, kb_s, vb_s, s_s, mp_s, lp_s,
                        acc_s):
    tq = q_ref.shape[0]
    i = pl.program_id(2)

    @pl.when(i == 0)
    def _():
        for jb in range(kb_s.shape[0]):
            kb_s[jb] = k_ref[:, jb * tq:(jb + 1) * tq].astype(BF16)
        vb_s[...] = v_ref[...].astype(BF16)

    lane = lax.broadcasted_iota(jnp.int32, (tq, DA_DV), 1)
    qs = q_ref[...] * (DA_DH ** -0.5)
    q2 = jnp.concatenate([jnp.where(lane < DA_DH, qs, 0.0), jnp.where(lane >= DA_DH, qs, 0.0)],
                         axis=0).astype(BF16)
    mp_s[...] = jnp.full(mp_s.shape, NEG_BIG, F32)

    def scores(j, masked):
        s = _dot(q2, kb_s[j])
        if masked:
            row = lax.broadcasted_iota(jnp.int32, (2 * tq, tq), 0) % tq
            col = lax.broadcasted_iota(jnp.int32, (2 * tq, tq), 1)
            s = jnp.where(col <= row, s, NEG_BIG)
        s_s[j] = s
        mp_s[...] = jnp.maximum(mp_s[...], _lane_fold(s, jnp.maximum))

    def pass1(j, c):
        scores(j, False)
        return c

    lax.fori_loop(0, i, pass1, 0)
    scores(i, True)
    m = jnp.max(mp_s[...], axis=-1, keepdims=True)
    lp_s[...] = jnp.zeros(lp_s.shape, F32)
    acc_s[...] = jnp.zeros(acc_s.shape, F32)

    def pass2(j, c):
        r0 = pl.multiple_of(j * tq, tq)
        p = jnp.exp(s_s[j] - m)
        lp_s[...] += _lane_fold(p, lambda a, b: a + b)
        acc_s[...] += _dot(p.astype(BF16), vb_s[pl.ds(r0, tq), :])
        return c

    lax.fori_loop(0, i + 1, pass2, 0)
    l = jnp.sum(lp_s[...], axis=-1, keepdims=True)
    lam, lam_init = _lambda(lamp_ref)
    o0 = acc_s[0:tq, :] / l[0:tq, :]
    o1 = acc_s[tq:2 * tq, :] / l[tq:2 * tq, :]
    o_ref[...] = _diff_finish(o0, o1, lam, lam_init, ng_ref[...])


def _attn_prompt(q, k_all, v_all, lamp, norm_g, layer, nb, seq):
    n = q.shape[0]
    tq = _tile(seq, 512)
    nq = seq // tq
    return pl.pallas_call(
        _attn_prompt_kernel, out_shape=jax.ShapeDtypeStruct((n, DA_WIDTH), F32),
        grid=(nb, DA_HEADS, nq),
        in_specs=[pl.BlockSpec((tq, DA_DV), lambda b, h, i: (b * nq + i, h)),
                  pl.BlockSpec((None, None, DA_DV, seq), lambda b, h, i: (layer, b, h, 0)),
                  pl.BlockSpec((None, seq, DA_DV), lambda b, h, i: (layer, b, h)),
                  pl.BlockSpec((8, LANES), lambda b, h, i: (0, 0)),
                  pl.BlockSpec((1, DA_DV), lambda b, h, i: (0, 0))],
        out_specs=pl.BlockSpec((tq, DA_DV), lambda b, h, i: (b * nq + i, h)),
        scratch_shapes=[pltpu.VMEM((nq, DA_DV, tq), BF16), pltpu.VMEM((seq, DA_DV), BF16),
                        pltpu.VMEM((nq, 2 * tq, tq), F32), pltpu.VMEM((2 * tq, LANES), F32),
                        pltpu.VMEM((2 * tq, LANES), F32), pltpu.VMEM((2 * tq, DA_DV), F32)],
        compiler_params=_params("parallel", "parallel", "arbitrary"),
        name="attn_prompt")(q, k_all, v_all, lamp, norm_g)


def _attn_sample_kernel(pt_ref, q_ref, *refs, n_pages):
    kp = refs[:n_pages]
    vp = refs[n_pages:2 * n_pages]
    kn_ref, vn_ref, lamp_ref, ng_ref, o_ref, kt_s, v_s = refs[2 * n_pages:]
    t = q_ref.shape[0]
    nrow = 2 * DA_HEADS * t
    hrow = 2 * t

    qs = q_ref[...] * (DA_DH ** -0.5)
    qt = jnp.concatenate([qs] * (2 * DA_HEADS), axis=0)
    r = lax.broadcasted_iota(jnp.int32, (nrow, DA_QK), 0) // t
    c = lax.broadcasted_iota(jnp.int32, (nrow, DA_QK), 1) // DA_DH
    qbd = jnp.where(r == c, qt, 0.0).astype(BF16)

    pc = PAGE_SIZE * DA_HEADS
    for i in range(n_pages):
        kt_s[:, i * PAGE_SIZE:(i + 1) * PAGE_SIZE] = kp[i][...].astype(BF16)
        v_s[i * pc:(i + 1) * pc, :] = vp[i][...].astype(BF16)
    s_old = _dot(qbd, kt_s[...])
    kb = _pad_rows(kn_ref[...], PAGE_SIZE).astype(BF16)
    vb = _pad_rows(vn_ref[...], PAGE_SIZE).astype(BF16)
    r = lax.broadcasted_iota(jnp.int32, (nrow, PAGE_SIZE), 0) % t
    c = lax.broadcasted_iota(jnp.int32, (nrow, PAGE_SIZE), 1)
    s_new = jnp.where(c <= r, _dot_nt(qbd, kb), NEG_BIG)
    m = jnp.maximum(jnp.max(s_old, axis=-1, keepdims=True), jnp.max(s_new, axis=-1, keepdims=True))
    p_old = jnp.exp(s_old - m)
    p_new = jnp.exp(s_new - m)
    l = jnp.sum(p_old, axis=-1, keepdims=True) + jnp.sum(p_new, axis=-1, keepdims=True)
    pb_old = p_old.astype(BF16)
    pb_new = p_new.astype(BF16)

    spread = jnp.where(lax.broadcasted_iota(jnp.int32, (PAGE_SIZE, pc), 1) // DA_HEADS
                       == lax.broadcasted_iota(jnp.int32, (PAGE_SIZE, pc), 0), 1.0, 0.0).astype(BF16)
    own = (lax.broadcasted_iota(jnp.int32, (n_pages * nrow, pc), 1) % DA_HEADS
           == (lax.broadcasted_iota(jnp.int32, (n_pages * nrow, pc), 0) % nrow) // hrow)
    p_rows = jnp.concatenate([pb_old[:, i * PAGE_SIZE:(i + 1) * PAGE_SIZE] for i in range(n_pages)],
                             axis=0)
    pe = jnp.where(own, _dot(p_rows, spread), 0.0).astype(BF16)
    pe = jnp.concatenate([pe[i * nrow:(i + 1) * nrow, :] for i in range(n_pages)], axis=1)
    acc = _dot(pe, v_s[...])
    acc = acc + jnp.concatenate([_dot(pb_new[h * hrow:(h + 1) * hrow, :], vb[:, h * DA_DV:(h + 1) * DA_DV])
                                 for h in range(DA_HEADS)], axis=0)
    o = acc / l
    lam, lam_init = _lambda(lamp_ref)
    ng = ng_ref[...]
    for h in range(DA_HEADS):
        r0 = h * hrow
        o_ref[:, h * DA_DV:(h + 1) * DA_DV] = _diff_finish(o[r0:r0 + t, :], o[r0 + t:r0 + 2 * t, :],
                                                           lam, lam_init, ng)


def _attn_sample(q, k_new, v_new, cache_kt, cache_v, page_table, lamp, norm_g, layer, nb, seq):
    n = q.shape[0]
    n_pages = page_table.shape[1]
    row_map = lambda b, pt: (b, 0)
    new_map = lambda b, pt: (layer, b, 0)
    k_specs = [pl.BlockSpec((None, None, DA_QK, PAGE_SIZE),
                            lambda b, pt, i=i: (layer, pt[b * n_pages + i], 0, 0))
               for i in range(n_pages)]
    v_specs = [pl.BlockSpec((None, None, PAGE_SIZE * DA_HEADS, DA_DV),
                            lambda b, pt, i=i: (layer, pt[b * n_pages + i], 0, 0))
               for i in range(n_pages)]
    grid_spec = pltpu.PrefetchScalarGridSpec(
        num_scalar_prefetch=1, grid=(nb,),
        in_specs=[pl.BlockSpec((seq, DA_QK), row_map)] + k_specs + v_specs
        + [pl.BlockSpec((None, seq, DA_QK), new_map), pl.BlockSpec((None, seq, DA_WIDTH), new_map),
           pl.BlockSpec((8, LANES), lambda b, pt: (0, 0)),
           pl.BlockSpec((1, DA_DV), lambda b, pt: (0, 0))],
        out_specs=pl.BlockSpec((seq, DA_WIDTH), row_map),
        scratch_shapes=[pltpu.VMEM((DA_QK, n_pages * PAGE_SIZE), BF16),
                        pltpu.VMEM((n_pages * PAGE_SIZE * DA_HEADS, DA_DV), BF16)])
    return pl.pallas_call(
        functools.partial(_attn_sample_kernel, n_pages=n_pages),
        out_shape=jax.ShapeDtypeStruct((n, DA_WIDTH), F32),
        grid_spec=grid_spec, compiler_params=_params("parallel"),
        name="attn_sample")(page_table.reshape(-1), q, *([cache_kt] * n_pages),
                            *([cache_v] * n_pages), k_new, v_new, lamp, norm_g)


FF_CHUNK = 1024


def _mix_mlp_kernel(a1_ref, a2_ref, x_ref, wmix_ref, n1g_ref, n1b_ref, p_ref, up_ref, down_ref,
                    n2g_ref, n2b_ref, gate_ref, proj_ref, o_ref, *, even):
    if even:
        h = (_dot(a1_ref[...].astype(BF16), wmix_ref[0:HG_WIDTH, :])
             + _dot(a2_ref[...].astype(BF16), wmix_ref[HG_WIDTH:D_MODEL, :]))
    else:
        h = _dot((a1_ref[...] * a2_ref[...]).astype(BF16), wmix_ref[...])
    x = _layer_norm(ALPHA * x_ref[...] + h, n1g_ref[...], n1b_ref[...])
    xb = x.astype(BF16)
    acc = jnp.zeros(x.shape, F32)
    for c in range(D_FF // FF_CHUNK):
        cs = slice(c * FF_CHUNK, (c + 1) * FF_CHUNK)
        hid = jnp.maximum(_dot(xb, up_ref[:, cs]), 0.0)
        acc = acc + _dot((hid * hid).astype(BF16), down_ref[cs, :])
    x2 = _layer_norm(ALPHA * x + acc, n2g_ref[...], n2b_ref[...])
    gate = _sigmoid(_dot(x2.astype(BF16), gate_ref[...]))
    o_ref[...] = x2 + gate * _dot(p_ref[...].astype(BF16), proj_ref[...])


def _mix_mlp(a1, a2, x, wmix_bf, n1g, n1b, even, p_all, layer, up_bf, down_bf, n2g, n2b, gate_bf,
             proj_bf):
    n = x.shape[0]
    tm = _tile(n, 512)
    vec = pl.BlockSpec((1, D_MODEL), lambda i: (0, 0))
    return pl.pallas_call(
        functools.partial(_mix_mlp_kernel, even=even),
        out_shape=jax.ShapeDtypeStruct((n, D_MODEL), F32), grid=(n // tm,),
        in_specs=[_row_spec(tm, a1.shape[1]), _row_spec(tm, a2.shape[1]), _row_spec(tm, D_MODEL),
                  _const_spec((D_MODEL, D_MODEL)), vec, vec,
                  pl.BlockSpec((None, tm, PLE_DIM), lambda i: (layer, i, 0)),
                  _const_spec((D_MODEL, D_FF)), _const_spec((D_FF, D_MODEL)), vec, vec,
                  _const_spec((D_MODEL, D_MODEL)), _const_spec((PLE_DIM, D_MODEL))],
        out_specs=_row_spec(tm, D_MODEL),
        compiler_params=_params("parallel"), name="mix_mlp")(
            a1, a2, x, wmix_bf, n1g, n1b, p_all, up_bf, down_bf, n2g, n2b, gate_bf, proj_bf)


def _transpose_atom_grid(p):
    shape = p[0][0].shape
    sub = lax.broadcasted_iota(jnp.int32, shape, 0)
    half = lax.broadcasted_iota(jnp.int32, shape, 1) // (LANES // 2)
    s0 = sub & 1
    na, nc = len(p), len(p[0])

    def bit0(x):
        z = pltpu.roll(x, LANES // 2, 1)
        moved = jnp.where(s0 == 1, pltpu.roll(z, 1, 0), pltpu.roll(z, SUBLANES - 1, 0))
        return jnp.where(s0 == half, x, moved)

    p = [[bit0(t) for t in row] for row in p]
    for shift, cbit in ((2, 1), (4, 2)):
        up = (sub & shift) != 0
        for a in range(na):
            for c in range(nc):
                if c & cbit == 0:
                    lo, hi = p[a][c], p[a][c + cbit]
                    p[a][c] = jnp.where(up, pltpu.roll(hi, shift, 0), lo)
                    p[a][c + cbit] = jnp.where(up, hi, pltpu.roll(lo, SUBLANES - shift, 0))
    q = nc // na
    return [[p[c // q][a2 * q + c % q] for c in range(nc)] for a2 in range(na)]


def _tokens_to_chains(x, nseq, tt, out_ref):
    sub = LANES // nseq
    ntile = D_MODEL // LANES
    for sb in range(tt // sub):
        grids = []
        for b in range(nseq):
            r0 = b * tt + sb * sub
            grids.append(_transpose_atom_grid(
                [[x[r0 + a * SUBLANES:r0 + (a + 1) * SUBLANES, c * LANES:(c + 1) * LANES]
                  for c in range(ntile)] for a in range(sub // SUBLANES)]))
        for tau in range(ntile):
            tile = jnp.concatenate([g[a][tau] for g in grids for a in range(sub // SUBLANES)],
                                   axis=0).T
            out_ref[sb * sub + 2 * tau] = tile[0:RW_HEAD, :]
            out_ref[sb * sub + 2 * tau + 1] = tile[RW_HEAD:2 * RW_HEAD, :]


def _rwkv_pre_kernel(*refs, has_vfirst, chains, want_v_tokens):
    if chains:
        x_ref, x8_ref, sh_ref = refs[:3]
        refs = refs[3:]
    else:
        x_ref, xp_ref = refs[:2]
        refs = refs[2:]
    vf_ref = vt_o = v0_ref = v1_ref = v2_ref = None
    if has_vfirst:
        vf_ref, refs = refs[0], refs[1:]
    (mix_ref, wr_ref, wk_ref, wv_ref, w0_ref, w1_ref, w2_ref, a0_ref, a1_ref, a2_ref) = refs[:10]
    refs = refs[10:]
    if has_vfirst:
        v0_ref, v1_ref, v2_ref = refs[:3]
        refs = refs[3:]
    g1_ref, g2_ref, r_o, k_o, v_o, g_o, dec_o, a_o = refs[:8]
    if want_v_tokens:
        vt_o = refs[8]
    if chains:
        nseq, tt = x_ref.shape[0], x_ref.shape[1]
        x = x_ref[...].reshape(nseq * tt, D_MODEL)
        first = pl.program_id(0) == 0
        before = jnp.where(first, sh_ref[...], x8_ref[:, SUBLANES - 1, :])
        before = jnp.concatenate([jnp.broadcast_to(before[b:b + 1, :], (tt, D_MODEL))
                                  for b in range(nseq)], axis=0)
        rowid = lax.broadcasted_iota(jnp.int32, x.shape, 0)
        x_prev = jnp.where(rowid % tt == 0, before, pltpu.roll(x, 1, 0))
    else:
        x = x_ref[...]
        x_prev = xp_ref[...]
    xx = x_prev - x
    xr, xw, xk, xv, xa, xg = ((x + xx * mix_ref[m:m + 1, :]).astype(BF16) for m in range(6))
    r = _dot(xr, wr_ref[...])
    k = _dot(xk, wk_ref[...])
    v = _dot(xv, wv_ref[...])
    lw = w0_ref[...] + _dot(jnp.tanh(_dot(xw, w1_ref[...])).astype(BF16), w2_ref[...])
    w = -(jnp.maximum(-lw, 0.0) + jnp.log(1.0 + jnp.exp(-jnp.abs(lw)))) - 0.5
    if has_vfirst:
        mixv = _sigmoid(v0_ref[...] + _dot(_dot(xv, v1_ref[...]).astype(BF16), v2_ref[...]))
        vf = vf_ref[...].reshape(v.shape) if chains else vf_ref[...]
        v = v + (vf - v) * mixv
    a = _sigmoid(a0_ref[...] + _dot(_dot(xa, a1_ref[...]).astype(BF16), a2_ref[...]))
    g = _dot(_sigmoid(_dot(xg, g1_ref[...])).astype(BF16), g2_ref[...])
    dec = jnp.exp(-jnp.exp(w))
    if chains:
        g_o[...] = g.reshape(g_o.shape)
        if want_v_tokens:
            vt_o[...] = v.reshape(vt_o.shape)
        for val, ref in ((r, r_o), (k, k_o), (v, v_o), (dec, dec_o), (a, a_o)):
            _tokens_to_chains(val, nseq, tt, ref)
    else:
        r_o[...] = r
        k_o[...] = k
        v_o[...] = v
        g_o[...] = g
        dec_o[...] = dec
        a_o[...] = a
        if want_v_tokens:
            vt_o[...] = v


def _chain_layout_ok(nb, seq):
    return nb * RW_HEADS == LANES and LANES // nb == RW_HEADS and seq % RW_TBLK == 0


def _rwkv_pre(x, shift0, v_first, wts, nb, seq, want_v_tokens):
    n = x.shape[0]
    has_vfirst = v_first is not None
    chains = _chain_layout_ok(nb, seq)
    vec = pl.BlockSpec((1, D_MODEL), lambda i: (0, 0))
    if chains:
        tt = RW_TBLK
        tok3 = pl.BlockSpec((nb, tt, D_MODEL), lambda i: (0, i, 0))
        x3 = x.reshape(nb, seq, D_MODEL)
        r8 = tt // SUBLANES
        acts = [x3, x3, shift0]
        act_specs = [tok3,
                     pl.BlockSpec((nb, SUBLANES, D_MODEL), lambda i: (0, jnp.maximum(i * r8 - 1, 0), 0)),
                     pl.BlockSpec((nb, D_MODEL), lambda i: (0, 0))]
        if has_vfirst:
            acts.append(v_first.reshape(nb, seq, D_MODEL))
            act_specs.append(tok3)
        grid = (seq // tt,)
        chain_out = jax.ShapeDtypeStruct((seq, RW_HEAD, LANES), F32)
        chain_spec = pl.BlockSpec((tt, RW_HEAD, LANES), lambda i: (i, 0, 0))
        tok_out = jax.ShapeDtypeStruct((nb, seq, D_MODEL), F32)
        out_shape = [chain_out, chain_out, chain_out, tok_out, chain_out, chain_out]
        out_specs = [chain_spec, chain_spec, chain_spec, tok3, chain_spec, chain_spec]
        if want_v_tokens:
            out_shape.append(tok_out)
            out_specs.append(tok3)
    else:
        tm = _tile(n, 256)
        xs = x.reshape(nb, seq, D_MODEL)
        x_prev = jnp.concatenate([shift0[:, None, :], xs[:, :-1]], axis=1).reshape(n, D_MODEL)
        acts = [x, x_prev] + ([v_first] if has_vfirst else [])
        act_specs = [_row_spec(tm, D_MODEL)] * len(acts)
        grid = (n // tm,)
        out_shape = [jax.ShapeDtypeStruct((n, D_MODEL), F32)] * 6
        out_specs = [_row_spec(tm, D_MODEL)] * 6
    names = ['mix', 'w_r', 'w_k', 'w_v', 'w0', 'w1', 'w2', 'a0', 'a1', 'a2']
    if has_vfirst:
        names += ['v0', 'v1', 'v2']
    names += ['g1', 'g2']
    ws = [wts[nm] for nm in names]
    w_specs = [vec if w.shape == (1, D_MODEL) else _const_spec(w.shape) for w in ws]
    outs = pl.pallas_call(
        functools.partial(_rwkv_pre_kernel, has_vfirst=has_vfirst, chains=chains,
                          want_v_tokens=want_v_tokens and chains),
        out_shape=tuple(out_shape), grid=grid,
        in_specs=act_specs + w_specs, out_specs=tuple(out_specs),
        compiler_params=_params("parallel"), name="rwkv_pre")(*acts, *ws)
    outs = list(outs)
    if chains:
        outs[3] = outs[3].reshape(n, D_MODEL)
        if want_v_tokens:
            outs[6] = outs[6].reshape(n, D_MODEL)
    elif want_v_tokens:
        outs.append(outs[2])
    return outs


RW_TBLK = 32


RW_IN = 5


def _rwkv_token(ins, v_rows, prm, s_s, y_s):
    r, k, v, dec, ag = ins
    kk_p, ka_p, rk_p, lg_p, lb_p = prm
    kk = k * kk_p
    kk = kk / jnp.maximum(jnp.sqrt(jnp.sum(kk * kk, axis=0, keepdims=True)), 1e-12)
    a = -kk
    b = kk * ag
    k2 = k * (1.0 + (ag - 1.0) * ka_p)
    wr = dec * r
    br = jnp.sum(b * r, axis=0, keepdims=True)
    kr = jnp.sum(k2 * r, axis=0, keepdims=True)

    def rows(i8, _):
        i0 = pl.multiple_of(i8 * SUBLANES, SUBLANES)
        v8 = v_rows(i0)
        ys = []
        for ii in range(SUBLANES):
            si = s_s[i0 + ii]
            sa = jnp.sum(si * a, axis=0, keepdims=True)
            qs = jnp.sum(si * wr, axis=0, keepdims=True)
            vi = v8[ii:ii + 1, :]
            s_s[i0 + ii] = si * dec + sa * b + vi * k2
            ys.append(qs + sa * br + vi * kr)
        y_s[pl.ds(i0, SUBLANES), :] = jnp.concatenate(ys, axis=0)
        return 0

    lax.fori_loop(0, RW_HEAD // SUBLANES, rows, 0)
    y = y_s[...]
    mu = jnp.mean(y, axis=0, keepdims=True)
    yc = y - mu
    var = jnp.mean(yc * yc, axis=0, keepdims=True)
    yn = yc * lax.rsqrt(var + GN_EPS) * lg_p + lb_p
    return yn + jnp.sum(r * k2 * rk_p, axis=0, keepdims=True) * v


RW_IBLK = 32
RW_ACC = 2


def _rwkv_token_keymajor(ins, v_rows, prm, s_s, y_s, vec_s, g_s):
    r, k, v, dec, ag = ins
    kk_p, ka_p, rk_p, lg_p, lb_p = prm
    kk = k * kk_p
    kk = kk / jnp.maximum(jnp.sqrt(jnp.sum(kk * kk, axis=0, keepdims=True)), 1e-12)
    b = kk * ag
    k2 = k * (1.0 + (ag - 1.0) * ka_p)
    br = jnp.sum(b * r, axis=0, keepdims=True)
    kr = jnp.sum(k2 * r, axis=0, keepdims=True)
    g_old = g_s[...]
    g_new = g_old * dec
    g_s[...] = g_new
    g_inv = 1.0 / g_new
    for idx, val in enumerate((-kk * g_old, r * g_new, b * g_inv, k2 * g_inv)):
        vec_s[idx] = val
    row = lambda idx, j: vec_s[idx, j:j + 1, :]

    for p in range(RW_HEAD // RW_IBLK):
        rs = pl.ds(p * RW_IBLK, RW_IBLK)
        sa = [None] * RW_ACC
        qs = [None] * RW_ACC
        for j in range(RW_HEAD):
            sj = s_s[j, rs, :]
            c = j % RW_ACC
            sa[c] = sj * row(0, j) if sa[c] is None else sa[c] + sj * row(0, j)
            qs[c] = sj * row(1, j) if qs[c] is None else qs[c] + sj * row(1, j)
        sa = functools.reduce(lambda x, y: x + y, sa)
        qs = functools.reduce(lambda x, y: x + y, qs)
        vi = v_rows(rs)
        for j in range(RW_HEAD):
            s_s[j, rs, :] = s_s[j, rs, :] + sa * row(2, j) + vi * row(3, j)
        y_s[rs, :] = qs + sa * br + vi * kr
    y = y_s[...]
    mu = jnp.mean(y, axis=0, keepdims=True)
    yc = y - mu
    var = jnp.mean(yc * yc, axis=0, keepdims=True)
    yn = yc * lax.rsqrt(var + GN_EPS) * lg_p + lb_p
    return yn + jnp.sum(r * k2 * rk_p, axis=0, keepdims=True) * v


def _rwkv_scan_kernel(r_ref, k_ref, v_ref, dec_ref, a_ref, kk_ref, ka_ref, rk_ref, lg_ref, lb_ref,
                      s0_ref, y_ref, sout_ref, s_s, y_s):
    tb = pl.program_id(1)
    in_refs = (r_ref, k_ref, v_ref, dec_ref, a_ref)
    prm_refs = (kk_ref, ka_ref, rk_ref, lg_ref, lb_ref)

    @pl.when(tb == 0)
    def _():
        s_s[...] = s0_ref[...]

    def step(t, _):
        y_ref[t] = _rwkv_token([ref[t] for ref in in_refs],
                               lambda i0: v_ref[t, pl.ds(i0, SUBLANES), :],
                               [p[...] for p in prm_refs], s_s, y_s)
        return 0

    lax.fori_loop(0, r_ref.shape[0], step, 0)

    @pl.when(tb == pl.num_programs(1) - 1)
    def _():
        sout_ref[...] = s_s[...]


def _rwkv_scan_tokens_kernel(r_ref, k_ref, v_ref, dec_ref, a_ref, kk_ref, ka_ref, rk_ref, lg_ref,
                             lb_ref, s0_ref, y_ref, sout_ref, s_s, y_s, yt_s, vec_s, g_s):
    tb = pl.program_id(1)
    nseq, tt = y_ref.shape[0], y_ref.shape[1]
    sub = LANES // nseq
    in_refs = (r_ref, k_ref, v_ref, dec_ref, a_ref)
    prm_refs = (kk_ref, ka_ref, rk_ref, lg_ref, lb_ref)

    @pl.when(tb == 0)
    def _():
        s_s[...] = s0_ref[...]
        g_s[...] = jnp.ones(g_s.shape, F32)

    def sub_block(sb, _):
        t0 = pl.multiple_of(sb * sub, sub)

        def step(t, _):
            yt_s[t] = _rwkv_token_keymajor([ref[t0 + t] for ref in in_refs],
                                           lambda rs: v_ref[t0 + t, rs, :],
                                           [p[...] for p in prm_refs], s_s, y_s, vec_s, g_s)
            return 0

        lax.fori_loop(0, sub, step, 0)
        for j in range(RW_HEAD):
            s_s[j] = s_s[j] * g_s[j:j + 1, :]
        g_s[...] = jnp.ones(g_s.shape, F32)
        ntile = D_MODEL // LANES
        na = sub // SUBLANES
        tiles = [jnp.concatenate([yt_s[2 * tau], yt_s[2 * tau + 1]], axis=0).T for tau in range(ntile)]
        for b in range(nseq):
            grid = _transpose_atom_grid(
                [[tiles[tau][b * sub + a * SUBLANES:b * sub + (a + 1) * SUBLANES, :]
                  for tau in range(ntile)] for a in range(na)])
            for a in range(na):
                for c in range(ntile):
                    y_ref[b, pl.ds(t0 + a * SUBLANES, SUBLANES), c * LANES:(c + 1) * LANES] = grid[a][c]
        return 0

    lax.fori_loop(0, tt // sub, sub_block, 0)

    @pl.when(tb == pl.num_programs(1) - 1)
    def _():
        sout_ref[...] = s_s[...]


def _rwkv_scan(r, k, v, dec, a, s0_all, s_layer, wts, nb, seq):
    nl = s0_all.shape[0]
    n = nb * seq
    hd = RW_HEAD
    batch_lanes = nb % LANES == 0
    if batch_lanes:
        nbg = nb // LANES
        groups, chains = RW_HEADS * nbg, LANES
        to_l = lambda x: (x.reshape(nbg, LANES, seq, RW_HEADS, hd).transpose(0, 3, 2, 4, 1)
                          .reshape(groups, seq, hd, chains))
        from_l = lambda y: (y.reshape(nbg, RW_HEADS, seq, hd, LANES).transpose(0, 4, 2, 1, 3)
                            .reshape(n, D_MODEL))
        par = lambda p: jnp.broadcast_to(p.reshape(1, RW_HEADS, hd, 1),
                                         (nbg, RW_HEADS, hd, chains)).reshape(groups, hd, chains)
        st_in = lambda s: (s.reshape(nl, nbg, LANES, RW_HEADS, hd, hd).transpose(0, 1, 3, 4, 5, 2)
                           .reshape(nl, groups, hd, hd, chains))
        st_out = lambda s: (s.reshape(nbg, RW_HEADS, hd, hd, LANES).transpose(0, 4, 1, 2, 3)
                            .reshape(nb, RW_HEADS, hd, hd))
    else:
        groups, chains = 1, nb * RW_HEADS
        to_l = lambda x: (x.reshape(nb, seq, RW_HEADS, hd).transpose(1, 3, 0, 2)
                          .reshape(1, seq, hd, chains))
        from_l = lambda y: (y.reshape(seq, hd, nb, RW_HEADS).transpose(2, 0, 3, 1)
                            .reshape(n, D_MODEL))
        par = lambda p: jnp.broadcast_to(p.reshape(RW_HEADS, hd).T.reshape(1, hd, 1, RW_HEADS),
                                         (1, hd, nb, RW_HEADS)).reshape(1, hd, chains)
        st_in = lambda s: s.transpose(0, 3, 4, 1, 2).reshape(nl, 1, hd, hd, chains)
        st_out = lambda s: s.reshape(hd, hd, nb, RW_HEADS).transpose(2, 3, 0, 1)
    tt = _tile(seq, RW_TBLK)
    prm = pl.BlockSpec((None, hd, chains), lambda g, t: (g, 0, 0))
    st = pl.BlockSpec((None, hd, hd, chains), lambda g, t: (g, 0, 0, 0))
    st0 = pl.BlockSpec((None, None, hd, hd, chains), lambda g, t: (s_layer, g, 0, 0, 0))
    prms = [par(wts[nm]) for nm in ('k_k', 'k_a', 'r_k', 'lnx_g', 'lnx_b')]
    if _chain_layout_ok(nb, seq):
        tok = pl.BlockSpec((nb, tt, D_MODEL), lambda g, t: (0, t, 0))
        act = pl.BlockSpec((tt, hd, chains), lambda g, t: (t, 0, 0))
        y, s_new = pl.pallas_call(
            _rwkv_scan_tokens_kernel,
            out_shape=(jax.ShapeDtypeStruct((nb, seq, D_MODEL), F32),
                       jax.ShapeDtypeStruct((1, hd, hd, chains), F32)),
            grid=(1, seq // tt),
            in_specs=[act] * RW_IN + [prm] * 5 + [st0],
            out_specs=(tok, st),
            scratch_shapes=[pltpu.VMEM((hd, hd, chains), F32), pltpu.VMEM((hd, chains), F32),
                            pltpu.VMEM((RW_HEADS, hd, chains), F32),
                            pltpu.VMEM((4, hd, chains), F32), pltpu.VMEM((hd, chains), F32)],
            compiler_params=_params("parallel", "arbitrary"),
            name="rwkv_scan_tokens")(r, k, v, dec, a, *prms, st_in(s0_all).swapaxes(2, 3))
        return y.reshape(n, D_MODEL), st_out(s_new.swapaxes(1, 2))
    act = pl.BlockSpec((None, tt, hd, chains), lambda g, t: (g, t, 0, 0))
    y, s_new = pl.pallas_call(
        _rwkv_scan_kernel,
        out_shape=(jax.ShapeDtypeStruct((groups, seq, hd, chains), F32),
                   jax.ShapeDtypeStruct((groups, hd, hd, chains), F32)),
        grid=(groups, seq // tt),
        in_specs=[act] * RW_IN + [prm] * 5 + [st0],
        out_specs=(act, st),
        scratch_shapes=[pltpu.VMEM((hd, hd, chains), F32), pltpu.VMEM((hd, chains), F32)],
        compiler_params=_params("parallel", "arbitrary"),
        name="rwkv_scan")(to_l(r), to_l(k), to_l(v), to_l(dec), to_l(a), *prms, st_in(s0_all))
    return from_l(y), st_out(s_new)


def _trunk(x3, p4, hg_s0, wkv_s0, shift0, cache_k, cache_v, page_table, W):
    nb, seq, _ = x3.shape
    n = nb * seq
    x = x3.reshape(n, D_MODEL)
    new_hg, new_wkv, new_shift = [], [], []
    kv = None
    v_first = None
    for i in range(DEPTH):
        j = i // 2
        if i % 2 == 0:
            wk_t = W['wk_t'][j] if page_table is None else None
            h4, q, k_all, v_all = _even_in(x, W['w_in_even'][j], wk_t, j, kv, nb, seq)
            kv = (k_all, v_all)
            oh, s_hg = _hgrn(h4, hg_s0, min(j, hg_s0.shape[0] - 1), W['hg_lb_logits'],
                             W['hg_norm_g'][j], j, nb, seq)
            if page_table is None:
                od = _attn_prompt(q, k_all, v_all, W['lamp'][j], W['da_norm_g'][j], j, nb, seq)
            else:
                od = _attn_sample(q, k_all, v_all, cache_k, cache_v, page_table, W['lamp'][j],
                                  W['da_norm_g'][j], j, nb, seq)
            mix = (oh, od, W['w_out_even'][j])
            new_hg.append(s_hg)
        else:
            need_vf = j == 0 and N_ODD > 1
            outs = _rwkv_pre(x, shift0[j], v_first, W['rw'][j], nb, seq, need_vf)
            r, k, v, g, dec, a = outs[:6]
            if need_vf:
                v_first = outs[6]
            y, s_wkv = _rwkv_scan(r, k, v, dec, a, wkv_s0, min(j, wkv_s0.shape[0] - 1),
                                  W['rw_scan'][j], nb, seq)
            new_shift.append(x.reshape(nb, seq, D_MODEL)[:, -1])
            new_wkv.append(s_wkv)
            mix = (y, g, W['rw_w_o'][j])
        x = _mix_mlp(mix[0], mix[1], x, mix[2], W['ln1_g'][i], W['ln1_b'][i], i % 2 == 0,
                     p4.reshape(DEPTH, n, PLE_DIM), i, W['mlp_up'][i], W['mlp_down'][i],
                     W['ln2_g'][i], W['ln2_b'][i], W['ple_gate'][i], W['ple_proj'][i])
    if page_table is None:
        k_out = kv[0].reshape(N_EVEN, nb, DA_HEADS, 2, DA_DH, seq).transpose(0, 1, 5, 2, 3, 4)
    else:
        k_out = kv[0].reshape(N_EVEN, nb, seq, DA_HEADS, 2, DA_DH)
    return (x.reshape(nb, seq, D_MODEL), k_out,
            kv[1].reshape(N_EVEN, nb, seq, DA_HEADS, DA_DV), jnp.stack(new_hg),
            jnp.stack(new_wkv), jnp.stack(new_shift))


def kernel(x_prompt, x_sample, cache_k, cache_v, state_hgrn, state_wkv, state_shift, page_table,
           p_prompt, p_sample, w_in_even, w_out_even, hg_lb_logits, hg_norm_g,
           da_lam_q1, da_lam_k1, da_lam_q2, da_lam_k2, da_norm_g,
           rw_mix, rw_w_r, rw_w_k, rw_w_v, rw_w_o, rw_w0, rw_w1, rw_w2,
           rw_a0, rw_a1, rw_a2, rw_v0, rw_v1, rw_v2, rw_g1, rw_g2,
           rw_k_k, rw_k_a, rw_r_k, rw_lnx_g, rw_lnx_b,
           ln1_g, ln1_b, ln2_g, ln2_b, mlp_up, mlp_down, ple_proj, ple_gate):
    bf = lambda w: w.astype(BF16)
    vec = lambda w: w.reshape(1, -1)

    def lam_rows(j):
        pad = lambda a: jnp.pad(a.astype(F32), (0, LANES - a.shape[0]))
        lam_init = 0.8 - 0.6 * math.exp(-0.3 * (2 * j))
        rows = [pad(da_lam_q1[j]), pad(da_lam_k1[j]), pad(da_lam_q2[j]), pad(da_lam_k2[j]),
                jnp.full((LANES,), lam_init, F32)]
        return jnp.stack(rows + [jnp.zeros((LANES,), F32)] * 3)

    rw = []
    for j in range(N_ODD):
        d = dict(mix=rw_mix[j], w_r=bf(rw_w_r[j]), w_k=bf(rw_w_k[j]), w_v=bf(rw_w_v[j]),
                 w0=vec(rw_w0[j]), w1=bf(rw_w1[j]), w2=bf(rw_w2[j]),
                 a0=vec(rw_a0[j]), a1=bf(rw_a1[j]), a2=bf(rw_a2[j]),
                 g1=bf(rw_g1[j]), g2=bf(rw_g2[j]))
        if j > 0:
            d.update(v0=vec(rw_v0[j - 1]), v1=bf(rw_v1[j - 1]), v2=bf(rw_v2[j - 1]))
        rw.append(d)
    W = dict(
        w_in_even=[bf(w_in_even[j]) for j in range(N_EVEN)],
        wk_t=[bf(w_in_even[j][:, 4 * HG_WIDTH + DA_QK:4 * HG_WIDTH + 2 * DA_QK].T) for j in range(N_EVEN)],
        w_out_even=[bf(w_out_even[j]) for j in range(N_EVEN)],
        hg_lb_logits=hg_lb_logits,
        hg_norm_g=[vec(hg_norm_g[j]) for j in range(N_EVEN)],
        da_norm_g=[vec(da_norm_g[j]) for j in range(N_EVEN)],
        lamp=[lam_rows(j) for j in range(N_EVEN)],
        rw=rw,
        rw_scan=[dict(k_k=rw_k_k[j], k_a=rw_k_a[j], r_k=rw_r_k[j], lnx_g=rw_lnx_g[j],
                      lnx_b=rw_lnx_b[j]) for j in range(N_ODD)],
        rw_w_o=[bf(rw_w_o[j]) for j in range(N_ODD)],
        ln1_g=[vec(ln1_g[i]) for i in range(DEPTH)], ln1_b=[vec(ln1_b[i]) for i in range(DEPTH)],
        ln2_g=[vec(ln2_g[i]) for i in range(DEPTH)], ln2_b=[vec(ln2_b[i]) for i in range(DEPTH)],
        mlp_up=[bf(mlp_up[i]) for i in range(DEPTH)], mlp_down=[bf(mlp_down[i]) for i in range(DEPTH)],
        ple_gate=[bf(ple_gate[i]) for i in range(DEPTH)], ple_proj=[bf(ple_proj[i]) for i in range(DEPTH)],
    )
    nbp = x_prompt.shape[0]
    dt = x_prompt.dtype
    hg0 = jnp.zeros((1, nbp, HG_HEADS, HG_DK, HG_DK), dt)
    wkv0 = jnp.zeros((1, nbp, RW_HEADS, RW_HEAD, RW_HEAD), dt)
    sh0 = jnp.zeros((N_ODD, nbp, D_MODEL), dt)
    y_p, k_p, v_p, hg_p, wkv_p, sh_p = _trunk(x_prompt, p_prompt, hg0, wkv0, sh0, None, None, None, W)
    ckt = jnp.transpose(cache_k, (0, 1, 3, 4, 5, 2)).reshape(
        cache_k.shape[0], cache_k.shape[1], DA_QK, PAGE_SIZE)
    cv = cache_v.reshape(cache_v.shape[0], cache_v.shape[1], PAGE_SIZE * DA_HEADS, DA_DV)
    y_s, k_s, v_s, hg_s, wkv_s, sh_s = _trunk(x_sample, p_sample, state_hgrn, state_wkv, state_shift,
                                              ckt, cv, page_table, W)
    return (y_p, y_s, k_p, v_p, k_s, v_s, hg_p, hg_s, wkv_p, wkv_s, sh_p, sh_s)
```

```python
import functools
import math

import jax
import jax.numpy as jnp
from jax import lax
from jax.experimental import pallas as pl
from jax.experimental.pallas import tpu as pltpu

F32 = jnp.float32
BF16 = jnp.bfloat16

D_MODEL = 1024
DEPTH = 4
PAGE_SIZE = 128
N_EVEN = (DEPTH + 1) // 2
N_ODD = DEPTH // 2
PLE_DIM = 256
D_FF = 4 * D_MODEL
HG_WIDTH = D_MODEL // 2
HG_HEADS = 4
HG_DK = HG_WIDTH // HG_HEADS
DA_WIDTH = D_MODEL - HG_WIDTH
DA_HEADS = 4
DA_DV = DA_WIDTH // DA_HEADS
DA_DH = DA_DV // 2
DA_QK = DA_HEADS * 2 * DA_DH
IN_COLS = 4 * HG_WIDTH + 2 * DA_QK + DA_WIDTH
RW_HEAD = 64
RW_HEADS = D_MODEL // RW_HEAD
ALPHA = (2.0 * DEPTH) ** 0.25
LN_EPS = 1e-5
RMS_EPS = 1e-6
GN_EPS = 64e-5
F_MIN = 1e-30
NEG_BIG = -1e30

LANES = 128
SUBLANES = 8
VMEM_LIMIT = 52 * 1024 * 1024
HG_CHUNK = 16
MM_ROWS = 16


def _params(*sem):
    return pltpu.CompilerParams(dimension_semantics=sem, vmem_limit_bytes=VMEM_LIMIT)


def _const_spec(shape):
    nd = len(shape)
    return pl.BlockSpec(shape, lambda *_: (0,) * nd, pipeline_mode=pl.Buffered(1))


def _row_spec(rows, cols):
    return pl.BlockSpec((rows, cols), lambda i: (i, 0))


def _tile(n, pref):
    t = min(n, pref)
    while n % t:
        t //= 2
    return t


def _dot(a, b):
    return jnp.dot(a, b, preferred_element_type=F32)


def _dot_nt(a, b):
    return lax.dot_general(a, b, (((1,), (1,)), ((), ())), preferred_element_type=F32)


def _dot_tn(a, b):
    return lax.dot_general(a, b, (((0,), (0,)), ((), ())), preferred_element_type=F32)


def _split3(x):
    h = x.astype(BF16)
    r = x - h.astype(F32)
    m = r.astype(BF16)
    l = (r - m.astype(F32)).astype(BF16)
    return h, m, l


def _dot_x3_lhs(lhs_bf16, x):
    h, m, l = _split3(x)
    return _dot(lhs_bf16, h) + _dot(lhs_bf16, m) + _dot(lhs_bf16, l)


def _layer_norm(z, g, b):
    mu = jnp.mean(z, axis=-1, keepdims=True)
    zc = z - mu
    var = jnp.mean(zc * zc, axis=-1, keepdims=True)
    return zc * lax.rsqrt(var + LN_EPS) * g + b


def _sigmoid(x):
    return 1.0 / (1.0 + jnp.exp(-x))


def _even_in_kernel(x_ref, w_ref, *refs, k_transposed, layer):
    h4_ref, q_ref, k_ref, v_ref = refs[-4:]
    xb = x_ref[...].astype(BF16)
    cw = HG_WIDTH
    for c in range(4):
        h4_ref[:, c * cw:(c + 1) * cw] = _dot(xb, w_ref[:, c * cw:(c + 1) * cw])
    q0 = 4 * HG_WIDTH
    q_ref[...] = _dot(xb, w_ref[:, q0:q0 + DA_QK])
    if k_transposed:
        k_ref[layer] = _dot_nt(refs[0][...], xb)
    else:
        k_ref[layer] = _dot(xb, w_ref[:, q0 + DA_QK:q0 + 2 * DA_QK])
    v_ref[layer] = _dot(xb, w_ref[:, q0 + 2 * DA_QK:q0 + 2 * DA_QK + DA_WIDTH])
    if layer:
        k_prev, v_prev = refs[-6:-4]
        k_ref[0:layer] = k_prev[...]
        v_ref[0:layer] = v_prev[...]


def _even_in(x, w_in_bf, wk_t_bf, layer, kv_stacks, nb, seq):
    n = x.shape[0]
    k_transposed = wk_t_bf is not None
    tm = _tile(seq if k_transposed else n, 512)
    in_specs = [_row_spec(tm, D_MODEL), _const_spec((D_MODEL, IN_COLS))]
    args = [x, w_in_bf]
    v_spec = lambda slabs: pl.BlockSpec((slabs, tm, DA_WIDTH), lambda i: (0, i, 0))
    if k_transposed:
        tps = seq // tm
        k_shape = (nb, DA_QK, seq)
        k_spec = lambda slabs: pl.BlockSpec((slabs, None, DA_QK, tm),
                                            lambda i: (0, i // tps, 0, i % tps))
        in_specs.append(_const_spec((DA_QK, D_MODEL)))
        args.append(wk_t_bf)
    else:
        k_shape = (n, DA_QK)
        k_spec = lambda slabs: pl.BlockSpec((slabs, tm, DA_QK), lambda i: (0, i, 0))
    outs = (jax.ShapeDtypeStruct((n, 4 * HG_WIDTH), F32), jax.ShapeDtypeStruct((n, DA_QK), F32),
            jax.ShapeDtypeStruct((layer + 1,) + k_shape, F32),
            jax.ShapeDtypeStruct((layer + 1, n, DA_WIDTH), F32))
    if layer:
        in_specs += [k_spec(layer), v_spec(layer)]
        args += list(kv_stacks)
    return pl.pallas_call(
        functools.partial(_even_in_kernel, k_transposed=k_transposed, layer=layer), out_shape=outs,
        grid=(n // tm,), in_specs=in_specs,
        out_specs=(_row_spec(tm, 4 * HG_WIDTH), _row_spec(tm, DA_QK), k_spec(layer + 1),
                   v_spec(layer + 1)),
        compiler_params=_params("parallel"), name="even_in")(*args)


def _pad_rows(x, rows):
    if x.shape[0] >= rows:
        return x
    return jnp.concatenate([x, jnp.zeros((rows - x.shape[0], x.shape[1]), x.dtype)], axis=0)


def _hgrn_kernel(h4_ref, s0_ref, lbl_ref, ng_ref, o_ref, sout_ref,
                 st_ref, q_s, k_s, b_s, u_s, *, layer, chunk, carry):
    tl = h4_ref.shape[0]
    nchunk = tl // chunk
    w = HG_WIDTH

    rows = [lbl_ref[i:i + 1, :] for i in range(N_EVEN)]
    mx = functools.reduce(jnp.maximum, rows)
    es = [jnp.exp(r - mx) for r in rows]
    den = functools.reduce(lambda a, b: a + b, es)
    sm = [e / den for e in es]
    cs = functools.reduce(lambda a, b: a + b, sm[:layer + 1])
    lb = cs - sm[0]

    hq = h4_ref[:, 0:w]
    hf = h4_ref[:, w:2 * w]
    q_s[...] = hq * _sigmoid(hq)
    f = lb + (1.0 - lb) * _sigmoid(hf)
    g = jnp.log(jnp.maximum(f, F_MIN))
    k_s[...] = 1.0 - f
    ri = lax.broadcasted_iota(jnp.int32, (tl, tl), 0)
    ci = lax.broadcasted_iota(jnp.int32, (tl, tl), 1)
    tril = jnp.where((ri // chunk == ci // chunk) & (ci <= ri), 1.0, 0.0).astype(BF16)
    b_s[...] = _dot_x3_lhs(tril, g)

    if carry:
        @pl.when(pl.program_id(1) == 0)
        def _():
            for h in range(HG_HEADS):
                st_ref[h] = s0_ref[0, h].T

    rowid = lax.broadcasted_iota(jnp.int32, (chunk, 1), 0)
    ng = ng_ref[...]

    def chunk_slices(c, h):
        rs = pl.ds(pl.multiple_of(c * chunk, chunk), chunk)
        cs_ = slice(h * HG_DK, (h + 1) * HG_DK)
        v_cols = slice(2 * w + h * HG_DK, 2 * w + (h + 1) * HG_DK)
        return rs, cs_, v_cols

    def state_free(c, _):
        for h in range(HG_HEADS):
            rs, cs_, v_cols = chunk_slices(c, h)
            q, k, b, v = q_s[rs, cs_], k_s[rs, cs_], b_s[rs, cs_], h4_ref[rs, v_cols]
            groups = [(g0, min(g0 + SUBLANES, chunk)) for g0 in range(0, chunk, SUBLANES)]
            o_g = [None] * len(groups)
            for s in range(chunk):
                for gi, (g0, g1) in enumerate(groups):
                    if g1 <= s:
                        continue
                    rel = b[g0:g1, :] - b[s:s + 1, :]
                    if g0 > s:
                        e = jnp.exp(rel)
                    else:
                        keep = rowid[g0:g1, :] >= s
                        e = jnp.where(keep, jnp.exp(jnp.where(keep, rel, 0.0)), 0.0)
                    col = jnp.sum(q[g0:g1, :] * e * k[s:s + 1, :], axis=-1, keepdims=True)
                    term = col * v[s:s + 1, :]
                    o_g[gi] = term if o_g[gi] is None else o_g[gi] + term
            o_ref[rs, cs_] = o_g[0] if len(o_g) == 1 else jnp.concatenate(o_g, axis=0)
            kd = _pad_rows((k * jnp.exp(b[chunk - 1:chunk, :] - b)).astype(BF16), MM_ROWS)
            vd = _pad_rows(v.astype(BF16), MM_ROWS)
            u_s[c * HG_HEADS + h] = _dot_tn(vd, kd)
        return 0

    def with_state(c, _):
        for h in range(HG_HEADS):
            rs, cs_, _ = chunk_slices(c, h)
            q, b = q_s[rs, cs_], b_s[rs, cs_]
            hg = h4_ref[rs, 3 * w + h * HG_DK:3 * w + (h + 1) * HG_DK]
            st = st_ref[h] if carry else s0_ref[c, h].T
            st_new = st * jnp.exp(b[chunk - 1:chunk, :]) + u_s[c * HG_HEADS + h]
            if carry:
                st_ref[h] = st_new
            else:
                sout_ref[c, h] = st_new.T
            qd = _pad_rows((q * jnp.exp(b)).astype(BF16), MM_ROWS)
            o = o_ref[rs, cs_] + _dot_nt(qd, st.astype(BF16))[:chunk]
            on = o * lax.rsqrt(jnp.mean(o * o, axis=-1, keepdims=True) + RMS_EPS) * ng
            o_ref[rs, cs_] = on * _sigmoid(hg)
        return 0

    lax.fori_loop(0, nchunk, state_free, 0, unroll=4)
    lax.fori_loop(0, nchunk, with_state, 0, unroll=True)

    if carry:
        @pl.when(pl.program_id(1) == pl.num_programs(1) - 1)
        def _():
            for h in range(HG_HEADS):
                sout_ref[0, h] = st_ref[h].T


def _hgrn(h4, s0_all, s_layer, lb_logits, norm_g, layer, nb, seq):
    n = h4.shape[0]
    chunk = math.gcd(seq, HG_CHUNK)
    carry = seq > chunk
    if carry:
        tl = _tile(seq, 256)
        grid = (nb, seq // tl)
        row_map = lambda b, l: (b * (seq // tl) + l, 0)
        st_map = lambda b, l: (b, 0, 0, 0)
        st_blk = (1, HG_HEADS, HG_DK, HG_DK)
        sem = ("parallel", "arbitrary")
    else:
        bt = _tile(nb, 8)
        tl = bt * seq
        grid = (nb // bt, 1)
        row_map = lambda b, l: (b, 0)
        st_map = lambda b, l: (b, 0, 0, 0)
        st_blk = (bt, HG_HEADS, HG_DK, HG_DK)
        sem = ("parallel", "arbitrary")
    kern = functools.partial(_hgrn_kernel, layer=layer, chunk=chunk, carry=carry)
    return pl.pallas_call(
        kern,
        out_shape=(jax.ShapeDtypeStruct((n, HG_WIDTH), F32),
                   jax.ShapeDtypeStruct((nb, HG_HEADS, HG_DK, HG_DK), F32)),
        grid=grid,
        in_specs=[pl.BlockSpec((tl, 4 * HG_WIDTH), row_map),
                  pl.BlockSpec((None,) + st_blk, lambda b, l: (s_layer, b, 0, 0, 0)),
                  pl.BlockSpec((N_EVEN, HG_WIDTH), lambda b, l: (0, 0)),
                  pl.BlockSpec((1, HG_DK), lambda b, l: (0, 0))],
        out_specs=(pl.BlockSpec((tl, HG_WIDTH), row_map), pl.BlockSpec(st_blk, st_map)),
        scratch_shapes=[pltpu.VMEM((HG_HEADS, HG_DK, HG_DK), F32),
                        pltpu.VMEM((tl, HG_WIDTH), F32), pltpu.VMEM((tl, HG_WIDTH), F32),
                        pltpu.VMEM((tl, HG_WIDTH), F32),
                        pltpu.VMEM((tl // chunk * HG_HEADS, HG_DK, HG_DK), F32)],
        compiler_params=_params(*sem), name="hgrn2")(h4, s0_all, lb_logits, norm_g)


def _lambda(lamp_ref):
    lam_init = lamp_ref[4:5, 0:1]
    l1 = jnp.sum(lamp_ref[0:1, :] * lamp_ref[1:2, :], axis=-1, keepdims=True)
    l2 = jnp.sum(lamp_ref[2:3, :] * lamp_ref[3:4, :], axis=-1, keepdims=True)
    return jnp.exp(l1) - jnp.exp(l2) + lam_init, lam_init


def _diff_finish(o0, o1, lam, lam_init, ng):
    o = o0 - lam * o1
    return o * lax.rsqrt(jnp.mean(o * o, axis=-1, keepdims=True) + RMS_EPS) * ng * (1.0 - lam_init)


def _lane_fold(x, op):
    parts = [x[:, c * LANES:(c + 1) * LANES] for c in range(x.shape[1] // LANES)]
    return functools.reduce(op, parts)


def _attn_prompt_kernel(q_ref, k_ref, v_ref, lamp_ref, ng_ref, o_ref, kb_s, vb_s, s_s, mp_s, lp_s,
                        acc_s):
    tq = q_ref.shape[0]
    i = pl.program_id(2)

    @pl.when(i == 0)
    def _():
        for jb in range(kb_s.shape[0]):
            kb_s[jb] = k_ref[:, jb * tq:(jb + 1) * tq].astype(BF16)
        vb_s[...] = v_ref[...].astype(BF16)

    lane = lax.broadcasted_iota(jnp.int32, (tq, DA_DV), 1)
    qs = q_ref[...] * (DA_DH ** -0.5)
    q2 = jnp.concatenate([jnp.where(lane < DA_DH, qs, 0.0), jnp.where(lane >= DA_DH, qs, 0.0)],
                         axis=0).astype(BF16)
    mp_s[...] = jnp.full(mp_s.shape, NEG_BIG, F32)

    def scores(j, masked):
        s = _dot(q2, kb_s[j])
        if masked:
            row = lax.broadcasted_iota(jnp.int32, (2 * tq, tq), 0) % tq
            col = lax.broadcasted_iota(jnp.int32, (2 * tq, tq), 1)
            s = jnp.where(col <= row, s, NEG_BIG)
        s_s[j] = s
        mp_s[...] = jnp.maximum(mp_s[...], _lane_fold(s, jnp.maximum))

    def pass1(j, c):
        scores(j, False)
        return c

    lax.fori_loop(0, i, pass1, 0)
    scores(i, True)
    m = jnp.max(mp_s[...], axis=-1, keepdims=True)
    lp_s[...] = jnp.zeros(lp_s.shape, F32)
    acc_s[...] = jnp.zeros(acc_s.shape, F32)

    def pass2(j, c):
        r0 = pl.multiple_of(j * tq, tq)
        p = jnp.exp(s_s[j] - m)
        lp_s[...] += _lane_fold(p, lambda a, b: a + b)
        acc_s[...] += _dot(p.astype(BF16), vb_s[pl.ds(r0, tq), :])
        return c

    lax.fori_loop(0, i + 1, pass2, 0)
    l = jnp.sum(lp_s[...], axis=-1, keepdims=True)
    lam, lam_init = _lambda(lamp_ref)
    o0 = acc_s[0:tq, :] / l[0:tq, :]
    o1 = acc_s[tq:2 * tq, :] / l[tq:2 * tq, :]
    o_ref[...] = _diff_finish(o0, o1, lam, lam_init, ng_ref[...])


def _attn_prompt(q, k_all, v_all, lamp, norm_g, layer, nb, seq):
    n = q.shape[0]
    tq = _tile(seq, 1024)
    nq = seq // tq
    return pl.pallas_call(
        _attn_prompt_kernel, out_shape=jax.ShapeDtypeStruct((n, DA_WIDTH), F32),
        grid=(nb, DA_HEADS, nq),
        in_specs=[pl.BlockSpec((tq, DA_DV), lambda b, h, i: (b * nq + i, h)),
                  pl.BlockSpec((None, None, DA_DV, seq), lambda b, h, i: (layer, b, h, 0)),
                  pl.BlockSpec((None, seq, DA_DV), lambda b, h, i: (layer, b, h)),
                  pl.BlockSpec((8, LANES), lambda b, h, i: (0, 0)),
                  pl.BlockSpec((1, DA_DV), lambda b, h, i: (0, 0))],
        out_specs=pl.BlockSpec((tq, DA_DV), lambda b, h, i: (b * nq + i, h)),
        scratch_shapes=[pltpu.VMEM((nq, DA_DV, tq), BF16), pltpu.VMEM((seq, DA_DV), BF16),
                        pltpu.VMEM((nq, 2 * tq, tq), F32), pltpu.VMEM((2 * tq, LANES), F32),
                        pltpu.VMEM((2 * tq, LANES), F32), pltpu.VMEM((2 * tq, DA_DV), F32)],
        compiler_params=_params("parallel", "parallel", "arbitrary"),
        name="attn_prompt")(q, k_all, v_all, lamp, norm_g)


def _attn_sample_kernel(pt_ref, q_ref, *refs, n_pages):
    kp = refs[:n_pages]
    vp = refs[n_pages:2 * n_pages]
    kn_ref, vn_ref, lamp_ref, ng_ref, o_ref, kt_s, v_s = refs[2 * n_pages:]
    t = q_ref.shape[0]
    nrow = 2 * DA_HEADS * t
    hrow = 2 * t

    qs = q_ref[...] * (DA_DH ** -0.5)
    qt = jnp.concatenate([qs] * (2 * DA_HEADS), axis=0)
    r = lax.broadcasted_iota(jnp.int32, (nrow, DA_QK), 0) // t
    c = lax.broadcasted_iota(jnp.int32, (nrow, DA_QK), 1) // DA_DH
    qbd = jnp.where(r == c, qt, 0.0).astype(BF16)

    pc = PAGE_SIZE * DA_HEADS
    for i in range(n_pages):
        kt_s[:, i * PAGE_SIZE:(i + 1) * PAGE_SIZE] = kp[i][...].astype(BF16)
        v_s[i * pc:(i + 1) * pc, :] = vp[i][...].astype(BF16)
    s_old = _dot(qbd, kt_s[...])
    kb = _pad_rows(kn_ref[...], PAGE_SIZE).astype(BF16)
    vb = _pad_rows(vn_ref[...], PAGE_SIZE).astype(BF16)
    r = lax.broadcasted_iota(jnp.int32, (nrow, PAGE_SIZE), 0) % t
    c = lax.broadcasted_iota(jnp.int32, (nrow, PAGE_SIZE), 1)
    s_new = jnp.where(c <= r, _dot_nt(qbd, kb), NEG_BIG)
    m = jnp.maximum(jnp.max(s_old, axis=-1, keepdims=True), jnp.max(s_new, axis=-1, keepdims=True))
    p_old = jnp.exp(s_old - m)
    p_new = jnp.exp(s_new - m)
    l = jnp.sum(p_old, axis=-1, keepdims=True) + jnp.sum(p_new, axis=-1, keepdims=True)
    pb_old = p_old.astype(BF16)
    pb_new = p_new.astype(BF16)

    spread = jnp.where(lax.broadcasted_iota(jnp.int32, (PAGE_SIZE, pc), 1) // DA_HEADS
                       == lax.broadcasted_iota(jnp.int32, (PAGE_SIZE, pc), 0), 1.0, 0.0).astype(BF16)
    own = (lax.broadcasted_iota(jnp.int32, (n_pages * nrow, pc), 1) % DA_HEADS
           == (lax.broadcasted_iota(jnp.int32, (n_pages * nrow, pc), 0) % nrow) // hrow)
    p_rows = jnp.concatenate([pb_old[:, i * PAGE_SIZE:(i + 1) * PAGE_SIZE] for i in range(n_pages)],
                             axis=0)
    pe = jnp.where(own, _dot(p_rows, spread), 0.0).astype(BF16)
    pe = jnp.concatenate([pe[i * nrow:(i + 1) * nrow, :] for i in range(n_pages)], axis=1)
    acc = _dot(pe, v_s[...])
    acc = acc + jnp.concatenate([_dot(pb_new[h * hrow:(h + 1) * hrow, :], vb[:, h * DA_DV:(h + 1) * DA_DV])
                                 for h in range(DA_HEADS)], axis=0)
    o = acc / l
    lam, lam_init = _lambda(lamp_ref)
    ng = ng_ref[...]
    for h in range(DA_HEADS):
        r0 = h * hrow
        o_ref[:, h * DA_DV:(h + 1) * DA_DV] = _diff_finish(o[r0:r0 + t, :], o[r0 + t:r0 + 2 * t, :],
                                                           lam, lam_init, ng)


def _attn_sample(q, k_new, v_new, cache_kt, cache_v, page_table, lamp, norm_g, layer, nb, seq):
    n = q.shape[0]
    n_pages = page_table.shape[1]
    row_map = lambda b, pt: (b, 0)
    new_map = lambda b, pt: (layer, b, 0)
    k_specs = [pl.BlockSpec((None, None, DA_QK, PAGE_SIZE),
                            lambda b, pt, i=i: (layer, pt[b * n_pages + i], 0, 0))
               for i in range(n_pages)]
    v_specs = [pl.BlockSpec((None, None, PAGE_SIZE * DA_HEADS, DA_DV),
                            lambda b, pt, i=i: (layer, pt[b * n_pages + i], 0, 0))
               for i in range(n_pages)]
    grid_spec = pltpu.PrefetchScalarGridSpec(
        num_scalar_prefetch=1, grid=(nb,),
        in_specs=[pl.BlockSpec((seq, DA_QK), row_map)] + k_specs + v_specs
        + [pl.BlockSpec((None, seq, DA_QK), new_map), pl.BlockSpec((None, seq, DA_WIDTH), new_map),
           pl.BlockSpec((8, LANES), lambda b, pt: (0, 0)),
           pl.BlockSpec((1, DA_DV), lambda b, pt: (0, 0))],
        out_specs=pl.BlockSpec((seq, DA_WIDTH), row_map),
        scratch_shapes=[pltpu.VMEM((DA_QK, n_pages * PAGE_SIZE), BF16),
                        pltpu.VMEM((n_pages * PAGE_SIZE * DA_HEADS, DA_DV), BF16)])
    return pl.pallas_call(
        functools.partial(_attn_sample_kernel, n_pages=n_pages),
        out_shape=jax.ShapeDtypeStruct((n, DA_WIDTH), F32),
        grid_spec=grid_spec, compiler_params=_params("parallel"),
        name="attn_sample")(page_table.reshape(-1), q, *([cache_kt] * n_pages),
                            *([cache_v] * n_pages), k_new, v_new, lamp, norm_g)


FF_CHUNK = 1024


def _mix_mlp_kernel(a1_ref, a2_ref, x_ref, wmix_ref, n1g_ref, n1b_ref, p_ref, up_ref, down_ref,
                    n2g_ref, n2b_ref, gate_ref, proj_ref, o_ref, *, even):
    if even:
        h = (_dot(a1_ref[...].astype(BF16), wmix_ref[0:HG_WIDTH, :])
             + _dot(a2_ref[...].astype(BF16), wmix_ref[HG_WIDTH:D_MODEL, :]))
    else:
        h = _dot((a1_ref[...] * a2_ref[...]).astype(BF16), wmix_ref[...])
    x = _layer_norm(ALPHA * x_ref[...] + h, n1g_ref[...], n1b_ref[...])
    xb = x.astype(BF16)
    acc = jnp.zeros(x.shape, F32)
    for c in range(D_FF // FF_CHUNK):
        cs = slice(c * FF_CHUNK, (c + 1) * FF_CHUNK)
        hid = jnp.maximum(_dot(xb, up_ref[:, cs]), 0.0)
        acc = acc + _dot((hid * hid).astype(BF16), down_ref[cs, :])
    x2 = _layer_norm(ALPHA * x + acc, n2g_ref[...], n2b_ref[...])
    gate = _sigmoid(_dot(x2.astype(BF16), gate_ref[...]))
    o_ref[...] = x2 + gate * _dot(p_ref[...].astype(BF16), proj_ref[...])


def _mix_mlp(a1, a2, x, wmix_bf, n1g, n1b, even, p_all, layer, up_bf, down_bf, n2g, n2b, gate_bf,
             proj_bf):
    n = x.shape[0]
    tm = _tile(n, 512)
    vec = pl.BlockSpec((1, D_MODEL), lambda i: (0, 0))
    return pl.pallas_call(
        functools.partial(_mix_mlp_kernel, even=even),
        out_shape=jax.ShapeDtypeStruct((n, D_MODEL), F32), grid=(n // tm,),
        in_specs=[_row_spec(tm, a1.shape[1]), _row_spec(tm, a2.shape[1]), _row_spec(tm, D_MODEL),
                  _const_spec((D_MODEL, D_MODEL)), vec, vec,
                  pl.BlockSpec((None, tm, PLE_DIM), lambda i: (layer, i, 0)),
                  _const_spec((D_MODEL, D_FF)), _const_spec((D_FF, D_MODEL)), vec, vec,
                  _const_spec((D_MODEL, D_MODEL)), _const_spec((PLE_DIM, D_MODEL))],
        out_specs=_row_spec(tm, D_MODEL),
        compiler_params=_params("parallel"), name="mix_mlp")(
            a1, a2, x, wmix_bf, n1g, n1b, p_all, up_bf, down_bf, n2g, n2b, gate_bf, proj_bf)


def _transpose_atom_grid(p):
    shape = p[0][0].shape
    sub = lax.broadcasted_iota(jnp.int32, shape, 0)
    half = lax.broadcasted_iota(jnp.int32, shape, 1) // (LANES // 2)
    s0 = sub & 1
    na, nc = len(p), len(p[0])

    def bit0(x):
        z = pltpu.roll(x, LANES // 2, 1)
        moved = jnp.where(s0 == 1, pltpu.roll(z, 1, 0), pltpu.roll(z, SUBLANES - 1, 0))
        return jnp.where(s0 == half, x, moved)

    p = [[bit0(t) for t in row] for row in p]
    for shift, cbit in ((2, 1), (4, 2)):
        up = (sub & shift) != 0
        for a in range(na):
            for c in range(nc):
                if c & cbit == 0:
                    lo, hi = p[a][c], p[a][c + cbit]
                    p[a][c] = jnp.where(up, pltpu.roll(hi, shift, 0), lo)
                    p[a][c + cbit] = jnp.where(up, hi, pltpu.roll(lo, SUBLANES - shift, 0))
    q = nc // na
    return [[p[c // q][a2 * q + c % q] for c in range(nc)] for a2 in range(na)]


def _tokens_to_chains(x, nseq, tt, out_ref):
    sub = LANES // nseq
    ntile = D_MODEL // LANES
    for sb in range(tt // sub):
        grids = []
        for b in range(nseq):
            r0 = b * tt + sb * sub
            grids.append(_transpose_atom_grid(
                [[x[r0 + a * SUBLANES:r0 + (a + 1) * SUBLANES, c * LANES:(c + 1) * LANES]
                  for c in range(ntile)] for a in range(sub // SUBLANES)]))
        for tau in range(ntile):
            tile = jnp.concatenate([g[a][tau] for g in grids for a in range(sub // SUBLANES)],
                                   axis=0).T
            out_ref[sb * sub + 2 * tau] = tile[0:RW_HEAD, :]
            out_ref[sb * sub + 2 * tau + 1] = tile[RW_HEAD:2 * RW_HEAD, :]


def _rwkv_pre_kernel(*refs, has_vfirst, chains, want_v_tokens):
    if chains:
        x_ref, x8_ref, sh_ref = refs[:3]
        refs = refs[3:]
    else:
        x_ref, xp_ref = refs[:2]
        refs = refs[2:]
    vf_ref = vt_o = v0_ref = v1_ref = v2_ref = None
    if has_vfirst:
        vf_ref, refs = refs[0], refs[1:]
    (mix_ref, wr_ref, wk_ref, wv_ref, w0_ref, w1_ref, w2_ref, a0_ref, a1_ref, a2_ref) = refs[:10]
    refs = refs[10:]
    if has_vfirst:
        v0_ref, v1_ref, v2_ref = refs[:3]
        refs = refs[3:]
    g1_ref, g2_ref, r_o, k_o, v_o, g_o, dec_o, a_o = refs[:8]
    if want_v_tokens:
        vt_o = refs[8]
    if chains:
        nseq, tt = x_ref.shape[0], x_ref.shape[1]
        x = x_ref[...].reshape(nseq * tt, D_MODEL)
        first = pl.program_id(0) == 0
        before = jnp.where(first, sh_ref[...], x8_ref[:, SUBLANES - 1, :])
        before = jnp.concatenate([jnp.broadcast_to(before[b:b + 1, :], (tt, D_MODEL))
                                  for b in range(nseq)], axis=0)
        rowid = lax.broadcasted_iota(jnp.int32, x.shape, 0)
        x_prev = jnp.where(rowid % tt == 0, before, pltpu.roll(x, 1, 0))
    else:
        x = x_ref[...]
        x_prev = xp_ref[...]
    xx = x_prev - x
    xr, xw, xk, xv, xa, xg = ((x + xx * mix_ref[m:m + 1, :]).astype(BF16) for m in range(6))
    r = _dot(xr, wr_ref[...])
    k = _dot(xk, wk_ref[...])
    v = _dot(xv, wv_ref[...])
    lw = w0_ref[...] + _dot(jnp.tanh(_dot(xw, w1_ref[...])).astype(BF16), w2_ref[...])
    w = -(jnp.maximum(-lw, 0.0) + jnp.log(1.0 + jnp.exp(-jnp.abs(lw)))) - 0.5
    if has_vfirst:
        mixv = _sigmoid(v0_ref[...] + _dot(_dot(xv, v1_ref[...]).astype(BF16), v2_ref[...]))
        vf = vf_ref[...].reshape(v.shape) if chains else vf_ref[...]
        v = v + (vf - v) * mixv
    a = _sigmoid(a0_ref[...] + _dot(_dot(xa, a1_ref[...]).astype(BF16), a2_ref[...]))
    g = _dot(_sigmoid(_dot(xg, g1_ref[...])).astype(BF16), g2_ref[...])
    dec = jnp.exp(-jnp.exp(w))
    if chains:
        g_o[...] = g.reshape(g_o.shape)
        if want_v_tokens:
            vt_o[...] = v.reshape(vt_o.shape)
        for val, ref in ((r, r_o), (k, k_o), (v, v_o), (dec, dec_o), (a, a_o)):
            _tokens_to_chains(val, nseq, tt, ref)
    else:
        r_o[...] = r
        k_o[...] = k
        v_o[...] = v
        g_o[...] = g
        dec_o[...] = dec
        a_o[...] = a
        if want_v_tokens:
            vt_o[...] = v


def _chain_layout_ok(nb, seq):
    return nb * RW_HEADS == LANES and LANES // nb == RW_HEADS and seq % RW_TBLK == 0


def _rwkv_pre(x, shift0, v_first, wts, nb, seq, want_v_tokens):
    n = x.shape[0]
    has_vfirst = v_first is not None
    chains = _chain_layout_ok(nb, seq)
    vec = pl.BlockSpec((1, D_MODEL), lambda i: (0, 0))
    if chains:
        tt = RW_TBLK
        tok3 = pl.BlockSpec((nb, tt, D_MODEL), lambda i: (0, i, 0))
        x3 = x.reshape(nb, seq, D_MODEL)
        r8 = tt // SUBLANES
        acts = [x3, x3, shift0]
        act_specs = [tok3,
                     pl.BlockSpec((nb, SUBLANES, D_MODEL), lambda i: (0, jnp.maximum(i * r8 - 1, 0), 0)),
                     pl.BlockSpec((nb, D_MODEL), lambda i: (0, 0))]
        if has_vfirst:
            acts.append(v_first.reshape(nb, seq, D_MODEL))
            act_specs.append(tok3)
        grid = (seq // tt,)
        chain_out = jax.ShapeDtypeStruct((seq, RW_HEAD, LANES), F32)
        chain_spec = pl.BlockSpec((tt, RW_HEAD, LANES), lambda i: (i, 0, 0))
        tok_out = jax.ShapeDtypeStruct((nb, seq, D_MODEL), F32)
        out_shape = [chain_out, chain_out, chain_out, tok_out, chain_out, chain_out]
        out_specs = [chain_spec, chain_spec, chain_spec, tok3, chain_spec, chain_spec]
        if want_v_tokens:
            out_shape.append(tok_out)
            out_specs.append(tok3)
    else:
        tm = _tile(n, 256)
        xs = x.reshape(nb, seq, D_MODEL)
        x_prev = jnp.concatenate([shift0[:, None, :], xs[:, :-1]], axis=1).reshape(n, D_MODEL)
        acts = [x, x_prev] + ([v_first] if has_vfirst else [])
        act_specs = [_row_spec(tm, D_MODEL)] * len(acts)
        grid = (n // tm,)
        out_shape = [jax.ShapeDtypeStruct((n, D_MODEL), F32)] * 6
        out_specs = [_row_spec(tm, D_MODEL)] * 6
    names = ['mix', 'w_r', 'w_k', 'w_v', 'w0', 'w1', 'w2', 'a0', 'a1', 'a2']
    if has_vfirst:
        names += ['v0', 'v1', 'v2']
    names += ['g1', 'g2']
    ws = [wts[nm] for nm in names]
    w_specs = [vec if w.shape == (1, D_MODEL) else _const_spec(w.shape) for w in ws]
    outs = pl.pallas_call(
        functools.partial(_rwkv_pre_kernel, has_vfirst=has_vfirst, chains=chains,
                          want_v_tokens=want_v_tokens and chains),
        out_shape=tuple(out_shape), grid=grid,
        in_specs=act_specs + w_specs, out_specs=tuple(out_specs),
        compiler_params=_params("parallel"), name="rwkv_pre")(*acts, *ws)
    outs = list(outs)
    if chains:
        outs[3] = outs[3].reshape(n, D_MODEL)
        if want_v_tokens:
            outs[6] = outs[6].reshape(n, D_MODEL)
    elif want_v_tokens:
        outs.append(outs[2])
    return outs


RW_TBLK = 32


RW_IN = 5


def _rwkv_token(ins, v_rows, prm, s_s, y_s):
    r, k, v, dec, ag = ins
    kk_p, ka_p, rk_p, lg_p, lb_p = prm
    kk = k * kk_p
    kk = kk / jnp.maximum(jnp.sqrt(jnp.sum(kk * kk, axis=0, keepdims=True)), 1e-12)
    a = -kk
    b = kk * ag
    k2 = k * (1.0 + (ag - 1.0) * ka_p)
    wr = dec * r
    br = jnp.sum(b * r, axis=0, keepdims=True)
    kr = jnp.sum(k2 * r, axis=0, keepdims=True)

    def rows(i8, _):
        i0 = pl.multiple_of(i8 * SUBLANES, SUBLANES)
        v8 = v_rows(i0)
        ys = []
        for ii in range(SUBLANES):
            si = s_s[i0 + ii]
            sa = jnp.sum(si * a, axis=0, keepdims=True)
            qs = jnp.sum(si * wr, axis=0, keepdims=True)
            vi = v8[ii:ii + 1, :]
            s_s[i0 + ii] = si * dec + sa * b + vi * k2
            ys.append(qs + sa * br + vi * kr)
        y_s[pl.ds(i0, SUBLANES), :] = jnp.concatenate(ys, axis=0)
        return 0

    lax.fori_loop(0, RW_HEAD // SUBLANES, rows, 0)
    y = y_s[...]
    mu = jnp.mean(y, axis=0, keepdims=True)
    yc = y - mu
    var = jnp.mean(yc * yc, axis=0, keepdims=True)
    yn = yc * lax.rsqrt(var + GN_EPS) * lg_p + lb_p
    return yn + jnp.sum(r * k2 * rk_p, axis=0, keepdims=True) * v


RW_IBLK = 32
RW_ACC = 2


def _rwkv_token_keymajor(ins, v_rows, prm, s_s, y_s, vec_s, g_s):
    r, k, v, dec, ag = ins
    kk_p, ka_p, rk_p, lg_p, lb_p = prm
    kk = k * kk_p
    kk = kk / jnp.maximum(jnp.sqrt(jnp.sum(kk * kk, axis=0, keepdims=True)), 1e-12)
    b = kk * ag
    k2 = k * (1.0 + (ag - 1.0) * ka_p)
    br = jnp.sum(b * r, axis=0, keepdims=True)
    kr = jnp.sum(k2 * r, axis=0, keepdims=True)
    g_old = g_s[...]
    g_new = g_old * dec
    g_s[...] = g_new
    g_inv = 1.0 / g_new
    for idx, val in enumerate((-kk * g_old, r * g_new, b * g_inv, k2 * g_inv)):
        vec_s[idx] = val
    row = lambda idx, j: vec_s[idx, j:j + 1, :]

    for p in range(RW_HEAD // RW_IBLK):
        rs = pl.ds(p * RW_IBLK, RW_IBLK)
        sa = [None] * RW_ACC
        qs = [None] * RW_ACC
        for j in range(RW_HEAD):
            sj = s_s[j, rs, :]
            c = j % RW_ACC
            sa[c] = sj * row(0, j) if sa[c] is None else sa[c] + sj * row(0, j)
            qs[c] = sj * row(1, j) if qs[c] is None else qs[c] + sj * row(1, j)
        sa = functools.reduce(lambda x, y: x + y, sa)
        qs = functools.reduce(lambda x, y: x + y, qs)
        vi = v_rows(rs)
        for j in range(RW_HEAD):
            s_s[j, rs, :] = s_s[j, rs, :] + sa * row(2, j) + vi * row(3, j)
        y_s[rs, :] = qs + sa * br + vi * kr
    y = y_s[...]
    mu = jnp.mean(y, axis=0, keepdims=True)
    yc = y - mu
    var = jnp.mean(yc * yc, axis=0, keepdims=True)
    yn = yc * lax.rsqrt(var + GN_EPS) * lg_p + lb_p
    return yn + jnp.sum(r * k2 * rk_p, axis=0, keepdims=True) * v


def _rwkv_scan_kernel(r_ref, k_ref, v_ref, dec_ref, a_ref, kk_ref, ka_ref, rk_ref, lg_ref, lb_ref,
                      s0_ref, y_ref, sout_ref, s_s, y_s):
    tb = pl.program_id(1)
    in_refs = (r_ref, k_ref, v_ref, dec_ref, a_ref)
    prm_refs = (kk_ref, ka_ref, rk_ref, lg_ref, lb_ref)

    @pl.when(tb == 0)
    def _():
        s_s[...] = s0_ref[...]

    def step(t, _):
        y_ref[t] = _rwkv_token([ref[t] for ref in in_refs],
                               lambda i0: v_ref[t, pl.ds(i0, SUBLANES), :],
                               [p[...] for p in prm_refs], s_s, y_s)
        return 0

    lax.fori_loop(0, r_ref.shape[0], step, 0)

    @pl.when(tb == pl.num_programs(1) - 1)
    def _():
        sout_ref[...] = s_s[...]


def _rwkv_scan_tokens_kernel(r_ref, k_ref, v_ref, dec_ref, a_ref, kk_ref, ka_ref, rk_ref, lg_ref,
                             lb_ref, s0_ref, y_ref, sout_ref, s_s, y_s, yt_s, vec_s, g_s):
    tb = pl.program_id(1)
    nseq, tt = y_ref.shape[0], y_ref.shape[1]
    sub = LANES // nseq
    in_refs = (r_ref, k_ref, v_ref, dec_ref, a_ref)
    prm_refs = (kk_ref, ka_ref, rk_ref, lg_ref, lb_ref)

    @pl.when(tb == 0)
    def _():
        s_s[...] = s0_ref[...]
        g_s[...] = jnp.ones(g_s.shape, F32)

    def sub_block(sb, _):
        t0 = pl.multiple_of(sb * sub, sub)

        def step(t, _):
            yt_s[t] = _rwkv_token_keymajor([ref[t0 + t] for ref in in_refs],
                                           lambda rs: v_ref[t0 + t, rs, :],
                                           [p[...] for p in prm_refs], s_s, y_s, vec_s, g_s)
            return 0

        lax.fori_loop(0, sub, step, 0)
        for j in range(RW_HEAD):
            s_s[j] = s_s[j] * g_s[j:j + 1, :]
        g_s[...] = jnp.ones(g_s.shape, F32)
        ntile = D_MODEL // LANES
        na = sub // SUBLANES
        tiles = [jnp.concatenate([yt_s[2 * tau], yt_s[2 * tau + 1]], axis=0).T for tau in range(ntile)]
        for b in range(nseq):
            grid = _transpose_atom_grid(
                [[tiles[tau][b * sub + a * SUBLANES:b * sub + (a + 1) * SUBLANES, :]
                  for tau in range(ntile)] for a in range(na)])
            for a in range(na):
                for c in range(ntile):
                    y_ref[b, pl.ds(t0 + a * SUBLANES, SUBLANES), c * LANES:(c + 1) * LANES] = grid[a][c]
        return 0

    lax.fori_loop(0, tt // sub, sub_block, 0)

    @pl.when(tb == pl.num_programs(1) - 1)
    def _():
        sout_ref[...] = s_s[...]


def _rwkv_scan(r, k, v, dec, a, s0_all, s_layer, wts, nb, seq):
    nl = s0_all.shape[0]
    n = nb * seq
    hd = RW_HEAD
    batch_lanes = nb % LANES == 0
    if batch_lanes:
        nbg = nb // LANES
        groups, chains = RW_HEADS * nbg, LANES
        to_l = lambda x: (x.reshape(nbg, LANES, seq, RW_HEADS, hd).transpose(0, 3, 2, 4, 1)
                          .reshape(groups, seq, hd, chains))
        from_l = lambda y: (y.reshape(nbg, RW_HEADS, seq, hd, LANES).transpose(0, 4, 2, 1, 3)
                            .reshape(n, D_MODEL))
        par = lambda p: jnp.broadcast_to(p.reshape(1, RW_HEADS, hd, 1),
                                         (nbg, RW_HEADS, hd, chains)).reshape(groups, hd, chains)
        st_in = lambda s: (s.reshape(nl, nbg, LANES, RW_HEADS, hd, hd).transpose(0, 1, 3, 4, 5, 2)
                           .reshape(nl, groups, hd, hd, chains))
        st_out = lambda s: (s.reshape(nbg, RW_HEADS, hd, hd, LANES).transpose(0, 4, 1, 2, 3)
                            .reshape(nb, RW_HEADS, hd, hd))
    else:
        groups, chains = 1, nb * RW_HEADS
        to_l = lambda x: (x.reshape(nb, seq, RW_HEADS, hd).transpose(1, 3, 0, 2)
                          .reshape(1, seq, hd, chains))
        from_l = lambda y: (y.reshape(seq, hd, nb, RW_HEADS).transpose(2, 0, 3, 1)
                            .reshape(n, D_MODEL))
        par = lambda p: jnp.broadcast_to(p.reshape(RW_HEADS, hd).T.reshape(1, hd, 1, RW_HEADS),
                                         (1, hd, nb, RW_HEADS)).reshape(1, hd, chains)
        st_in = lambda s: s.transpose(0, 3, 4, 1, 2).reshape(nl, 1, hd, hd, chains)
        st_out = lambda s: s.reshape(hd, hd, nb, RW_HEADS).transpose(2, 3, 0, 1)
    tt = _tile(seq, RW_TBLK)
    prm = pl.BlockSpec((None, hd, chains), lambda g, t: (g, 0, 0))
    st = pl.BlockSpec((None, hd, hd, chains), lambda g, t: (g, 0, 0, 0))
    st0 = pl.BlockSpec((None, None, hd, hd, chains), lambda g, t: (s_layer, g, 0, 0, 0))
    prms = [par(wts[nm]) for nm in ('k_k', 'k_a', 'r_k', 'lnx_g', 'lnx_b')]
    if _chain_layout_ok(nb, seq):
        tok = pl.BlockSpec((nb, tt, D_MODEL), lambda g, t: (0, t, 0))
        act = pl.BlockSpec((tt, hd, chains), lambda g, t: (t, 0, 0))
        y, s_new = pl.pallas_call(
            _rwkv_scan_tokens_kernel,
            out_shape=(jax.ShapeDtypeStruct((nb, seq, D_MODEL), F32),
                       jax.ShapeDtypeStruct((1, hd, hd, chains), F32)),
            grid=(1, seq // tt),
            in_specs=[act] * RW_IN + [prm] * 5 + [st0],
            out_specs=(tok, st),
            scratch_shapes=[pltpu.VMEM((hd, hd, chains), F32), pltpu.VMEM((hd, chains), F32),
                            pltpu.VMEM((RW_HEADS, hd, chains), F32),
                            pltpu.VMEM((4, hd, chains), F32), pltpu.VMEM((hd, chains), F32)],
            compiler_params=_params("parallel", "arbitrary"),
            name="rwkv_scan_tokens")(r, k, v, dec, a, *prms, st_in(s0_all).swapaxes(2, 3))
        return y.reshape(n, D_MODEL), st_out(s_new.swapaxes(1, 2))
    act = pl.BlockSpec((None, tt, hd, chains), lambda g, t: (g, t, 0, 0))
    y, s_new = pl.pallas_call(
        _rwkv_scan_kernel,
        out_shape=(jax.ShapeDtypeStruct((groups, seq, hd, chains), F32),
                   jax.ShapeDtypeStruct((groups, hd, hd, chains), F32)),
        grid=(groups, seq // tt),
        in_specs=[act] * RW_IN + [prm] * 5 + [st0],
        out_specs=(act, st),
        scratch_shapes=[pltpu.VMEM((hd, hd, chains), F32), pltpu.VMEM((hd, chains), F32)],
        compiler_params=_params("parallel", "arbitrary"),
        name="rwkv_scan")(to_l(r), to_l(k), to_l(v), to_l(dec), to_l(a), *prms, st_in(s0_all))
    return from_l(y), st_out(s_new)


def _trunk(x3, p4, hg_s0, wkv_s0, shift0, cache_k, cache_v, page_table, W):
    nb, seq, _ = x3.shape
    n = nb * seq
    x = x3.reshape(n, D_MODEL)
    new_hg, new_wkv, new_shift = [], [], []
    kv = None
    v_first = None
    for i in range(DEPTH):
        j = i // 2
        if i % 2 == 0:
            wk_t = W['wk_t'][j] if page_table is None else None
            h4, q, k_all, v_all = _even_in(x, W['w_in_even'][j], wk_t, j, kv, nb, seq)
            kv = (k_all, v_all)
            oh, s_hg = _hgrn(h4, hg_s0, min(j, hg_s0.shape[0] - 1), W['hg_lb_logits'],
                             W['hg_norm_g'][j], j, nb, seq)
            if page_table is None:
                od = _attn_prompt(q, k_all, v_all, W['lamp'][j], W['da_norm_g'][j], j, nb, seq)
            else:
                od = _attn_sample(q, k_all, v_all, cache_k, cache_v, page_table, W['lamp'][j],
                                  W['da_norm_g'][j], j, nb, seq)
            mix = (oh, od, W['w_out_even'][j])
            new_hg.append(s_hg)
        else:
            need_vf = j == 0 and N_ODD > 1
            outs = _rwkv_pre(x, shift0[j], v_first, W['rw'][j], nb, seq, need_vf)
            r, k, v, g, dec, a = outs[:6]
            if need_vf:
                v_first = outs[6]
            y, s_wkv = _rwkv_scan(r, k, v, dec, a, wkv_s0, min(j, wkv_s0.shape[0] - 1),
                                  W['rw_scan'][j], nb, seq)
            new_shift.append(x.reshape(nb, seq, D_MODEL)[:, -1])
            new_wkv.append(s_wkv)
            mix = (y, g, W['rw_w_o'][j])
        x = _mix_mlp(mix[0], mix[1], x, mix[2], W['ln1_g'][i], W['ln1_b'][i], i % 2 == 0,
                     p4.reshape(DEPTH, n, PLE_DIM), i, W['mlp_up'][i], W['mlp_down'][i],
                     W['ln2_g'][i], W['ln2_b'][i], W['ple_gate'][i], W['ple_proj'][i])
    if page_table is None:
        k_out = kv[0].reshape(N_EVEN, nb, DA_HEADS, 2, DA_DH, seq).transpose(0, 1, 5, 2, 3, 4)
    else:
        k_out = kv[0].reshape(N_EVEN, nb, seq, DA_HEADS, 2, DA_DH)
    return (x.reshape(nb, seq, D_MODEL), k_out,
            kv[1].reshape(N_EVEN, nb, seq, DA_HEADS, DA_DV), jnp.stack(new_hg),
            jnp.stack(new_wkv), jnp.stack(new_shift))


def kernel(x_prompt, x_sample, cache_k, cache_v, state_hgrn, state_wkv, state_shift, page_table,
           p_prompt, p_sample, w_in_even, w_out_even, hg_lb_logits, hg_norm_g,
           da_lam_q1, da_lam_k1, da_lam_q2, da_lam_k2, da_norm_g,
           rw_mix, rw_w_r, rw_w_k, rw_w_v, rw_w_o, rw_w0, rw_w1, rw_w2,
           rw_a0, rw_a1, rw_a2, rw_v0, rw_v1, rw_v2, rw_g1, rw_g2,
           rw_k_k, rw_k_a, rw_r_k, rw_lnx_g, rw_lnx_b,
           ln1_g, ln1_b, ln2_g, ln2_b, mlp_up, mlp_down, ple_proj, ple_gate):
    bf = lambda w: w.astype(BF16)
    vec = lambda w: w.reshape(1, -1)

    def lam_rows(j):
        pad = lambda a: jnp.pad(a.astype(F32), (0, LANES - a.shape[0]))
        lam_init = 0.8 - 0.6 * math.exp(-0.3 * (2 * j))
        rows = [pad(da_lam_q1[j]), pad(da_lam_k1[j]), pad(da_lam_q2[j]), pad(da_lam_k2[j]),
                jnp.full((LANES,), lam_init, F32)]
        return jnp.stack(rows + [jnp.zeros((LANES,), F32)] * 3)

    rw = []
    for j in range(N_ODD):
        d = dict(mix=rw_mix[j], w_r=bf(rw_w_r[j]), w_k=bf(rw_w_k[j]), w_v=bf(rw_w_v[j]),
                 w0=vec(rw_w0[j]), w1=bf(rw_w1[j]), w2=bf(rw_w2[j]),
                 a0=vec(rw_a0[j]), a1=bf(rw_a1[j]), a2=bf(rw_a2[j]),
                 g1=bf(rw_g1[j]), g2=bf(rw_g2[j]))
        if j > 0:
            d.update(v0=vec(rw_v0[j - 1]), v1=bf(rw_v1[j - 1]), v2=bf(rw_v2[j - 1]))
        rw.append(d)
    W = dict(
        w_in_even=[bf(w_in_even[j]) for j in range(N_EVEN)],
        wk_t=[bf(w_in_even[j][:, 4 * HG_WIDTH + DA_QK:4 * HG_WIDTH + 2 * DA_QK].T) for j in range(N_EVEN)],
        w_out_even=[bf(w_out_even[j]) for j in range(N_EVEN)],
        hg_lb_logits=hg_lb_logits,
        hg_norm_g=[vec(hg_norm_g[j]) for j in range(N_EVEN)],
        da_norm_g=[vec(da_norm_g[j]) for j in range(N_EVEN)],
        lamp=[lam_rows(j) for j in range(N_EVEN)],
        rw=rw,
        rw_scan=[dict(k_k=rw_k_k[j], k_a=rw_k_a[j], r_k=rw_r_k[j], lnx_g=rw_lnx_g[j],
                      lnx_b=rw_lnx_b[j]) for j in range(N_ODD)],
        rw_w_o=[bf(rw_w_o[j]) for j in range(N_ODD)],
        ln1_g=[vec(ln1_g[i]) for i in range(DEPTH)], ln1_b=[vec(ln1_b[i]) for i in range(DEPTH)],
        ln2_g=[vec(ln2_g[i]) for i in range(DEPTH)], ln2_b=[vec(ln2_b[i]) for i in range(DEPTH)],
        mlp_up=[bf(mlp_up[i]) for i in range(DEPTH)], mlp_down=[bf(mlp_down[i]) for i in range(DEPTH)],
        ple_gate=[bf(ple_gate[i]) for i in range(DEPTH)], ple_proj=[bf(ple_proj[i]) for i in range(DEPTH)],
    )
    nbp = x_prompt.shape[0]
    dt = x_prompt.dtype
    hg0 = jnp.zeros((1, nbp, HG_HEADS, HG_DK, HG_DK), dt)
    wkv0 = jnp.zeros((1, nbp, RW_HEADS, RW_HEAD, RW_HEAD), dt)
    sh0 = jnp.zeros((N_ODD, nbp, D_MODEL), dt)
    y_p, k_p, v_p, hg_p, wkv_p, sh_p = _trunk(x_prompt, p_prompt, hg0, wkv0, sh0, None, None, None, W)
    ckt = jnp.transpose(cache_k, (0, 1, 3, 4, 5, 2)).reshape(
        cache_k.shape[0], cache_k.shape[1], DA_QK, PAGE_SIZE)
    cv = cache_v.reshape(cache_v.shape[0], cache_v.shape[1], PAGE_SIZE * DA_HEADS, DA_DV)
    y_s, k_s, v_s, hg_s, wkv_s, sh_s = _trunk(x_sample, p_sample, state_hgrn, state_wkv, state_shift,
                                              ckt, cv, page_table, W)
    return (y_p, y_s, k_p, v_p, k_s, v_s, hg_p, hg_s, wkv_p, wkv_s, sh_p, sh_s)
```
